```python
import jax, jax.numpy as jnp
from jax import lax
import numpy as np

D_MODEL = 1024
BATCH = 1
SEQ = 16384
DEPTH = 1

GRID_W = 64
CTX_LEN = 256
LN_EPS = 1e-5
DEEPNORM_ALPHA = (2.0 * DEPTH) ** 0.25
DEEPNORM_BETA = (8.0 * DEPTH) ** -0.25

SSD_EXPAND = 2
D_INNER = SSD_EXPAND * D_MODEL
SSD_HEAD_DIM = 64
SSD_HEADS = D_INNER // SSD_HEAD_DIM
SSD_GROUPS = 4
SSD_STATE = 128
SSD_CHUNK = 128
CONV_W = 5
CONV_CH = D_INNER + 2 * SSD_GROUPS * SSD_STATE

ATTN_HEAD_DIM = 64
ATTN_Q_HEADS = 16
ATTN_KV_HEADS = 4
GQA_GROUP = ATTN_Q_HEADS // ATTN_KV_HEADS
ATTN_WIDTH = ATTN_Q_HEADS * ATTN_HEAD_DIM
KV_WIDTH = ATTN_KV_HEADS * ATTN_HEAD_DIM
WINDOW = 128
ATTN_BLOCK = 128
ATTN_SCALE = ATTN_HEAD_DIM ** -0.5
ROPE_BASE = 10000.0
ROPE_AXIS_DIM = ATTN_HEAD_DIM // 2

N_IN = D_INNER + CONV_CH + 2 * SSD_HEADS + ATTN_WIDTH + 2 * KV_WIDTH + 2 * D_MODEL

N_EXPERTS = 16
EXPERT_FF = 1024
CAPACITY_FACTOR = 2

kernel_name = "hybrid_ssd_swa_ecmoe_dit_block"


def layer_norm(x):
    xf = x.astype(jnp.float32)
    mu = jnp.mean(xf, -1, keepdims=True)
    var = jnp.mean(jnp.square(xf - mu), -1, keepdims=True)
    return ((xf - mu) * lax.rsqrt(var + LN_EPS)).astype(x.dtype)


def layer_norm_affine(x, g, b):
    xf = x.astype(jnp.float32)
    mu = jnp.mean(xf, -1, keepdims=True)
    var = jnp.mean(jnp.square(xf - mu), -1, keepdims=True)
    y = (xf - mu) * lax.rsqrt(var + LN_EPS) * g.astype(jnp.float32) + b.astype(jnp.float32)
    return y.astype(x.dtype)


def rms_norm(xf, g):
    return xf * lax.rsqrt(jnp.mean(jnp.square(xf), -1, keepdims=True) + LN_EPS) * g.astype(jnp.float32)


def modulate(h, shift, scale):
    return h * (1.0 + scale) + shift


def axial_rope(rows, dtype):
    row = jnp.repeat(jnp.arange(rows, dtype=jnp.float32), GRID_W)
    col = jnp.tile(jnp.arange(GRID_W, dtype=jnp.float32), rows)
    inv_freq = ROPE_BASE ** (-jnp.arange(0, ROPE_AXIS_DIM, 2, dtype=jnp.float32) / ROPE_AXIS_DIM)
    ang_r = row[:, None] * inv_freq[None, :]
    ang_c = col[:, None] * inv_freq[None, :]
    ang = jnp.concatenate([ang_r, ang_r, ang_c, ang_c], -1)
    return (jnp.cos(ang).astype(dtype)[None, :, None, :],
            jnp.sin(ang).astype(dtype)[None, :, None, :])


def apply_rope(x, cos, sin):
    x1, x2, x3, x4 = jnp.split(x, 4, axis=-1)
    rot = jnp.concatenate([-x2, x1, -x4, x3], -1)
    return x * cos + rot * sin


def split_projection(p):
    sizes = (D_INNER, CONV_CH, 2 * SSD_HEADS, ATTN_WIDTH, KV_WIDTH, KV_WIDTH, 2 * D_MODEL)
    points = []
    acc = 0
    for s in sizes[:-1]:
        acc += s
        points.append(acc)
    return jnp.split(p, points, axis=-1)


def depthwise_conv_centred(x, w, b):
    y = lax.conv_general_dilated(
        x, w[:, None, :], window_strides=(1,),
        padding=[(CONV_W // 2, CONV_W // 2)],
        dimension_numbers=("NWC", "WIO", "NWC"),
        feature_group_count=x.shape[-1])
    return y + b


def ssd_chunked(xdt, da, bm, cm, init_state):
    b, L, H, P = xdt.shape
    G, N = bm.shape[2], bm.shape[3]
    R = H // G
    Q = SSD_CHUNK
    nc = L // Q
    xdt = xdt.astype(jnp.float32).reshape(b, nc, Q, G, R, P)
    da = da.astype(jnp.float32).reshape(b, nc, Q, G, R)
    bm = bm.astype(jnp.float32).reshape(b, nc, Q, G, N)
    cm = cm.astype(jnp.float32).reshape(b, nc, Q, G, N)
    a_cum = jnp.cumsum(da, axis=2)
    seg = a_cum[:, :, :, None] - a_cum[:, :, None, :]
    lower = jnp.tril(jnp.ones((Q, Q), dtype=bool))[:, :, None, None]
    decay = jnp.exp(jnp.where(lower, seg, -jnp.inf))
    lmat = jnp.einsum("bcign,bcjgn->bcijg", cm, bm)[..., None] * decay
    y_diag = jnp.einsum("bcijgr,bcjgrp->bcigrp", lmat, xdt)
    decay_end = jnp.exp(a_cum[:, :, -1:] - a_cum)
    chunk_states = jnp.einsum("bcjgn,bcjgr,bcjgrp->bcgrpn", bm, decay_end, xdt)
    chunk_decay = jnp.exp(a_cum[:, :, -1])

    def step(s, inp):
        dec, st = inp
        return s * dec[..., None, None] + st, s

    init = init_state.astype(jnp.float32).reshape(b, G, R, P, N)
    final, s_prev = lax.scan(step, init, (jnp.moveaxis(chunk_decay, 1, 0),
                                          jnp.moveaxis(chunk_states, 1, 0)))
    s_prev = jnp.moveaxis(s_prev, 0, 1)
    y_off = jnp.einsum("bcign,bcigr,bcgrpn->bcigrp", cm, jnp.exp(a_cum), s_prev)
    y = (y_diag + y_off).reshape(b, L, H, P)
    return y, final.reshape(b, H, P, N)


def ssd_core(xbc, dt_raw, a_log, dt_bias, state_f, state_b):
    b, L, _ = xbc.shape
    xs, bm, cm = jnp.split(xbc, [D_INNER, D_INNER + SSD_GROUPS * SSD_STATE], axis=-1)
    xs = xs.reshape(b, L, SSD_HEADS, SSD_HEAD_DIM)
    bm = bm.reshape(b, L, SSD_GROUPS, SSD_STATE)
    cm = cm.reshape(b, L, SSD_GROUPS, SSD_STATE)
    dt = jax.nn.softplus(dt_raw.reshape(b, L, 2, SSD_HEADS).astype(jnp.float32)
                         + dt_bias.astype(jnp.float32))
    da = dt * (-jnp.exp(a_log.astype(jnp.float32)))
    xdt = xs.astype(jnp.float32)[:, :, None] * dt[..., None]
    y_f, fin_f = ssd_chunked(xdt[:, :, 0], da[:, :, 0], bm, cm, state_f)
    y_b, fin_b = ssd_chunked(jnp.flip(xdt[:, :, 1], 1), jnp.flip(da[:, :, 1], 1),
                             jnp.flip(bm, 1), jnp.flip(cm, 1), state_b)
    return y_f + jnp.flip(y_b, 1), xs, fin_f, fin_b


def ssd_output(y, xs, z, d_skip, norm_w):
    b, L = z.shape[:2]
    y = y + d_skip.astype(jnp.float32)[:, None] * xs.astype(jnp.float32)
    y = y.reshape(b, L, D_INNER) * jax.nn.silu(z.astype(jnp.float32))
    return rms_norm(y, norm_w).astype(z.dtype)


def windowed_attention(q, k, v, k_ctx, v_ctx, sink):
    b, L = q.shape[:2]
    nblk = L // ATTN_BLOCK
    span = ATTN_BLOCK + 2 * WINDOW
    kp = jnp.pad(k, ((0, 0), (WINDOW, WINDOW), (0, 0), (0, 0)))
    vp = jnp.pad(v, ((0, 0), (WINDOW, WINDOW), (0, 0), (0, 0)))
    qg = q.reshape(b, L, ATTN_KV_HEADS, GQA_GROUP, ATTN_HEAD_DIM) * ATTN_SCALE
    q_off = jnp.arange(ATTN_BLOCK)
    k_off = jnp.arange(span)
    rel = (k_off[None, :] - WINDOW) - q_off[:, None]
    s_sink = jnp.broadcast_to(
        sink.astype(jnp.float32).reshape(ATTN_KV_HEADS, GQA_GROUP, 1, 1),
        (b, ATTN_KV_HEADS, GQA_GROUP, ATTN_BLOCK, 1))

    def block(i):
        start = i * ATTN_BLOCK
        qb = lax.dynamic_slice_in_dim(qg, start, ATTN_BLOCK, axis=1)
        kb = lax.dynamic_slice_in_dim(kp, start, span, axis=1)
        vb = lax.dynamic_slice_in_dim(vp, start, span, axis=1)
        key_pos = start - WINDOW + k_off
        valid = (jnp.abs(rel) <= WINDOW) & ((key_pos >= 0) & (key_pos < L))[None, :]
        s_loc = jnp.einsum("bqkrd,bskd->bkrqs", qb, kb).astype(jnp.float32)
        s_loc = jnp.where(valid, s_loc, -jnp.inf)
        s_ctx = jnp.einsum("bqkrd,bskd->bkrqs", qb, k_ctx).astype(jnp.float32)
        p = jax.nn.softmax(jnp.concatenate([s_loc, s_ctx, s_sink], -1), axis=-1).astype(v.dtype)
        o = (jnp.einsum("bkrqs,bskd->bqkrd", p[..., :span], vb)
             + jnp.einsum("bkrqs,bskd->bqkrd", p[..., span:-1], v_ctx))
        return o.reshape(b, ATTN_BLOCK, ATTN_WIDTH)

    out = lax.map(block, jnp.arange(nblk))
    return jnp.moveaxis(out, 0, 1).reshape(b, L, ATTN_WIDTH)


def context_attention(q, k, v, sink):
    b, S = q.shape[:2]
    qg = q.reshape(b, S, ATTN_KV_HEADS, GQA_GROUP, ATTN_HEAD_DIM) * ATTN_SCALE
    s = jnp.einsum("bqkrd,bskd->bkrqs", qg, k).astype(jnp.float32)
    s_sink = jnp.broadcast_to(
        sink.astype(jnp.float32).reshape(ATTN_KV_HEADS, GQA_GROUP, 1, 1),
        (b, ATTN_KV_HEADS, GQA_GROUP, S, 1))
    p = jax.nn.softmax(jnp.concatenate([s, s_sink], -1), axis=-1)[..., :-1].astype(v.dtype)
    o = jnp.einsum("bkrqs,bskd->bqkrd", p, v)
    return o.reshape(b, S, ATTN_WIDTH)


def merge_branches(y_ssd, y_attn, gates, w_ssd_br, w_attn_br, w_o, b_o):
    g_ssd, g_attn = jnp.split(jax.nn.sigmoid(gates), 2, axis=-1)
    m = g_ssd * (y_ssd @ w_ssd_br) + g_attn * (y_attn @ w_attn_br)
    return m @ w_o + b_o


def token_mixer(h, hc, cos, sin, w_in, conv_w, conv_b, a_log, dt_bias, d_skip, ssd_norm_w,
                attn_sink, w_ssd_br, w_attn_br, w_o, b_o, with_ctx_out):
    b, L, _ = h.shape
    S = hc.shape[1]
    zc, xbcc, dtc, qc, kc, vc, gc = split_projection(hc @ w_in)
    xbcc = jax.nn.silu(depthwise_conv_centred(xbcc, conv_w, conv_b))
    zero = jnp.zeros((b, SSD_HEADS, SSD_HEAD_DIM, SSD_STATE), jnp.float32)
    yc_core, xsc, fin_f, fin_b = ssd_core(xbcc, dtc, a_log, dt_bias, zero, zero)
    kc = kc.reshape(b, S, ATTN_KV_HEADS, ATTN_HEAD_DIM)
    vc = vc.reshape(b, S, ATTN_KV_HEADS, ATTN_HEAD_DIM)
    z, xbc, dtr, q, k, v, g = split_projection(h @ w_in)
    xbc = jax.nn.silu(depthwise_conv_centred(xbc, conv_w, conv_b))
    y_core, xs, _, _ = ssd_core(xbc, dtr, a_log, dt_bias, fin_f, fin_b)
    y_ssd = ssd_output(y_core, xs, z, d_skip, ssd_norm_w)
    q = apply_rope(q.reshape(b, L, ATTN_Q_HEADS, ATTN_HEAD_DIM), cos, sin)
    k = apply_rope(k.reshape(b, L, ATTN_KV_HEADS, ATTN_HEAD_DIM), cos, sin)
    v = v.reshape(b, L, ATTN_KV_HEADS, ATTN_HEAD_DIM)
    y_attn = windowed_attention(q, k, v, kc, vc, attn_sink)
    y = merge_branches(y_ssd, y_attn, g, w_ssd_br, w_attn_br, w_o, b_o)
    yc = None
    if with_ctx_out:
        yc_ssd = ssd_output(yc_core, xsc, zc, d_skip, ssd_norm_w)
        yc_attn = context_attention(qc.reshape(b, S, ATTN_Q_HEADS, ATTN_HEAD_DIM), kc, vc, attn_sink)
        yc = merge_branches(yc_ssd, yc_attn, gc, w_ssd_br, w_attn_br, w_o, b_o)
    return y, yc


def expert_choice_ffn(h, w_router, w_gate_up, w_down):
    b, L, D = h.shape
    cap = CAPACITY_FACTOR * L // N_EXPERTS
    aff = jax.nn.softmax((h @ w_router).astype(jnp.float32), axis=-1)
    gate_vals, idx = lax.top_k(jnp.swapaxes(aff, 1, 2), cap)
    xs = jax.vmap(lambda hb, ib: hb[ib])(h, idx)
    gu = jnp.einsum("becd,edf->becf", xs, w_gate_up)
    g, u = jnp.split(gu, 2, axis=-1)
    ye = jnp.einsum("becf,efd->becd", jax.nn.silu(g) * u, w_down)
    ye = ye * gate_vals[..., None].astype(h.dtype)
    return jax.vmap(lambda ib, yb: jnp.zeros((L, D), h.dtype).at[ib.reshape(-1)].add(yb.reshape(-1, D)))(idx, ye)


def setup_inputs(seed: int = 0) -> dict:
    key = jax.random.key(seed)
    ks = jax.random.split(key, 26)
    f32 = jnp.float32

    def nrm(k, shape, scale):
        return jax.random.normal(k, shape, f32) * scale

    dt0 = jnp.exp(jax.random.uniform(ks[9], (DEPTH, 2, SSD_HEADS), f32,
                                     minval=np.log(1e-3), maxval=np.log(1e-1)))
    return {
        "x": nrm(ks[0], (BATCH, SEQ, D_MODEL), 1.0),
        "c": nrm(ks[1], (BATCH, D_MODEL), 1.0),
        "ctx": nrm(ks[2], (BATCH, CTX_LEN, D_MODEL), 1.0),
        "c_ctx": nrm(ks[3], (D_MODEL,), 1.0),
        "w_ada": nrm(ks[4], (DEPTH, D_MODEL, 6 * D_MODEL), 0.5 * D_MODEL ** -0.5),
        "b_ada": nrm(ks[5], (DEPTH, 6 * D_MODEL), 0.02),
        "w_in": nrm(ks[6], (DEPTH, D_MODEL, N_IN), D_MODEL ** -0.5),
        "conv_w": nrm(ks[7], (DEPTH, CONV_W, CONV_CH), CONV_W ** -0.5),
        "conv_b": nrm(ks[8], (DEPTH, CONV_CH), 0.02),
        "a_log": jnp.log(jax.random.uniform(ks[10], (DEPTH, 2, SSD_HEADS), f32, minval=1.0, maxval=16.0)),
        "dt_bias": dt0 + jnp.log(-jnp.expm1(-dt0)),
        "d_skip": 1.0 + nrm(ks[11], (DEPTH, SSD_HEADS), 0.1),
        "ssd_norm_w": 1.0 + nrm(ks[12], (DEPTH, D_INNER), 0.02),
        "attn_sink": nrm(ks[13], (DEPTH, ATTN_Q_HEADS), 0.5),
        "w_ssd_br": nrm(ks[14], (DEPTH, D_INNER, D_MODEL), DEEPNORM_BETA * D_INNER ** -0.5),
        "w_attn_br": nrm(ks[15], (DEPTH, ATTN_WIDTH, D_MODEL), DEEPNORM_BETA * ATTN_WIDTH ** -0.5),
        "w_o": nrm(ks[16], (DEPTH, D_MODEL, D_MODEL), DEEPNORM_BETA * D_MODEL ** -0.5),
        "b_o": nrm(ks[17], (DEPTH, D_MODEL), 0.02),
        "ln1_g": 1.0 + nrm(ks[18], (DEPTH, D_MODEL), 0.02),
        "ln1_b": nrm(ks[19], (DEPTH, D_MODEL), 0.02),
        "w_router": nrm(ks[20], (DEPTH, D_MODEL, N_EXPERTS), D_MODEL ** -0.5),
        "w_gate_up": nrm(ks[21], (DEPTH, N_EXPERTS, D_MODEL, 2 * EXPERT_FF), D_MODEL ** -0.5),
        "w_down": nrm(ks[22], (DEPTH, N_EXPERTS, EXPERT_FF, D_MODEL), DEEPNORM_BETA * EXPERT_FF ** -0.5),
        "ln2_g": 1.0 + nrm(ks[23], (DEPTH, D_MODEL), 0.02),
        "ln2_b": nrm(ks[24], (DEPTH, D_MODEL), 0.02),
    }


def reference(x, c, ctx, c_ctx, w_ada, b_ada, w_in, conv_w, conv_b, a_log, dt_bias, d_skip,
              ssd_norm_w, attn_sink, w_ssd_br, w_attn_br, w_o, b_o, ln1_g, ln1_b,
              w_router, w_gate_up, w_down, ln2_g, ln2_b):
    L = x.shape[1]
    rows = L // GRID_W
    cos, sin = axial_rope(rows, x.dtype)
    xc = ctx
    for l in range(DEPTH):
        last = l == DEPTH - 1
        mod = jax.nn.silu(c) @ w_ada[l] + b_ada[l]
        mod_c = jax.nn.silu(c_ctx) @ w_ada[l] + b_ada[l]
        sh1, sc1, g1, sh2, sc2, g2 = [m[:, None, :] for m in jnp.split(mod, 6, axis=-1)]
        csh1, csc1, cg1, csh2, csc2, cg2 = [m[None, None, :] for m in jnp.split(mod_c, 6, axis=-1)]
        h = modulate(layer_norm(x), sh1, sc1)
        hc = modulate(layer_norm(xc), csh1, csc1)
        y, yc = token_mixer(h, hc, cos, sin, w_in[l], conv_w[l], conv_b[l], a_log[l], dt_bias[l],
                            d_skip[l], ssd_norm_w[l], attn_sink[l], w_ssd_br[l], w_attn_br[l],
                            w_o[l], b_o[l], not last)
        x = layer_norm_affine(DEEPNORM_ALPHA * x + g1 * y, ln1_g[l], ln1_b[l])
        h = modulate(layer_norm(x), sh2, sc2)
        x = layer_norm_affine(DEEPNORM_ALPHA * x + g2 * expert_choice_ffn(h, w_router[l], w_gate_up[l], w_down[l]),
                              ln2_g[l], ln2_b[l])
        if not last:
            xc = layer_norm_affine(DEEPNORM_ALPHA * xc + cg1 * yc, ln1_g[l], ln1_b[l])
            hc = modulate(layer_norm(xc), csh2, csc2)
            xc = layer_norm_affine(DEEPNORM_ALPHA * xc + cg2 * expert_choice_ffn(hc, w_router[l], w_gate_up[l], w_down[l]),
                                   ln2_g[l], ln2_b[l])
    return x
```

```python
import functools

import numpy as np
import jax
import jax.numpy as jnp
from jax import lax
from jax.experimental import pallas as pl
from jax.experimental.pallas import tpu as pltpu

F32 = jnp.float32
BF16 = jnp.bfloat16
I32 = jnp.int32
HIGHEST = lax.Precision.HIGHEST

LANES = 128
SUBLANES = 8
VMEM_LIMIT_BYTES = 56 * 1024 * 1024

D_MODEL = 1024
GRID_W = 64
LN_EPS = 1e-5
DEPTH = 1
DEEPNORM_ALPHA = (2.0 * DEPTH) ** 0.25
D_INNER = 2048
SSD_HEAD_DIM = 64
SSD_HEADS = 32
SSD_GROUPS = 4
SSD_STATE = 128
SSD_CHUNK = 128
CONV_W = 5
CONV_CH = D_INNER + 2 * SSD_GROUPS * SSD_STATE
ATTN_HEAD_DIM = 64
ATTN_Q_HEADS = 16
ATTN_KV_HEADS = 4
ATTN_WIDTH = 1024
KV_WIDTH = 256
WINDOW = 128
ATTN_BLOCK = 128
ATTN_SCALE = ATTN_HEAD_DIM ** -0.5
ROPE_BASE = 10000.0
ROPE_AXIS_DIM = ATTN_HEAD_DIM // 2
N_EXPERTS = 16
EXPERT_FF = 1024
CAPACITY_FACTOR = 2

OFF_Z = 0
OFF_G = OFF_Z + D_INNER
OFF_XBC = OFF_G + 2 * D_MODEL
OFF_Q = OFF_XBC + CONV_CH
OFF_K = OFF_Q + ATTN_WIDTH
OFF_V = OFF_K + KV_WIDTH
N_MAIN = OFF_V + KV_WIDTH
PROJ_TN = 512
KV_DUP = 2 * KV_WIDTH


def _params(*sem):
    return pltpu.CompilerParams(dimension_semantics=sem, vmem_limit_bytes=VMEM_LIMIT_BYTES)


def _silu(x):
    return x * jax.nn.sigmoid(x)


def _dot(a, b, **kw):
    return jnp.dot(a, b, preferred_element_type=F32, **kw)


def _dot_nt(a, b, **kw):
    return lax.dot_general(a, b, (((1,), (1,)), ((), ())), preferred_element_type=F32, **kw)


def _ln(x):
    mu = jnp.mean(x, -1, keepdims=True)
    xc = x - mu
    var = jnp.mean(xc * xc, -1, keepdims=True)
    return xc * lax.rsqrt(var + LN_EPS)


def _ada_kernel(s_ref, w_ref, b_ref, o_ref):
    s = _silu(s_ref[...])
    o_ref[...] = _dot(s, w_ref[...], precision=HIGHEST) + b_ref[...]


def _ada(cc, w_ada, b_ada):
    n = w_ada.shape[1]
    tn = 1024
    return pl.pallas_call(
        _ada_kernel,
        grid=(n // tn,),
        in_specs=[pl.BlockSpec((SUBLANES, D_MODEL), lambda j: (0, 0)),
                  pl.BlockSpec((D_MODEL, tn), lambda j: (0, j)),
                  pl.BlockSpec((1, tn), lambda j: (0, j))],
        out_specs=pl.BlockSpec((SUBLANES, tn), lambda j: (0, j)),
        out_shape=jax.ShapeDtypeStruct((SUBLANES, n), F32),
        compiler_params=_params("parallel"),
        name="ada",
    )(cc, w_ada, b_ada)


def _ln_proj_kernel(x_ref, sh_ref, sc_ref, w_ref, wdt_ref, o_ref, dt_ref, h_ref, *, tm):
    @pl.when(pl.program_id(1) == 0)
    def _():
        rows = 256 if tm % 256 == 0 else tm
        for r0 in range(0, tm, rows):
            h = _ln(x_ref[r0:r0 + rows, :]) * (1.0 + sc_ref[...]) + sh_ref[...]
            h_ref[r0:r0 + rows, :] = h.astype(BF16)
        dt_ref[...] = _dot(h_ref[...], wdt_ref[...])

    o_ref[...] = _dot(h_ref[...], w_ref[...]).astype(o_ref.dtype)


def _ln_proj(x, shift, scale, w_main, w_dt, tm):
    L = x.shape[0]
    return pl.pallas_call(
        functools.partial(_ln_proj_kernel, tm=tm),
        grid=(L // tm, N_MAIN // PROJ_TN),
        in_specs=[pl.BlockSpec((tm, D_MODEL), lambda i, j: (i, 0)),
                  pl.BlockSpec((1, D_MODEL), lambda i, j: (0, 0)),
                  pl.BlockSpec((1, D_MODEL), lambda i, j: (0, 0)),
                  pl.BlockSpec((D_MODEL, PROJ_TN), lambda i, j: (0, j)),
                  pl.BlockSpec((D_MODEL, LANES), lambda i, j: (0, 0))],
        out_specs=[pl.BlockSpec((tm, PROJ_TN), lambda i, j: (i, j)),
                   pl.BlockSpec((tm, LANES), lambda i, j: (i, 0))],
        out_shape=[jax.ShapeDtypeStruct((L, N_MAIN), BF16),
                   jax.ShapeDtypeStruct((L, LANES), F32)],
        scratch_shapes=[pltpu.VMEM((tm, D_MODEL), BF16)],
        compiler_params=_params("parallel", "arbitrary"),
        name="ln_proj",
    )(x, shift, scale, w_main, w_dt)


CONV_HALO = 16
CONV_PAD = 8


def _conv_kernel(xp_ref, x_ref, xn_ref, w_ref, b_ref, o_ref, ext_ref, *, tm):
    i = pl.program_id(0)
    last = pl.num_programs(0) - 1
    prev = xp_ref[CONV_HALO - CONV_PAD:, :].astype(F32)
    nxt = xn_ref[:CONV_PAD, :].astype(F32)
    ext_ref[0:CONV_PAD, :] = jnp.where(i > 0, prev, 0.0)
    ext_ref[CONV_PAD:CONV_PAD + tm, :] = x_ref[...].astype(F32)
    ext_ref[CONV_PAD + tm:, :] = jnp.where(i < last, nxt, 0.0)
    rows = min(tm, 128)
    half = CONV_W // 2
    for r0 in range(0, tm, rows):
        acc = jnp.broadcast_to(b_ref[...], (rows, x_ref.shape[1]))
        for k in range(CONV_W):
            start = CONV_PAD - half + k + r0
            acc = acc + w_ref[k:k + 1, :] * ext_ref[start:start + rows, :]
        o_ref[r0:r0 + rows, :] = _silu(acc).astype(o_ref.dtype)


def _conv(p_main, conv_w, conv_b, tm, tc=512):
    L = p_main.shape[0]
    hb = tm // CONV_HALO
    n_halo = L // CONV_HALO
    c0 = OFF_XBC // tc
    return pl.pallas_call(
        functools.partial(_conv_kernel, tm=tm),
        grid=(L // tm, CONV_CH // tc),
        in_specs=[pl.BlockSpec((CONV_HALO, tc), lambda i, j: (jnp.maximum(i * hb - 1, 0), c0 + j)),
                  pl.BlockSpec((tm, tc), lambda i, j: (i, c0 + j)),
                  pl.BlockSpec((CONV_HALO, tc), lambda i, j: (jnp.minimum((i + 1) * hb, n_halo - 1), c0 + j)),
                  pl.BlockSpec((CONV_W, tc), lambda i, j: (0, j)),
                  pl.BlockSpec((1, tc), lambda i, j: (0, j))],
        out_specs=pl.BlockSpec((tm, tc), lambda i, j: (i, j)),
        out_shape=jax.ShapeDtypeStruct((L, CONV_CH), BF16),
        scratch_shapes=[pltpu.VMEM((tm + 2 * CONV_PAD, tc), F32)],
        compiler_params=_params("parallel", "parallel"),
        name="conv",
    )(p_main, p_main, p_main, conv_w, conv_b)


def _qkv_kernel(*refs, tm, rope, with_q):
    if with_q:
        q_ref, k_ref, v_ref, cos_ref, sin_ref, qo_ref, ko_ref, vo_ref = refs
    else:
        k_ref, v_ref, ko_ref, vo_ref = refs
    lane = lax.broadcasted_iota(I32, (tm, LANES), 1)
    first = (lane & 31) < 16
    lo = lane < 64

    def rot(x):
        if not rope:
            return x
        r = jnp.where(first, -pltpu.roll(x, LANES - 16, 1), pltpu.roll(x, 16, 1))
        return x * cos_ref[...] + r * sin_ref[...]

    def dup(x, o_ref, p):
        r = pltpu.roll(x, 64, 1)
        o_ref[:, (2 * p) * LANES:(2 * p + 1) * LANES] = jnp.where(lo, x, r).astype(o_ref.dtype)
        o_ref[:, (2 * p + 1) * LANES:(2 * p + 2) * LANES] = jnp.where(lo, r, x).astype(o_ref.dtype)

    if with_q:
        for c in range(ATTN_WIDTH // LANES):
            sl = slice(c * LANES, (c + 1) * LANES)
            qo_ref[:, sl] = (rot(q_ref[:, sl].astype(F32)) * ATTN_SCALE).astype(qo_ref.dtype)
    for p in range(KV_WIDTH // LANES):
        sl = slice(p * LANES, (p + 1) * LANES)
        dup(rot(k_ref[:, sl].astype(F32)), ko_ref, p)
        dup(v_ref[:, sl].astype(F32), vo_ref, p)


def _qkv(p_main, cos, sin, tm, rope, with_q):
    L = p_main.shape[0]
    kv_specs = [pl.BlockSpec((tm, KV_WIDTH), lambda i: (i, OFF_K // KV_WIDTH)),
                pl.BlockSpec((tm, KV_WIDTH), lambda i: (i, OFF_V // KV_WIDTH))]
    kv_out = [pl.BlockSpec((tm, KV_DUP), lambda i: (i, 0)), pl.BlockSpec((tm, KV_DUP), lambda i: (i, 0))]
    kv_shape = [jax.ShapeDtypeStruct((L, KV_DUP), BF16), jax.ShapeDtypeStruct((L, KV_DUP), BF16)]
    if with_q:
        in_specs = ([pl.BlockSpec((tm, ATTN_WIDTH), lambda i: (i, OFF_Q // ATTN_WIDTH))] + kv_specs
                    + [pl.BlockSpec((tm, LANES), lambda i: (i, 0)), pl.BlockSpec((tm, LANES), lambda i: (i, 0))])
        out_specs = [pl.BlockSpec((tm, ATTN_WIDTH), lambda i: (i, 0))] + kv_out
        out_shape = [jax.ShapeDtypeStruct((L, ATTN_WIDTH), BF16)] + kv_shape
        args = (p_main, p_main, p_main, cos, sin)
    else:
        in_specs, out_specs, out_shape, args = kv_specs, kv_out, kv_shape, (p_main, p_main)
    return pl.pallas_call(
        functools.partial(_qkv_kernel, tm=tm, rope=rope, with_q=with_q),
        grid=(L // tm,),
        in_specs=in_specs, out_specs=out_specs, out_shape=out_shape,
        compiler_params=_params("parallel"),
        name="qkv_rope" if with_q else "kv_ctx",
    )(*args)


PAIRS_PER_GROUP = (SSD_HEADS // SSD_GROUPS) // 2
GROUP_W = (SSD_HEADS // SSD_GROUPS) * SSD_HEAD_DIM


def _ssd_kernel(xf_ref, xb_ref, dtf_ref, dtb_ref, s0_ref, bias_ref, alog_ref, yf_ref, yb_ref, s_ref):
    Q = SSD_CHUNK

    @pl.when(pl.program_id(0) == 0)
    def _():
        s_ref[...] = s0_ref[...]

    row = lax.broadcasted_iota(I32, (Q, Q), 0)
    col = lax.broadcasted_iota(I32, (Q, Q), 1)
    lo = col < 64
    lo_row = lax.broadcasted_iota(I32, (1, Q), 1) < 64
    neg_a = -jnp.exp(alog_ref[...])

    for d, (x_ref, dt_ref, y_ref) in enumerate(((xf_ref, dtf_ref, yf_ref), (xb_ref, dtb_ref, yb_ref))):
        tri = (row >= col) if d == 0 else (row <= col)
        off = d * SSD_HEADS
        z = dt_ref[...] + bias_ref[...]
        dt = jnp.maximum(z, 0.0) + jnp.log1p(jnp.exp(-jnp.abs(z)))
        da = dt * neg_a
        a = _dot(jnp.where(tri, 1.0, 0.0), da, precision=HIGHEST)
        a_t = a.T
        dt_t = dt.T
        ea = jnp.exp(a)
        edge = a_t[:, Q - 1:Q] if d == 0 else a_t[:, 0:1]
        w_t = dt_t * jnp.exp(edge - a_t)
        dec = jnp.exp(edge)

        for g in range(SSD_GROUPS):
            bg = x_ref[:, D_INNER + g * SSD_STATE:D_INNER + (g + 1) * SSD_STATE]
            cg = x_ref[:, D_INNER + (SSD_GROUPS + g) * SSD_STATE:D_INNER + (SSD_GROUPS + g + 1) * SSD_STATE]
            cb = _dot_nt(cg, bg)
            b_t = bg.astype(F32).T
            s_prev = s_ref[d, g]
            y_off = _dot(cg, s_prev.astype(BF16))
            for p in range(PAIRS_PER_GROUP):
                h0 = g * (SSD_HEADS // SSD_GROUPS) + 2 * p
                c0 = h0 * SSD_HEAD_DIM
                xpair = x_ref[:, c0:c0 + LANES]
                zero = jnp.zeros_like(xpair)
                y_diag = None
                st = None
                for t, xm in enumerate((jnp.where(lo, xpair, zero), jnp.where(lo, zero, xpair))):
                    h = off + h0 + t
                    seg = a[:, h:h + 1] - a_t[h:h + 1, :]
                    lm = cb * jnp.exp(jnp.where(tri, seg, -jnp.inf)) * dt_t[h:h + 1, :]
                    yd = _dot(lm.astype(BF16), xm)
                    bw = (b_t * w_t[h:h + 1, :]).astype(BF16)
                    sd = _dot(bw, xm)
                    y_diag = yd if y_diag is None else y_diag + yd
                    st = sd if st is None else st + sd
                h = off + h0
                ea_pair = jnp.where(lo, ea[:, h:h + 1], ea[:, h + 1:h + 2])
                ps = slice(p * LANES, (p + 1) * LANES)
                y_ref[:, c0:c0 + LANES] = (y_diag + y_off[:, ps] * ea_pair).astype(y_ref.dtype)
                dec_pair = jnp.where(lo_row, dec[h:h + 1, :], dec[h + 1:h + 2, :])
                s_ref[d, g, :, ps] = s_prev[:, ps] * dec_pair + st


def _ssd(xc, dt, s0, bias, alog):
    L = xc.shape[0]
    nc = L // SSD_CHUNK
    sshape = (2, SSD_GROUPS, SSD_STATE, GROUP_W)
    return pl.pallas_call(
        _ssd_kernel,
        grid=(nc,),
        in_specs=[pl.BlockSpec((SSD_CHUNK, CONV_CH), lambda c: (c, 0)),
                  pl.BlockSpec((SSD_CHUNK, CONV_CH), lambda c: (nc - 1 - c, 0)),
                  pl.BlockSpec((SSD_CHUNK, LANES), lambda c: (c, 0)),
                  pl.BlockSpec((SSD_CHUNK, LANES), lambda c: (nc - 1 - c, 0)),
                  pl.BlockSpec(sshape, lambda c: (0, 0, 0, 0)),
                  pl.BlockSpec((1, LANES), lambda c: (0, 0)),
                  pl.BlockSpec((1, LANES), lambda c: (0, 0))],
        out_specs=[pl.BlockSpec((SSD_CHUNK, D_INNER), lambda c: (c, 0)),
                   pl.BlockSpec((SSD_CHUNK, D_INNER), lambda c: (nc - 1 - c, 0)),
                   pl.BlockSpec(sshape, lambda c: (0, 0, 0, 0))],
        out_shape=[jax.ShapeDtypeStruct((L, D_INNER), BF16),
                   jax.ShapeDtypeStruct((L, D_INNER), BF16),
                   jax.ShapeDtypeStruct(sshape, F32)],
        compiler_params=_params("arbitrary"),
        name="ssd",
    )(xc, xc, dt, dt, s0, bias, alog)


def _attn_kernel(sink_ref, q_ref, kp_ref, kc_ref, kn_ref, vp_ref, vc_ref, vn_ref, kx_ref, vx_ref, o_ref, *, s_ctx):
    i = pl.program_id(0)
    last = pl.num_programs(0) - 1
    B = ATTN_BLOCK
    span = 3 * B
    width = span + s_ctx
    t = lax.broadcasted_iota(I32, (B, width), 0)
    j = lax.broadcasted_iota(I32, (B, width), 1)
    rel = j - B - t
    j_min = jnp.where(i > 0, 0, B)
    j_max = jnp.where(i < last, span, 2 * B)
    valid = (j >= span) | ((jnp.abs(rel) <= WINDOW) & (j >= j_min) & (j < j_max))
    lo = lax.broadcasted_iota(I32, (B, LANES), 1) < 64
    lo_k = lax.broadcasted_iota(I32, (width, LANES), 1) < 64

    for kh in range(ATTN_KV_HEADS):
        sl = slice(kh * LANES, (kh + 1) * LANES)
        k_all = jnp.concatenate([kp_ref[:, sl], kc_ref[:, sl], kn_ref[:, sl], kx_ref[:, sl]], axis=0)
        v_all = jnp.concatenate([vp_ref[:, sl], vc_ref[:, sl], vn_ref[:, sl], vx_ref[:, sl]], axis=0)
        zero_v = jnp.zeros_like(v_all)
        v_halves = (jnp.where(lo_k, v_all, zero_v), jnp.where(lo_k, zero_v, v_all))
        for u in range(2):
            c = 2 * kh + u
            qpair = q_ref[:, c * LANES:(c + 1) * LANES]
            zero_q = jnp.zeros_like(qpair)
            acc = None
            for hh, qm in enumerate((jnp.where(lo, qpair, zero_q), jnp.where(lo, zero_q, qpair))):
                sink = sink_ref[2 * c + hh]
                s = jnp.where(valid, _dot_nt(qm, k_all), -jnp.inf)
                m = jnp.maximum(jnp.max(s, axis=1, keepdims=True), sink)
                e = jnp.exp(s - m)
                denom = jnp.sum(e, axis=1, keepdims=True) + jnp.exp(sink - m)
                o = _dot(e.astype(BF16), v_halves[hh]) * (1.0 / denom)
                acc = o if acc is None else acc + o
            o_ref[:, c * LANES:(c + 1) * LANES] = acc.astype(o_ref.dtype)


def _attn(sink, q, kd, vd, kx, vx):
    L = q.shape[0]
    s_ctx = kx.shape[0]
    nb = L // ATTN_BLOCK
    prev = lambda i: (jnp.maximum(i - 1, 0), 0)
    cur = lambda i: (i, 0)
    nxt = lambda i: (jnp.minimum(i + 1, nb - 1), 0)
    kvb = lambda f: pl.BlockSpec((ATTN_BLOCK, KV_DUP), f)
    return pl.pallas_call(
        functools.partial(_attn_kernel, s_ctx=s_ctx),
        grid=(nb,),
        in_specs=[pl.BlockSpec(memory_space=pltpu.SMEM),
                  pl.BlockSpec((ATTN_BLOCK, ATTN_WIDTH), cur),
                  kvb(prev), kvb(cur), kvb(nxt), kvb(prev), kvb(cur), kvb(nxt),
                  pl.BlockSpec((s_ctx, KV_DUP), lambda i: (0, 0)),
                  pl.BlockSpec((s_ctx, KV_DUP), lambda i: (0, 0))],
        out_specs=pl.BlockSpec((ATTN_BLOCK, ATTN_WIDTH), cur),
        out_shape=jax.ShapeDtypeStruct((L, ATTN_WIDTH), BF16),
        compiler_params=_params("parallel"),
        name="attn",
    )(sink, q, kd, kd, kd, vd, vd, vd, kx, vx)


def _out_kernel(yf_ref, yb_ref, xs_ref, z_ref, g_ref, ya_ref, x_ref, dsk_ref, nw_ref, wssd_ref, wattn_ref, wo_ref,
                bo_ref, g1_ref, ln1g_ref, ln1b_ref, sh2_ref, sc2_ref, wr_ref, x1_ref, h2_ref, aff_ref):
    y = yf_ref[...].astype(F32) + yb_ref[...].astype(F32) + dsk_ref[...] * xs_ref[...].astype(F32)
    y = y * _silu(z_ref[...].astype(F32))
    y = y * lax.rsqrt(jnp.mean(y * y, -1, keepdims=True) + LN_EPS) * nw_ref[...]
    br_ssd = _dot(y.astype(BF16), wssd_ref[...])
    br_attn = _dot(ya_ref[...], wattn_ref[...])
    g_ssd = jax.nn.sigmoid(g_ref[:, :D_MODEL].astype(F32))
    g_attn = jax.nn.sigmoid(g_ref[:, D_MODEL:].astype(F32))
    m = g_ssd * br_ssd + g_attn * br_attn
    o = _dot(m.astype(BF16), wo_ref[...]) + bo_ref[...]
    x1 = _ln(DEEPNORM_ALPHA * x_ref[...] + g1_ref[...] * o) * ln1g_ref[...] + ln1b_ref[...]
    x1_ref[...] = x1
    h2 = _ln(x1) * (1.0 + sc2_ref[...]) + sh2_ref[...]
    h2_ref[...] = h2
    logits = _dot_nt(h2, wr_ref[...], precision=HIGHEST)
    lane = lax.broadcasted_iota(I32, logits.shape, 1)
    logits = jnp.where(lane < N_EXPERTS, logits, -jnp.inf)
    e = jnp.exp(logits - jnp.max(logits, axis=1, keepdims=True))
    aff = e / jnp.sum(e, axis=1, keepdims=True)
    aff_ref[...] = aff.T[:N_EXPERTS, :]


def _out(yf, yb, xc, p_main, ya, x, dsk, nw, wssd, wattn, wo, bo, g1, ln1g, ln1b, sh2, sc2, wr, tm):
    L = x.shape[0]
    row = lambda w: pl.BlockSpec((1, w), lambda i: (0, 0))
    full = lambda a: pl.BlockSpec(a.shape, lambda i: (0, 0))
    return pl.pallas_call(
        _out_kernel,
        grid=(L // tm,),
        in_specs=[pl.BlockSpec((tm, D_INNER), lambda i: (i, 0)),
                  pl.BlockSpec((tm, D_INNER), lambda i: (i, 0)),
                  pl.BlockSpec((tm, D_INNER), lambda i: (i, 0)),
                  pl.BlockSpec((tm, D_INNER), lambda i: (i, OFF_Z // D_INNER)),
                  pl.BlockSpec((tm, 2 * D_MODEL), lambda i: (i, OFF_G // (2 * D_MODEL))),
                  pl.BlockSpec((tm, ATTN_WIDTH), lambda i: (i, 0)),
                  pl.BlockSpec((tm, D_MODEL), lambda i: (i, 0)),
                  row(D_INNER), row(D_INNER), full(wssd), full(wattn), full(wo),
                  row(D_MODEL), row(D_MODEL), row(D_MODEL), row(D_MODEL), row(D_MODEL), row(D_MODEL), full(wr)],
        out_specs=[pl.BlockSpec((tm, D_MODEL), lambda i: (i, 0)),
                   pl.BlockSpec((tm, D_MODEL), lambda i: (i, 0)),
                   pl.BlockSpec((N_EXPERTS, tm), lambda i: (0, i))],
        out_shape=[jax.ShapeDtypeStruct((L, D_MODEL), F32),
                   jax.ShapeDtypeStruct((L, D_MODEL), F32),
                   jax.ShapeDtypeStruct((N_EXPERTS, L), F32)],
        compiler_params=_params("parallel"),
        name="out_proj",
    )(yf, yb, xc, p_main, p_main, ya, x, dsk, nw, wssd, wattn, wo, bo, g1, ln1g, ln1b, sh2, sc2, wr)


def _count(mask):
    c = jnp.sum(jnp.where(mask, 1.0, 0.0), axis=1, keepdims=True)
    return jnp.sum(c, axis=0, keepdims=True)


def _route_kernel(aff_ref, idx_ref, pos_ref, start_ref, cnt_ref, sel_ref, *, cap, nrows):
    E = N_EXPERTS
    R = nrows
    tok = (lax.broadcasted_iota(I32, (R, LANES), 0) * LANES + lax.broadcasted_iota(I32, (R, LANES), 1))
    capf = jnp.float32(cap)

    def bits(e):
        return pltpu.bitcast(aff_ref[e], I32)

    def thr_body(it, ts):
        bit = lax.shift_left(jnp.int32(1), 30 - it)
        out = []
        for e in range(E):
            cand = ts[e] | bit
            out.append(jnp.where(_count(bits(e) >= cand) >= capf, cand, ts[e]))
        return tuple(out)

    ts = lax.fori_loop(0, 31, thr_body, tuple(jnp.zeros((1, 1), I32) for _ in range(E)))
    need = [capf - _count(bits(e) > ts[e]) for e in range(E)]

    nbits = max((R * LANES - 1).bit_length(), 1)

    def tie_body(it, js):
        bit = lax.shift_left(jnp.int32(1), nbits - 1 - it)
        out = []
        for e in range(E):
            cand = js[e] | bit
            f = _count((bits(e) == ts[e]) & (tok < cand))
            out.append(jnp.where(f < need[e], cand, js[e]))
        return tuple(out)

    js = lax.fori_loop(0, nbits, tie_body, tuple(jnp.zeros((1, 1), I32) for _ in range(E)))
    for e in range(E):
        b = bits(e)
        sel = (b > ts[e]) | ((b == ts[e]) & (tok <= js[e]))
        sel_ref[e] = jnp.where(sel, 1.0, 0.0)

    li = lax.broadcasted_iota(I32, (LANES, LANES), 0)
    lj = lax.broadcasted_iota(I32, (LANES, LANES), 1)
    upper_incl = jnp.where(li <= lj, 1.0, 0.0).astype(BF16)
    lower_incl = jnp.where(li >= lj, 1.0, 0.0).astype(BF16)
    ri = lax.broadcasted_iota(I32, (R, R), 0)
    rj = lax.broadcasted_iota(I32, (R, R), 1)
    row_lower = jnp.where(ri >= rj, 1.0, 0.0).astype(BF16)
    row_id = lax.broadcasted_iota(I32, (R, LANES), 0)
    lane_id = lax.broadcasted_iota(I32, (R, LANES), 1)
    lane_row = lax.broadcasted_iota(I32, (1, LANES), 1)

    def expert_body(e, carry):
        start_acc, cnt_acc = carry
        sel = sel_ref[e]
        selb = sel.astype(BF16)
        w = _dot(selb, upper_incl)
        w_t = _dot_nt(lower_incl, selb)
        tot = jnp.sum(sel, axis=1, keepdims=True)
        tot_b = jnp.broadcast_to(tot, (R, LANES))
        cum_incl = _dot(row_lower, tot_b.astype(BF16))
        base = cum_incl - tot_b
        pos_ref[e] = jnp.where(sel > 0.0, base + w - 1.0, -1.0).astype(I32)
        start_acc = jnp.where(lane_id == e, base, start_acc)
        cnt_acc = jnp.where(lane_id == e, tot_b, cnt_acc)

        def chunk_body(sc, _):
            s_row = (lane_row + sc * LANES).astype(F32)
            r_of_s = jnp.sum(jnp.where(cum_incl <= s_row, 1.0, 0.0), axis=0, keepdims=True)
            onehot = row_id.astype(F32) == r_of_s
            base_s = jnp.sum(jnp.where(onehot, base, 0.0), axis=0, keepdims=True)
            w_row = _dot(w_t.astype(BF16), jnp.where(onehot, 1.0, 0.0).astype(BF16))
            local = s_row - base_s
            l_of_s = jnp.sum(jnp.where(w_row <= local, 1.0, 0.0), axis=0, keepdims=True)
            idx_ref[e, pl.ds(sc, 1), :] = (r_of_s * LANES + l_of_s).astype(I32)
            return 0

        lax.fori_loop(0, cap // LANES, chunk_body, 0)
        return start_acc, cnt_acc

    zero = jnp.zeros((R, LANES), F32)
    start_acc, cnt_acc = lax.fori_loop(0, E, expert_body, (zero, zero))
    start_ref[...] = start_acc.astype(I32)
    cnt_ref[...] = cnt_acc.astype(I32)


def _route(aff3, cap):
    E, R, _ = aff3.shape
    return pl.pallas_call(
        functools.partial(_route_kernel, cap=cap, nrows=R),
        out_shape=[jax.ShapeDtypeStruct((E, cap // LANES, LANES), I32),
                   jax.ShapeDtypeStruct((E, R, LANES), I32),
                   jax.ShapeDtypeStruct((R, LANES), I32),
                   jax.ShapeDtypeStruct((R, LANES), I32)],
        scratch_shapes=[pltpu.VMEM((E, R, LANES), F32)],
        compiler_params=pltpu.CompilerParams(vmem_limit_bytes=VMEM_LIMIT_BYTES),
        name="route",
    )(aff3)


def _expert_kernel(idx_ref, h_hbm, wgu_ref, wd_ref, o_ref, xs_ref, wgu_bf, wd_bf, sem, *, ts):
    e = pl.program_id(0)
    j = pl.program_id(1)

    def issue(s, _):
        tok = idx_ref[e, j * ts + s]
        pltpu.make_async_copy(h_hbm.at[pl.ds(tok, 1), :], xs_ref.at[pl.ds(s, 1), :], sem).start()
        return 0

    lax.fori_loop(0, ts, issue, 0)

    @pl.when(j == 0)
    def _():
        rows = 128
        for r0 in range(0, D_MODEL, rows):
            wgu_bf[r0:r0 + rows, :] = wgu_ref[0, r0:r0 + rows, :].astype(BF16)
        for r0 in range(0, EXPERT_FF, rows):
            wd_bf[r0:r0 + rows, :] = wd_ref[0, r0:r0 + rows, :].astype(BF16)

    pltpu.make_async_copy(h_hbm.at[pl.ds(0, ts), :], xs_ref, sem).wait()
    xs = xs_ref[...].astype(BF16)
    gu = _dot(xs, wgu_bf[...])
    act = _silu(gu[:, :EXPERT_FF]) * gu[:, EXPERT_FF:]
    o_ref[0] = _dot(act.astype(BF16), wd_bf[...])


def _expert(idx, h2, wgu, wd, ts):
    E, cap = idx.shape
    return pl.pallas_call(
        functools.partial(_expert_kernel, ts=ts),
        grid_spec=pltpu.PrefetchScalarGridSpec(
            num_scalar_prefetch=1,
            grid=(E, cap // ts),
            in_specs=[pl.BlockSpec(memory_space=pl.ANY),
                      pl.BlockSpec((1, D_MODEL, 2 * EXPERT_FF), lambda e, j, idx: (e, 0, 0)),
                      pl.BlockSpec((1, EXPERT_FF, D_MODEL), lambda e, j, idx: (e, 0, 0))],
            out_specs=pl.BlockSpec((1, ts, D_MODEL), lambda e, j, idx: (e, j, 0)),
            scratch_shapes=[pltpu.VMEM((ts, D_MODEL), F32),
                            pltpu.VMEM((D_MODEL, 2 * EXPERT_FF), BF16),
                            pltpu.VMEM((EXPERT_FF, D_MODEL), BF16),
                            pltpu.SemaphoreType.DMA(())]),
        out_shape=jax.ShapeDtypeStruct((E, cap, D_MODEL), F32),
        compiler_params=_params("arbitrary", "arbitrary"),
        name="expert_ffn",
    )(idx, h2, wgu, wd)


def _combine_kernel(idx_ref, start_ref, cnt_ref, ye_hbm, x1_ref, aff_ref, pos_ref, g2_ref, lng_ref, lnb_ref,
                    o_ref, buf_ref, sem, *, cap):
    r = pl.program_id(0)
    T = LANES
    total = jnp.int32(0)
    for e in range(N_EXPERTS):
        st = start_ref[e, r]
        n = cnt_ref[e, r]

        def issue(s, _, e=e, st=st):
            tok = idx_ref[e, st + s]
            pltpu.make_async_copy(ye_hbm.at[pl.ds(e * cap + st + s, 1), :],
                                  buf_ref.at[e, pl.ds(tok - r * T, 1), :], sem).start()
            return 0

        lax.fori_loop(0, n, issue, 0)
        total = total + n

    def wait_one(s, _):
        pltpu.make_async_copy(ye_hbm.at[pl.ds(0, 1), :], buf_ref.at[0, pl.ds(0, 1), :], sem).wait()
        return 0

    lax.fori_loop(0, total, wait_one, 0)

    gate = jnp.where(pos_ref[:, 0, 0, :] >= 0, aff_ref[...], 0.0)
    gate_tok = jnp.concatenate([gate, jnp.zeros((T - N_EXPERTS, T), F32)], axis=0).T
    moe = jnp.zeros((T, D_MODEL), F32)
    for e in range(N_EXPERTS):
        gcol = gate_tok[:, e:e + 1]
        moe = moe + jnp.where(gcol > 0.0, buf_ref[e], 0.0) * gcol
    y = _ln(DEEPNORM_ALPHA * x1_ref[...] + g2_ref[...] * moe)
    o_ref[...] = y * lng_ref[...] + lnb_ref[...]


def _combine(idx, start, cnt, ye, x1, aff_t, pos4, g2, lng, lnb):
    L = x1.shape[0]
    E, cap = idx.shape
    T = LANES
    row = pl.BlockSpec((1, D_MODEL), lambda r, *_: (0, 0))
    return pl.pallas_call(
        functools.partial(_combine_kernel, cap=cap),
        grid_spec=pltpu.PrefetchScalarGridSpec(
            num_scalar_prefetch=3,
            grid=(L // T,),
            in_specs=[pl.BlockSpec(memory_space=pl.ANY),
                      pl.BlockSpec((T, D_MODEL), lambda r, *_: (r, 0)),
                      pl.BlockSpec((E, T), lambda r, *_: (0, r)),
                      pl.BlockSpec((E, 1, 1, T), lambda r, *_: (0, r, 0, 0)),
                      row, row, row],
            out_specs=pl.BlockSpec((T, D_MODEL), lambda r, *_: (r, 0)),
            scratch_shapes=[pltpu.VMEM((E, T, D_MODEL), F32),
                            pltpu.SemaphoreType.DMA(())]),
        out_shape=jax.ShapeDtypeStruct((L, D_MODEL), F32),
        compiler_params=_params("arbitrary"),
        name="combine",
    )(idx, start, cnt, ye, x1, aff_t, pos4, g2, lng, lnb)


def _rope_tables(L):
    t = jnp.arange(L, dtype=jnp.int32)
    row = (t // GRID_W).astype(F32)
    col = (t % GRID_W).astype(F32)
    inv_freq = ROPE_BASE ** (-jnp.arange(0, ROPE_AXIS_DIM, 2, dtype=F32) / ROPE_AXIS_DIM)
    ang_r = row[:, None] * inv_freq[None, :]
    ang_c = col[:, None] * inv_freq[None, :]
    ang = jnp.concatenate([ang_r, ang_r, ang_c, ang_c], -1)
    ang = jnp.concatenate([ang, ang], -1)
    return jnp.cos(ang), jnp.sin(ang)


def _row_tile(L, pref):
    return pref if L % pref == 0 else L


def kernel(x, c, ctx, c_ctx, w_ada, b_ada, w_in, conv_w, conv_b, a_log, dt_bias, d_skip, ssd_norm_w, attn_sink,
           w_ssd_br, w_attn_br, w_o, b_o, ln1_g, ln1_b, w_router, w_gate_up, w_down, ln2_g, ln2_b):
    assert x.shape[0] == 1 and w_in.shape[0] == 1, "single batch element, depth 1"
    L = x.shape[1]
    S = ctx.shape[1]
    assert L % (LANES * SUBLANES) == 0 and S % SSD_CHUNK == 0
    cap = CAPACITY_FACTOR * L // N_EXPERTS
    x2, ctx2 = x[0], ctx[0]

    cc = jnp.zeros((SUBLANES, D_MODEL), F32).at[0].set(c[0]).at[1].set(c_ctx)
    mod = _ada(cc, w_ada[0], b_ada[0][None, :])
    sh1, sc1, g1, sh2, sc2, g2 = [mod[0:1, k * D_MODEL:(k + 1) * D_MODEL] for k in range(6)]
    csh1, csc1 = mod[1:2, 0:D_MODEL], mod[1:2, D_MODEL:2 * D_MODEL]

    w = w_in[0]
    o = 0
    parts = {}
    for name, width in (("z", D_INNER), ("xbc", CONV_CH), ("dt", 2 * SSD_HEADS), ("q", ATTN_WIDTH),
                        ("k", KV_WIDTH), ("v", KV_WIDTH), ("g", 2 * D_MODEL)):
        parts[name] = w[:, o:o + width]
        o += width
    w_main = jnp.concatenate([parts[n] for n in ("z", "g", "xbc", "q", "k", "v")], axis=1).astype(BF16)
    w_dt = jnp.pad(parts["dt"], ((0, 0), (0, LANES - 2 * SSD_HEADS))).astype(BF16)

    pad_heads = lambda v: jnp.pad(v.reshape(1, 2 * SSD_HEADS), ((0, 0), (0, LANES - 2 * SSD_HEADS)))
    bias_row, alog_row = pad_heads(dt_bias[0]), pad_heads(a_log[0])
    conv_b2 = conv_b[0][None, :]

    tm_c = _row_tile(S, 256)
    pc, dtc = _ln_proj(ctx2, csh1, csc1, w_main, w_dt, tm_c)
    xcc = _conv(pc, conv_w[0], conv_b2, tm_c)
    kx, vx = _qkv(pc, None, None, tm_c, rope=False, with_q=False)
    s_zero = jnp.zeros((2, SSD_GROUPS, SSD_STATE, GROUP_W), F32)
    _, _, s_ctx = _ssd(xcc, dtc, s_zero, bias_row, alog_row)

    p_main, dt = _ln_proj(x2, sh1, sc1, w_main, w_dt, _row_tile(L, 1024))
    xc = _conv(p_main, conv_w[0], conv_b2, _row_tile(L, 512))
    cos, sin = _rope_tables(L)
    q, kd, vd = _qkv(p_main, cos, sin, _row_tile(L, 512), rope=True, with_q=True)
    yf, yb, _ = _ssd(xc, dt, s_ctx, bias_row, alog_row)
    ya = _attn(attn_sink[0], q, kd, vd, kx, vx)

    dsk = jnp.repeat(d_skip[0], SSD_HEAD_DIM)[None, :]
    wr = jnp.pad(w_router[0].T, ((0, LANES - N_EXPERTS), (0, 0)))
    x1, h2, aff_t = _out(yf, yb, xc, p_main, ya, x2, dsk, ssd_norm_w[0][None, :],
                         w_ssd_br[0].astype(BF16), w_attn_br[0].astype(BF16), w_o[0].astype(BF16),
                         b_o[0][None, :], g1, ln1_g[0][None, :], ln1_b[0][None, :], sh2, sc2, wr,
                         _row_tile(L, 256))

    R = L // LANES
    idx3, pos, start_t, cnt_t = _route(aff_t.reshape(N_EXPERTS, R, LANES), cap)
    idx = idx3.reshape(N_EXPERTS, cap)
    start = start_t[:, :N_EXPERTS].T
    cnt = cnt_t[:, :N_EXPERTS].T
    ye = _expert(idx, h2, w_gate_up[0], w_down[0], min(cap, 512))
    out = _combine(idx, start, cnt, ye.reshape(N_EXPERTS * cap, D_MODEL), x1, aff_t,
                   pos.reshape(N_EXPERTS, R, 1, LANES), g2, ln2_g[0][None, :], ln2_b[0][None, :])
    return out[None]
```

```python
import functools

import numpy as np
import jax
import jax.numpy as jnp
from jax import lax
from jax.experimental import pallas as pl
from jax.experimental.pallas import tpu as pltpu

F32 = jnp.float32
BF16 = jnp.bfloat16
I32 = jnp.int32
HIGHEST = lax.Precision.HIGHEST

LANES = 128
SUBLANES = 8
VMEM_LIMIT_BYTES = 56 * 1024 * 1024

D_MODEL = 1024
GRID_W = 64
LN_EPS = 1e-5
DEPTH = 1
DEEPNORM_ALPHA = (2.0 * DEPTH) ** 0.25
D_INNER = 2048
SSD_HEAD_DIM = 64
SSD_HEADS = 32
SSD_GROUPS = 4
SSD_STATE = 128
SSD_CHUNK = 128
CONV_W = 5
CONV_CH = D_INNER + 2 * SSD_GROUPS * SSD_STATE
ATTN_HEAD_DIM = 64
ATTN_Q_HEADS = 16
ATTN_KV_HEADS = 4
ATTN_WIDTH = 1024
KV_WIDTH = 256
WINDOW = 128
ATTN_BLOCK = 128
ATTN_SCALE = ATTN_HEAD_DIM ** -0.5
ROPE_BASE = 10000.0
ROPE_AXIS_DIM = ATTN_HEAD_DIM // 2
N_EXPERTS = 16
EXPERT_FF = 1024
CAPACITY_FACTOR = 2

OFF_Z = 0
OFF_G = OFF_Z + D_INNER
OFF_XBC = OFF_G + 2 * D_MODEL
OFF_Q = OFF_XBC + CONV_CH
OFF_K = OFF_Q + ATTN_WIDTH
OFF_V = OFF_K + KV_WIDTH
N_MAIN = OFF_V + KV_WIDTH
PROJ_TN = N_MAIN // 4
KV_DUP = 2 * KV_WIDTH


def _params(*sem):
    return pltpu.CompilerParams(dimension_semantics=sem, vmem_limit_bytes=VMEM_LIMIT_BYTES)


def _silu(x):
    return x * jax.nn.sigmoid(x)


def _dot(a, b, **kw):
    return jnp.dot(a, b, preferred_element_type=F32, **kw)


def _dot_nt(a, b, **kw):
    return lax.dot_general(a, b, (((1,), (1,)), ((), ())), preferred_element_type=F32, **kw)


def _ln(x):
    mu = jnp.mean(x, -1, keepdims=True)
    xc = x - mu
    var = jnp.mean(xc * xc, -1, keepdims=True)
    return xc * lax.rsqrt(var + LN_EPS)


def _ada_kernel(s_ref, w_ref, b_ref, o_ref):
    s = _silu(s_ref[...])
    o_ref[...] = _dot(s, w_ref[...], precision=HIGHEST) + b_ref[...]


def _ada(cc, w_ada, b_ada):
    n = w_ada.shape[1]
    tn = 1024
    return pl.pallas_call(
        _ada_kernel,
        grid=(n // tn,),
        in_specs=[pl.BlockSpec((SUBLANES, D_MODEL), lambda j: (0, 0)),
                  pl.BlockSpec((D_MODEL, tn), lambda j: (0, j)),
                  pl.BlockSpec((1, tn), lambda j: (0, j))],
        out_specs=pl.BlockSpec((SUBLANES, tn), lambda j: (0, j)),
        out_shape=jax.ShapeDtypeStruct((SUBLANES, n), F32),
        compiler_params=_params("parallel"),
        name="ada",
    )(cc, w_ada, b_ada)


def _ln_proj_kernel(x_ref, sh_ref, sc_ref, w_ref, wdt_ref, o_ref, dt_ref, h_ref, *, tm):
    @pl.when(pl.program_id(1) == 0)
    def _():
        rows = 256 if tm % 256 == 0 else tm
        for r0 in range(0, tm, rows):
            h = _ln(x_ref[r0:r0 + rows, :]) * (1.0 + sc_ref[...]) + sh_ref[...]
            h_ref[r0:r0 + rows, :] = h.astype(BF16)
        dt_ref[...] = _dot(h_ref[...], wdt_ref[...])

    o_ref[...] = _dot(h_ref[...], w_ref[...]).astype(o_ref.dtype)


def _ln_proj(x, shift, scale, w_main, w_dt, tm):
    L = x.shape[0]
    return pl.pallas_call(
        functools.partial(_ln_proj_kernel, tm=tm),
        grid=(L // tm, N_MAIN // PROJ_TN),
        in_specs=[pl.BlockSpec((tm, D_MODEL), lambda i, j: (i, 0)),
                  pl.BlockSpec((1, D_MODEL), lambda i, j: (0, 0)),
                  pl.BlockSpec((1, D_MODEL), lambda i, j: (0, 0)),
                  pl.BlockSpec((D_MODEL, PROJ_TN), lambda i, j: (0, j)),
                  pl.BlockSpec((D_MODEL, LANES), lambda i, j: (0, 0))],
        out_specs=[pl.BlockSpec((tm, PROJ_TN), lambda i, j: (i, j)),
                   pl.BlockSpec((tm, LANES), lambda i, j: (i, 0))],
        out_shape=[jax.ShapeDtypeStruct((L, N_MAIN), BF16),
                   jax.ShapeDtypeStruct((L, LANES), F32)],
        scratch_shapes=[pltpu.VMEM((tm, D_MODEL), BF16)],
        compiler_params=_params("parallel", "arbitrary"),
        name="ln_proj",
    )(x, shift, scale, w_main, w_dt)


CONV_HALO = 64
CONV_ROWS = 128


def _conv_kernel(xp_ref, x_ref, xn_ref, w_ref, b_ref, o_ref, ext_ref, *, tm):
    i = pl.program_id(0)
    last = pl.num_programs(0) - 1
    zero_halo = jnp.zeros_like(xp_ref[...])
    ext_ref[0:CONV_HALO, :] = jnp.where(i > 0, xp_ref[...], zero_halo)
    ext_ref[CONV_HALO:CONV_HALO + tm, :] = x_ref[...]
    ext_ref[CONV_HALO + tm:, :] = jnp.where(i < last, xn_ref[...], zero_halo)

    half = CONV_W // 2
    side_taps = [k for k in range(CONV_W) if k != half]
    win = CONV_ROWS + 2 * CONV_HALO
    rr = lax.broadcasted_iota(I32, (len(side_taps) * CONV_ROWS, win), 0)
    jj = lax.broadcasted_iota(I32, (len(side_taps) * CONV_ROWS, win), 1)
    blk = lax.shift_right_logical(rr, CONV_ROWS.bit_length() - 1)
    tap = jnp.where(blk < half, blk, blk + 1)
    selector = jnp.where(jj == (rr - blk * CONV_ROWS) + (CONV_HALO - half) + tap, 1.0, 0.0).astype(BF16)

    for r0 in range(0, tm, CONV_ROWS):
        shifted = _dot(selector, ext_ref[r0:r0 + win, :])
        centre = ext_ref[CONV_HALO + r0:CONV_HALO + r0 + CONV_ROWS, :].astype(F32)
        acc = b_ref[...] + w_ref[half:half + 1, :] * centre
        for n, k in enumerate(side_taps):
            acc = acc + w_ref[k:k + 1, :] * shifted[n * CONV_ROWS:(n + 1) * CONV_ROWS, :]
        o_ref[r0:r0 + CONV_ROWS, :] = _silu(acc).astype(o_ref.dtype)


def _conv(p_main, conv_w, conv_b, tm, tc=512):
    L = p_main.shape[0]
    hb = tm // CONV_HALO
    n_halo = L // CONV_HALO
    c0 = OFF_XBC // tc
    return pl.pallas_call(
        functools.partial(_conv_kernel, tm=tm),
        grid=(L // tm, CONV_CH // tc),
        in_specs=[pl.BlockSpec((CONV_HALO, tc), lambda i, j: (jnp.maximum(i * hb - 1, 0), c0 + j)),
                  pl.BlockSpec((tm, tc), lambda i, j: (i, c0 + j)),
                  pl.BlockSpec((CONV_HALO, tc), lambda i, j: (jnp.minimum((i + 1) * hb, n_halo - 1), c0 + j)),
                  pl.BlockSpec((CONV_W, tc), lambda i, j: (0, j)),
                  pl.BlockSpec((1, tc), lambda i, j: (0, j))],
        out_specs=pl.BlockSpec((tm, tc), lambda i, j: (i, j)),
        out_shape=jax.ShapeDtypeStruct((L, CONV_CH), BF16),
        scratch_shapes=[pltpu.VMEM((tm + 2 * CONV_HALO, tc), BF16)],
        compiler_params=_params("parallel", "parallel"),
        name="conv",
    )(p_main, p_main, p_main, conv_w, conv_b)


def _qkv_kernel(*refs, tm, rope, with_q):
    if with_q:
        q_ref, k_ref, v_ref, cos_ref, sin_ref, qo_ref, ko_ref, vo_ref = refs
    else:
        k_ref, v_ref, ko_ref, vo_ref = refs
    lane = lax.broadcasted_iota(I32, (tm, LANES), 1)
    first = (lane & 31) < 16
    lo = lane < 64

    def rot(x):
        if not rope:
            return x
        r = jnp.where(first, -pltpu.roll(x, LANES - 16, 1), pltpu.roll(x, 16, 1))
        return x * cos_ref[...] + r * sin_ref[...]

    def dup(x, o_ref, p):
        r = pltpu.roll(x, 64, 1)
        o_ref[:, (2 * p) * LANES:(2 * p + 1) * LANES] = jnp.where(lo, x, r).astype(o_ref.dtype)
        o_ref[:, (2 * p + 1) * LANES:(2 * p + 2) * LANES] = jnp.where(lo, r, x).astype(o_ref.dtype)

    if with_q:
        for c in range(ATTN_WIDTH // LANES):
            sl = slice(c * LANES, (c + 1) * LANES)
            qo_ref[:, sl] = (rot(q_ref[:, sl].astype(F32)) * ATTN_SCALE).astype(qo_ref.dtype)
    for p in range(KV_WIDTH // LANES):
        sl = slice(p * LANES, (p + 1) * LANES)
        dup(rot(k_ref[:, sl].astype(F32)), ko_ref, p)
        dup(v_ref[:, sl].astype(F32), vo_ref, p)


def _qkv(p_main, cos, sin, tm, rope, with_q):
    L = p_main.shape[0]
    kv_specs = [pl.BlockSpec((tm, KV_WIDTH), lambda i: (i, OFF_K // KV_WIDTH)),
                pl.BlockSpec((tm, KV_WIDTH), lambda i: (i, OFF_V // KV_WIDTH))]
    kv_out = [pl.BlockSpec((tm, KV_DUP), lambda i: (i, 0)), pl.BlockSpec((tm, KV_DUP), lambda i: (i, 0))]
    kv_shape = [jax.ShapeDtypeStruct((L, KV_DUP), BF16), jax.ShapeDtypeStruct((L, KV_DUP), BF16)]
    if with_q:
        in_specs = ([pl.BlockSpec((tm, ATTN_WIDTH), lambda i: (i, OFF_Q // ATTN_WIDTH))] + kv_specs
                    + [pl.BlockSpec((tm, LANES), lambda i: (i, 0)), pl.BlockSpec((tm, LANES), lambda i: (i, 0))])
        out_specs = [pl.BlockSpec((tm, ATTN_WIDTH), lambda i: (i, 0))] + kv_out
        out_shape = [jax.ShapeDtypeStruct((L, ATTN_WIDTH), BF16)] + kv_shape
        args = (p_main, p_main, p_main, cos, sin)
    else:
        in_specs, out_specs, out_shape, args = kv_specs, kv_out, kv_shape, (p_main, p_main)
    return pl.pallas_call(
        functools.partial(_qkv_kernel, tm=tm, rope=rope, with_q=with_q),
        grid=(L // tm,),
        in_specs=in_specs, out_specs=out_specs, out_shape=out_shape,
        compiler_params=_params("parallel"),
        name="qkv_rope" if with_q else "kv_ctx",
    )(*args)


PAIRS_PER_GROUP = (SSD_HEADS // SSD_GROUPS) // 2
GROUP_W = (SSD_HEADS // SSD_GROUPS) * SSD_HEAD_DIM


def _ssd_kernel(xf_ref, xb_ref, dtf_ref, dtb_ref, s0_ref, bias_ref, alog_ref, yf_ref, yb_ref, s_ref):
    Q = SSD_CHUNK

    @pl.when(pl.program_id(0) == 0)
    def _():
        s_ref[...] = s0_ref[...]

    row = lax.broadcasted_iota(I32, (Q, Q), 0)
    col = lax.broadcasted_iota(I32, (Q, Q), 1)
    lo = col < 64
    lo_row = lax.broadcasted_iota(I32, (1, Q), 1) < 64
    neg_a = -jnp.exp(alog_ref[...])

    for d, (x_ref, dt_ref, y_ref) in enumerate(((xf_ref, dtf_ref, yf_ref), (xb_ref, dtb_ref, yb_ref))):
        tri = (row >= col) if d == 0 else (row <= col)
        off = d * SSD_HEADS
        z = dt_ref[...] + bias_ref[...]
        dt = jnp.maximum(z, 0.0) + jnp.log1p(jnp.exp(-jnp.abs(z)))
        da = dt * neg_a
        a = _dot(jnp.where(tri, 1.0, 0.0), da, precision=HIGHEST)
        a_t = a.T
        dt_t = dt.T
        edge = a_t[:, Q - 1:Q] if d == 0 else a_t[:, 0:1]
        w_t = dt_t * jnp.exp(edge - a_t)
        dec = jnp.exp(edge)

        for g in range(SSD_GROUPS):
            bg = x_ref[:, D_INNER + g * SSD_STATE:D_INNER + (g + 1) * SSD_STATE]
            cg = x_ref[:, D_INNER + (SSD_GROUPS + g) * SSD_STATE:D_INNER + (SSD_GROUPS + g + 1) * SSD_STATE]
            cb = _dot_nt(cg, bg)
            b_t = bg.astype(F32).T
            s_prev = s_ref[d, g]
            y_off = _dot(cg, s_prev.astype(BF16))
            for p in range(PAIRS_PER_GROUP):
                h0 = g * (SSD_HEADS // SSD_GROUPS) + 2 * p
                c0 = h0 * SSD_HEAD_DIM
                xpair = x_ref[:, c0:c0 + LANES]
                zero = jnp.zeros_like(xpair)
                y_diag = None
                st = None
                a_cols = []
                for t, xm in enumerate((jnp.where(lo, xpair, zero), jnp.where(lo, zero, xpair))):
                    h = off + h0 + t
                    a_col = jnp.broadcast_to(a[:, h:h + 1], (Q, Q))
                    a_cols.append(a_col)
                    seg = a_col - a_t[h:h + 1, :]
                    lm = cb * jnp.exp(jnp.where(tri, seg, -jnp.inf)) * dt_t[h:h + 1, :]
                    yd = _dot(lm.astype(BF16), xm)
                    bw = (b_t * w_t[h:h + 1, :]).astype(BF16)
                    sd = _dot(bw, xm)
                    y_diag = yd if y_diag is None else y_diag + yd
                    st = sd if st is None else st + sd
                h = off + h0
                ea_pair = jnp.exp(jnp.where(lo, a_cols[0], a_cols[1]))
                ps = slice(p * LANES, (p + 1) * LANES)
                y_ref[:, c0:c0 + LANES] = (y_diag + y_off[:, ps] * ea_pair).astype(y_ref.dtype)
                dec_pair = jnp.where(lo_row, dec[h:h + 1, :], dec[h + 1:h + 2, :])
                s_ref[d, g, :, ps] = s_prev[:, ps] * dec_pair + st


def _ssd(xc, dt, s0, bias, alog):
    L = xc.shape[0]
    nc = L // SSD_CHUNK
    sshape = (2, SSD_GROUPS, SSD_STATE, GROUP_W)
    return pl.pallas_call(
        _ssd_kernel,
        grid=(nc,),
        in_specs=[pl.BlockSpec((SSD_CHUNK, CONV_CH), lambda c: (c, 0)),
                  pl.BlockSpec((SSD_CHUNK, CONV_CH), lambda c: (nc - 1 - c, 0)),
                  pl.BlockSpec((SSD_CHUNK, LANES), lambda c: (c, 0)),
                  pl.BlockSpec((SSD_CHUNK, LANES), lambda c: (nc - 1 - c, 0)),
                  pl.BlockSpec(sshape, lambda c: (0, 0, 0, 0)),
                  pl.BlockSpec((1, LANES), lambda c: (0, 0)),
                  pl.BlockSpec((1, LANES), lambda c: (0, 0))],
        out_specs=[pl.BlockSpec((SSD_CHUNK, D_INNER), lambda c: (c, 0)),
                   pl.BlockSpec((SSD_CHUNK, D_INNER), lambda c: (nc - 1 - c, 0)),
                   pl.BlockSpec(sshape, lambda c: (0, 0, 0, 0))],
        out_shape=[jax.ShapeDtypeStruct((L, D_INNER), BF16),
                   jax.ShapeDtypeStruct((L, D_INNER), BF16),
                   jax.ShapeDtypeStruct(sshape, F32)],
        compiler_params=_params("arbitrary"),
        name="ssd",
    )(xc, xc, dt, dt, s0, bias, alog)


def _attn_kernel(sink_ref, q_ref, kp_ref, kc_ref, kn_ref, vp_ref, vc_ref, vn_ref, kx_ref, vx_ref, o_ref, *, s_ctx):
    i = pl.program_id(0)
    last = pl.num_programs(0) - 1
    B = ATTN_BLOCK
    span = 3 * B
    width = span + s_ctx
    t = lax.broadcasted_iota(I32, (B, width), 0)
    j = lax.broadcasted_iota(I32, (B, width), 1)
    rel = j - B - t
    j_min = jnp.where(i > 0, 0, B)
    j_max = jnp.where(i < last, span, 2 * B)
    valid = (j >= span) | ((jnp.abs(rel) <= WINDOW) & (j >= j_min) & (j < j_max))
    lo = lax.broadcasted_iota(I32, (B, LANES), 1) < 64
    group = ATTN_Q_HEADS // ATTN_KV_HEADS

    for kh in range(ATTN_KV_HEADS):
        sl = slice(kh * LANES, (kh + 1) * LANES)
        k_all = jnp.concatenate([kp_ref[:, sl], kc_ref[:, sl], kn_ref[:, sl], kx_ref[:, sl]], axis=0)
        v_all = jnp.concatenate([vp_ref[:, sl], vc_ref[:, sl], vn_ref[:, sl], vx_ref[:, sl]], axis=0)
        q_rows = []
        for u in range(group // 2):
            c = (group // 2) * kh + u
            qpair = q_ref[:, c * LANES:(c + 1) * LANES]
            zero_q = jnp.zeros_like(qpair)
            q_rows += [jnp.where(lo, qpair, zero_q), jnp.where(lo, zero_q, qpair)]
        s_all = _dot_nt(jnp.concatenate(q_rows, axis=0), k_all)
        e_rows, inv = [], []
        for b in range(group):
            sink = sink_ref[group * kh + b]
            s = jnp.where(valid, s_all[b * B:(b + 1) * B, :], -jnp.inf)
            m = jnp.maximum(jnp.max(s, axis=1, keepdims=True), sink)
            e = jnp.exp(s - m)
            inv.append(1.0 / (jnp.sum(e, axis=1, keepdims=True) + jnp.exp(sink - m)))
            e_rows.append(e.astype(BF16))
        o_all = _dot(jnp.concatenate(e_rows, axis=0), v_all)
        for u in range(group // 2):
            c = (group // 2) * kh + u
            b0 = 2 * u
            o = jnp.where(lo, o_all[b0 * B:(b0 + 1) * B, :] * inv[b0], o_all[(b0 + 1) * B:(b0 + 2) * B, :] * inv[b0 + 1])
            o_ref[:, c * LANES:(c + 1) * LANES] = o.astype(o_ref.dtype)


def _attn(sink, q, kd, vd, kx, vx):
    L = q.shape[0]
    s_ctx = kx.shape[0]
    nb = L // ATTN_BLOCK
    prev = lambda i: (jnp.maximum(i - 1, 0), 0)
    cur = lambda i: (i, 0)
    nxt = lambda i: (jnp.minimum(i + 1, nb - 1), 0)
    kvb = lambda f: pl.BlockSpec((ATTN_BLOCK, KV_DUP), f)
    return pl.pallas_call(
        functools.partial(_attn_kernel, s_ctx=s_ctx),
        grid=(nb,),
        in_specs=[pl.BlockSpec(memory_space=pltpu.SMEM),
                  pl.BlockSpec((ATTN_BLOCK, ATTN_WIDTH), cur),
                  kvb(prev), kvb(cur), kvb(nxt), kvb(prev), kvb(cur), kvb(nxt),
                  pl.BlockSpec((s_ctx, KV_DUP), lambda i: (0, 0)),
                  pl.BlockSpec((s_ctx, KV_DUP), lambda i: (0, 0))],
        out_specs=pl.BlockSpec((ATTN_BLOCK, ATTN_WIDTH), cur),
        out_shape=jax.ShapeDtypeStruct((L, ATTN_WIDTH), BF16),
        compiler_params=_params("parallel"),
        name="attn",
    )(sink, q, kd, kd, kd, vd, vd, vd, kx, vx)


def _out_kernel(yf_ref, yb_ref, xs_ref, z_ref, g_ref, ya_ref, x_ref, dsk_ref, nw_ref, wssd_ref, wattn_ref, wo_ref,
                bo_ref, g1_ref, ln1g_ref, ln1b_ref, sh2_ref, sc2_ref, wrh_ref, wrl_ref, x1_ref, h2_ref, aff_ref):
    y = yf_ref[...].astype(F32) + yb_ref[...].astype(F32) + dsk_ref[...] * xs_ref[...].astype(F32)
    y = y * _silu(z_ref[...].astype(F32))
    y = y * lax.rsqrt(jnp.mean(y * y, -1, keepdims=True) + LN_EPS) * nw_ref[...]
    br_ssd = _dot(y.astype(BF16), wssd_ref[...])
    br_attn = _dot(ya_ref[...], wattn_ref[...])
    g_ssd = jax.nn.sigmoid(g_ref[:, :D_MODEL].astype(F32))
    g_attn = jax.nn.sigmoid(g_ref[:, D_MODEL:].astype(F32))
    m = g_ssd * br_ssd + g_attn * br_attn
    o = _dot(m.astype(BF16), wo_ref[...]) + bo_ref[...]
    x1 = _ln(DEEPNORM_ALPHA * x_ref[...] + g1_ref[...] * o) * ln1g_ref[...] + ln1b_ref[...]
    x1_ref[...] = x1
    h2 = _ln(x1) * (1.0 + sc2_ref[...]) + sh2_ref[...]
    h2_ref[...] = h2.astype(h2_ref.dtype)
    h_hi = h2.astype(BF16)
    h_lo = (h2 - h_hi.astype(F32)).astype(BF16)
    logits = _dot_nt(h_hi, wrh_ref[...]) + (_dot_nt(h_hi, wrl_ref[...]) + _dot_nt(h_lo, wrh_ref[...]))
    lane = lax.broadcasted_iota(I32, logits.shape, 1)
    logits = jnp.where(lane < N_EXPERTS, logits, -jnp.inf)
    e = jnp.exp(logits - jnp.max(logits, axis=1, keepdims=True))
    aff = e / jnp.sum(e, axis=1, keepdims=True)
    aff_ref[...] = aff.T[:N_EXPERTS, :]


def _out(yf, yb, xc, p_main, ya, x, dsk, nw, wssd, wattn, wo, bo, g1, ln1g, ln1b, sh2, sc2, wrh, wrl, tm):
    L = x.shape[0]
    row = lambda w: pl.BlockSpec((1, w), lambda i: (0, 0))
    full = lambda a: pl.BlockSpec(a.shape, lambda i: (0, 0))
    return pl.pallas_call(
        _out_kernel,
        grid=(L // tm,),
        in_specs=[pl.BlockSpec((tm, D_INNER), lambda i: (i, 0)),
                  pl.BlockSpec((tm, D_INNER), lambda i: (i, 0)),
                  pl.BlockSpec((tm, D_INNER), lambda i: (i, 0)),
                  pl.BlockSpec((tm, D_INNER), lambda i: (i, OFF_Z // D_INNER)),
                  pl.BlockSpec((tm, 2 * D_MODEL), lambda i: (i, OFF_G // (2 * D_MODEL))),
                  pl.BlockSpec((tm, ATTN_WIDTH), lambda i: (i, 0)),
                  pl.BlockSpec((tm, D_MODEL), lambda i: (i, 0)),
                  row(D_INNER), row(D_INNER), full(wssd), full(wattn), full(wo),
                  row(D_MODEL), row(D_MODEL), row(D_MODEL), row(D_MODEL), row(D_MODEL), row(D_MODEL),
                  full(wrh), full(wrl)],
        out_specs=[pl.BlockSpec((tm, D_MODEL), lambda i: (i, 0)),
                   pl.BlockSpec((tm, D_MODEL), lambda i: (i, 0)),
                   pl.BlockSpec((N_EXPERTS, tm), lambda i: (0, i))],
        out_shape=[jax.ShapeDtypeStruct((L, D_MODEL), F32),
                   jax.ShapeDtypeStruct((L, D_MODEL), BF16),
                   jax.ShapeDtypeStruct((N_EXPERTS, L), F32)],
        compiler_params=_params("parallel"),
        name="out_proj",
    )(yf, yb, xc, p_main, p_main, ya, x, dsk, nw, wssd, wattn, wo, bo, g1, ln1g, ln1b, sh2, sc2, wrh, wrl)


def _count(mask):
    c = jnp.sum(jnp.where(mask, 1.0, 0.0), axis=1, keepdims=True)
    return jnp.sum(c, axis=0, keepdims=True)


def _route_kernel(aff_ref, rank_ref, start_ref, cnt_ref, sel_ref, *, cap, nrows):
    E = N_EXPERTS
    R = nrows
    tok = (lax.broadcasted_iota(I32, (R, LANES), 0) * LANES + lax.broadcasted_iota(I32, (R, LANES), 1))
    capf = jnp.float32(cap)

    def as_f32(word):
        return pltpu.bitcast(word, F32)

    def thr_body(it, ts):
        bit = lax.shift_left(jnp.int32(1), 30 - it)
        out = []
        for e in range(E):
            cand = ts[e] | bit
            out.append(jnp.where(_count(aff_ref[e] >= as_f32(cand)) >= capf, cand, ts[e]))
        return tuple(out)

    ts = lax.fori_loop(0, 31, thr_body, tuple(jnp.zeros((1, 1), I32) for _ in range(E)))
    thr = [as_f32(t) for t in ts]
    need = [capf - _count(aff_ref[e] > thr[e]) for e in range(E)]

    nbits = max((R * LANES - 1).bit_length(), 1)

    def tie_body(it, js):
        bit = lax.shift_left(jnp.int32(1), nbits - 1 - it)
        out = []
        for e in range(E):
            cand = js[e] | bit
            f = _count((aff_ref[e] == thr[e]) & (tok < cand))
            out.append(jnp.where(f < need[e], cand, js[e]))
        return tuple(out)

    js = lax.fori_loop(0, nbits, tie_body, tuple(jnp.zeros((1, 1), I32) for _ in range(E)))
    for e in range(E):
        a = aff_ref[e]
        sel = (a > thr[e]) | ((a == thr[e]) & (tok <= js[e]))
        sel_ref[e] = jnp.where(sel, 1.0, 0.0)

    li = lax.broadcasted_iota(I32, (LANES, LANES), 0)
    lj = lax.broadcasted_iota(I32, (LANES, LANES), 1)
    upper_incl = jnp.where(li <= lj, 1.0, 0.0).astype(BF16)
    ri = lax.broadcasted_iota(I32, (R, R), 0)
    rj = lax.broadcasted_iota(I32, (R, R), 1)
    row_lower = jnp.where(ri >= rj, 1.0, 0.0).astype(BF16)
    lane_id = lax.broadcasted_iota(I32, (R, LANES), 1)

    def expert_body(e, carry):
        start_acc, cnt_acc = carry
        sel = sel_ref[e]
        w = _dot(sel.astype(BF16), upper_incl)
        tot_b = jnp.broadcast_to(jnp.sum(sel, axis=1, keepdims=True), (R, LANES))
        seg_b = jnp.floor((tot_b + (SUBLANES - 1)) * (1.0 / SUBLANES)) * SUBLANES
        cum_incl = _dot(row_lower, seg_b.astype(BF16))
        rank_ref[e] = jnp.where(sel > 0.0, w - 1.0, -1.0)
        start_acc = jnp.where(lane_id == e, cum_incl - seg_b, start_acc)
        cnt_acc = jnp.where(lane_id == e, tot_b, cnt_acc)
        return start_acc, cnt_acc

    zero = jnp.zeros((R, LANES), F32)
    start_acc, cnt_acc = lax.fori_loop(0, E, expert_body, (zero, zero))
    start_ref[...] = start_acc.astype(I32)
    cnt_ref[...] = cnt_acc.astype(I32)


def _route(aff3, cap):
    E, R, _ = aff3.shape
    return pl.pallas_call(
        functools.partial(_route_kernel, cap=cap, nrows=R),
        out_shape=[jax.ShapeDtypeStruct((E, R, LANES), F32),
                   jax.ShapeDtypeStruct((R, LANES), I32),
                   jax.ShapeDtypeStruct((R, LANES), I32)],
        scratch_shapes=[pltpu.VMEM((E, R, LANES), F32)],
        compiler_params=pltpu.CompilerParams(vmem_limit_bytes=VMEM_LIMIT_BYTES),
        name="route",
    )(aff3)


SLOTS = 32
WIN = N_EXPERTS * SLOTS


def _slot_onehot(tgt, values, k):
    sub = lax.broadcasted_iota(I32, (SLOTS, tgt.shape[1]), 0).astype(F32) + (SLOTS * k).astype(F32)
    rows = []
    for e in range(N_EXPERTS):
        hit = jnp.broadcast_to(tgt[e:e + 1, :], sub.shape) == sub
        rows.append(jnp.where(hit, jnp.broadcast_to(values[e:e + 1, :], sub.shape), 0.0))
    return jnp.concatenate(rows, axis=0).astype(BF16)


def _rounds(cnt_ref, r):
    n = cnt_ref[0, r]
    for e in range(1, N_EXPERTS):
        n = jnp.maximum(n, cnt_ref[e, r])
    return lax.shift_right_logical(n + (SLOTS - 1), SLOTS.bit_length() - 1)


def _dispatch_kernel(start_ref, cnt_ref, h_ref, rank_ref, xs_hbm, stage_ref, ovf_ref, sem, osem):
    r = pl.program_id(0)
    last = pl.num_programs(0) - 1
    slot = lax.rem(r, 2)
    rank = rank_ref[:, 0, 0, :]
    ones = jnp.ones_like(rank)

    def compact(k):
        return _dot(_slot_onehot(rank, ones, k), h_ref[...])

    def window_copy(src_ref, e, k, s):
        return pltpu.make_async_copy(src_ref.at[pl.ds(e * SLOTS, SLOTS), :],
                                     xs_hbm.at[e, pl.ds(pl.multiple_of(start_ref[e, r] + SLOTS * k, SUBLANES), SLOTS), :], s)

    def stage_wait(sl):
        pltpu.make_async_copy(stage_ref.at[sl], stage_ref.at[sl], sem.at[sl]).wait()

    stage_ref[slot] = compact(jnp.int32(0))

    @pl.when(r > 0)
    def _():
        stage_wait(1 - slot)

    for e in range(N_EXPERTS):
        window_copy(stage_ref.at[slot], e, 0, sem.at[slot]).start()

    def extra(k, _):
        ovf_ref[...] = compact(k)
        for e in range(N_EXPERTS):
            @pl.when(cnt_ref[e, r] > SLOTS * k)
            def _():
                window_copy(ovf_ref, e, k, osem).start()
        for e in range(N_EXPERTS):
            @pl.when(cnt_ref[e, r] > SLOTS * k)
            def _():
                window_copy(ovf_ref, e, k, osem).wait()
        return 0

    lax.fori_loop(1, _rounds(cnt_ref, r), extra, 0)

    def tail_copies(act):
        n_rows = xs_hbm.shape[1]
        for e in range(N_EXPERTS):
            n_win = jnp.maximum(lax.shift_right_logical(cnt_ref[e, r] + (SLOTS - 1), SLOTS.bit_length() - 1), 1)
            off0 = start_ref[e, r] + SLOTS * n_win
            length = n_rows - off0
            n_full = lax.div(length, WIN)

            def full(i, _, e=e, off0=off0):
                act(pltpu.make_async_copy(ovf_ref, xs_hbm.at[e, pl.ds(pl.multiple_of(off0 + i * WIN, SUBLANES), WIN), :],
                                          osem))
                return 0

            lax.fori_loop(0, n_full, full, 0)
            off = off0 + n_full * WIN
            rem = length - n_full * WIN
            size = WIN // 2
            while size >= SUBLANES:
                @pl.when((rem & size) != 0)
                def _(e=e, off=off, size=size):
                    act(pltpu.make_async_copy(ovf_ref.at[pl.ds(0, size), :],
                                              xs_hbm.at[e, pl.ds(pl.multiple_of(off, SUBLANES), size), :], osem))
                off = off + (rem & size)
                size //= 2

    @pl.when(r == last)
    def _():
        stage_wait(slot)
        ovf_ref[...] = jnp.zeros(ovf_ref.shape, F32)
        tail_copies(lambda cp: cp.start())
        tail_copies(lambda cp: cp.wait())


def _padded_cap(cap, n_tiles, ts):
    worst = cap + (SUBLANES - 1) * n_tiles
    return -(-worst // ts) * ts


def _dispatch(start, cnt, h2, rank4, cap):
    L = h2.shape[0]
    T = LANES
    return pl.pallas_call(
        _dispatch_kernel,
        grid_spec=pltpu.PrefetchScalarGridSpec(
            num_scalar_prefetch=2,
            grid=(L // T,),
            in_specs=[pl.BlockSpec((T, D_MODEL), lambda r, *_: (r, 0)),
                      pl.BlockSpec((N_EXPERTS, 1, 1, T), lambda r, *_: (0, r, 0, 0))],
            out_specs=pl.BlockSpec(memory_space=pl.ANY),
            scratch_shapes=[pltpu.VMEM((2, WIN, D_MODEL), F32),
                            pltpu.VMEM((WIN, D_MODEL), F32),
                            pltpu.SemaphoreType.DMA((2,)),
                            pltpu.SemaphoreType.DMA(())]),
        out_shape=jax.ShapeDtypeStruct((N_EXPERTS, cap + SLOTS, D_MODEL), F32),
        compiler_params=_params("arbitrary"),
        name="dispatch",
    )(start, cnt, h2, rank4)


def _expert_kernel(start_ref, cnt_ref, xs_ref, wgu_ref, wd_ref, o_ref, wgu_bf, wd_bf, *, ts):
    e = pl.program_id(0)
    j = pl.program_id(1)
    last_tile = start_ref.shape[1] - 1
    seg = lax.shift_left(lax.shift_right_logical(cnt_ref[e, last_tile] + (SUBLANES - 1), 3), 3)
    total = start_ref[e, last_tile] + seg
    live = total - j * ts

    @pl.when(j == 0)
    def _():
        rows = 128
        for r0 in range(0, D_MODEL, rows):
            wgu_bf[r0:r0 + rows, :] = wgu_ref[0, r0:r0 + rows, :].astype(BF16)
        for r0 in range(0, EXPERT_FF, rows):
            wd_bf[r0:r0 + rows, :] = wd_ref[0, r0:r0 + rows, :].astype(BF16)

    @pl.when(live > 0)
    def _():
        row = lax.broadcasted_iota(I32, (ts, D_MODEL), 0)
        xs = jnp.where(row < live, xs_ref[0], 0.0).astype(BF16)
        gu = _dot(xs, wgu_bf[...])
        act = _silu(gu[:, :EXPERT_FF]) * gu[:, EXPERT_FF:]
        o_ref[0] = _dot(act.astype(BF16), wd_bf[...])

    @pl.when(live <= 0)
    def _():
        o_ref[0] = jnp.zeros((ts, D_MODEL), F32)


def _expert(start, cnt, xs, wgu, wd, capp, ts):
    E = xs.shape[0]
    return pl.pallas_call(
        functools.partial(_expert_kernel, ts=ts),
        grid_spec=pltpu.PrefetchScalarGridSpec(
            num_scalar_prefetch=2,
            grid=(E, capp // ts),
            in_specs=[pl.BlockSpec((1, ts, D_MODEL), lambda e, j, *_: (e, j, 0)),
                      pl.BlockSpec((1, D_MODEL, 2 * EXPERT_FF), lambda e, j, *_: (e, 0, 0)),
                      pl.BlockSpec((1, EXPERT_FF, D_MODEL), lambda e, j, *_: (e, 0, 0))],
            out_specs=pl.BlockSpec((1, ts, D_MODEL), lambda e, j, *_: (e, j, 0)),
            scratch_shapes=[pltpu.VMEM((D_MODEL, 2 * EXPERT_FF), BF16),
                            pltpu.VMEM((EXPERT_FF, D_MODEL), BF16)]),
        out_shape=jax.ShapeDtypeStruct((E, capp, D_MODEL), F32),
        compiler_params=_params("parallel", "arbitrary"),
        name="expert_ffn",
    )(start, cnt, xs, wgu, wd)


def _combine_kernel(start_ref, cnt_ref, ye_hbm, x1_ref, aff_ref, rank_ref, g2_ref, lng_ref, lnb_ref,
                    o_ref, buf_ref, sem, *, cap):
    r = pl.program_id(0)
    last = pl.num_programs(0) - 1
    slot = lax.rem(r, 2)

    def win_start(e, rr, k):
        return pl.multiple_of(jnp.minimum(start_ref[e, rr] + SLOTS * k, cap - SLOTS), SUBLANES)

    def fetch(rr, k, sl):
        for e in range(N_EXPERTS):
            pltpu.make_async_copy(ye_hbm.at[e, pl.ds(win_start(e, rr, k), SLOTS), :],
                                  buf_ref.at[sl, pl.ds(e * SLOTS, SLOTS), :], sem.at[sl]).start()

    def fetch_wait(sl):
        pltpu.make_async_copy(buf_ref.at[sl], buf_ref.at[sl], sem.at[sl]).wait()

    @pl.when(r == 0)
    def _():
        fetch(r, 0, slot)

    @pl.when(r < last)
    def _():
        fetch(r + 1, 0, 1 - slot)

    rank = rank_ref[:, 0, 0, :]
    aff = aff_ref[...]
    g_hi = aff.astype(BF16).astype(F32)
    g_lo = aff - g_hi
    e_id = lax.broadcasted_iota(I32, rank.shape, 0)

    def contribution(k, sl):
        shift = jnp.zeros_like(rank)
        for e in range(N_EXPERTS):
            d = start_ref[e, r] + SLOTS * k - win_start(e, r, k)
            shift = jnp.where(e_id == e, d.astype(F32), shift)
        base = (SLOTS * k).astype(F32)
        in_round = (rank >= base) & (rank < base + SLOTS)
        tgt = jnp.where(in_round, rank + shift, -1.0)
        yb = buf_ref[sl].astype(BF16)
        tn = (((0,), (0,)), ((), ()))
        return (lax.dot_general(_slot_onehot(tgt, g_hi, k), yb, tn, preferred_element_type=F32)
                + lax.dot_general(_slot_onehot(tgt, g_lo, k), yb, tn, preferred_element_type=F32))

    fetch_wait(slot)
    moe = contribution(jnp.int32(0), slot)

    def extra(k, acc):
        fetch(r, k, slot)
        fetch_wait(slot)
        return acc + contribution(k, slot)

    moe = lax.fori_loop(1, _rounds(cnt_ref, r), extra, moe)
    y = _ln(DEEPNORM_ALPHA * x1_ref[...] + g2_ref[...] * moe)
    o_ref[...] = y * lng_ref[...] + lnb_ref[...]


def _combine(start, cnt, ye, x1, aff_t, rank4, g2, lng, lnb):
    L = x1.shape[0]
    E, cap, _ = ye.shape
    T = LANES
    row = pl.BlockSpec((1, D_MODEL), lambda r, *_: (0, 0))
    return pl.pallas_call(
        functools.partial(_combine_kernel, cap=cap),
        grid_spec=pltpu.PrefetchScalarGridSpec(
            num_scalar_prefetch=2,
            grid=(L // T,),
            in_specs=[pl.BlockSpec(memory_space=pl.ANY),
                      pl.BlockSpec((T, D_MODEL), lambda r, *_: (r, 0)),
                      pl.BlockSpec((E, T), lambda r, *_: (0, r)),
                      pl.BlockSpec((E, 1, 1, T), lambda r, *_: (0, r, 0, 0)),
                      row, row, row],
            out_specs=pl.BlockSpec((T, D_MODEL), lambda r, *_: (r, 0)),
            scratch_shapes=[pltpu.VMEM((2, WIN, D_MODEL), F32),
                            pltpu.SemaphoreType.DMA((2,))]),
        out_shape=jax.ShapeDtypeStruct((L, D_MODEL), F32),
        compiler_params=_params("arbitrary"),
        name="combine",
    )(start, cnt, ye, x1, aff_t, rank4, g2, lng, lnb)


def _rope_tables(L):
    t = jnp.arange(L, dtype=jnp.int32)
    row = (t // GRID_W).astype(F32)
    col = (t % GRID_W).astype(F32)
    inv_freq = ROPE_BASE ** (-jnp.arange(0, ROPE_AXIS_DIM, 2, dtype=F32) / ROPE_AXIS_DIM)
    ang_r = row[:, None] * inv_freq[None, :]
    ang_c = col[:, None] * inv_freq[None, :]
    ang = jnp.concatenate([ang_r, ang_r, ang_c, ang_c], -1)
    ang = jnp.concatenate([ang, ang], -1)
    return jnp.cos(ang), jnp.sin(ang)


def _row_tile(L, pref):
    return pref if L % pref == 0 else L


def kernel(x, c, ctx, c_ctx, w_ada, b_ada, w_in, conv_w, conv_b, a_log, dt_bias, d_skip, ssd_norm_w, attn_sink,
           w_ssd_br, w_attn_br, w_o, b_o, ln1_g, ln1_b, w_router, w_gate_up, w_down, ln2_g, ln2_b):
    assert x.shape[0] == 1 and w_in.shape[0] == 1, "single batch element, depth 1"
    L = x.shape[1]
    S = ctx.shape[1]
    assert L % (LANES * SUBLANES) == 0 and S % SSD_CHUNK == 0
    cap = CAPACITY_FACTOR * L // N_EXPERTS
    x2, ctx2 = x[0], ctx[0]

    cc = jnp.zeros((SUBLANES, D_MODEL), F32).at[0].set(c[0]).at[1].set(c_ctx)
    mod = _ada(cc, w_ada[0], b_ada[0][None, :])
    sh1, sc1, g1, sh2, sc2, g2 = [mod[0:1, k * D_MODEL:(k + 1) * D_MODEL] for k in range(6)]
    csh1, csc1 = mod[1:2, 0:D_MODEL], mod[1:2, D_MODEL:2 * D_MODEL]

    w = w_in[0]
    o = 0
    parts = {}
    for name, width in (("z", D_INNER), ("xbc", CONV_CH), ("dt", 2 * SSD_HEADS), ("q", ATTN_WIDTH),
                        ("k", KV_WIDTH), ("v", KV_WIDTH), ("g", 2 * D_MODEL)):
        parts[name] = w[:, o:o + width]
        o += width
    w_main = jnp.concatenate([parts[n] for n in ("z", "g", "xbc", "q", "k", "v")], axis=1).astype(BF16)
    w_dt = jnp.pad(parts["dt"], ((0, 0), (0, LANES - 2 * SSD_HEADS))).astype(BF16)

    pad_heads = lambda v: jnp.pad(v.reshape(1, 2 * SSD_HEADS), ((0, 0), (0, LANES - 2 * SSD_HEADS)))
    bias_row, alog_row = pad_heads(dt_bias[0]), pad_heads(a_log[0])
    conv_b2 = conv_b[0][None, :]

    tm_c = _row_tile(S, 256)
    pc, dtc = _ln_proj(ctx2, csh1, csc1, w_main, w_dt, tm_c)
    xcc = _conv(pc, conv_w[0], conv_b2, tm_c)
    kx, vx = _qkv(pc, None, None, tm_c, rope=False, with_q=False)
    s_zero = jnp.zeros((2, SSD_GROUPS, SSD_STATE, GROUP_W), F32)
    _, _, s_ctx = _ssd(xcc, dtc, s_zero, bias_row, alog_row)

    p_main, dt = _ln_proj(x2, sh1, sc1, w_main, w_dt, _row_tile(L, 1024))
    xc = _conv(p_main, conv_w[0], conv_b2, _row_tile(L, 512))
    cos, sin = _rope_tables(L)
    q, kd, vd = _qkv(p_main, cos, sin, _row_tile(L, 512), rope=True, with_q=True)
    yf, yb, _ = _ssd(xc, dt, s_ctx, bias_row, alog_row)
    ya = _attn(attn_sink[0], q, kd, vd, kx, vx)

    dsk = jnp.repeat(d_skip[0], SSD_HEAD_DIM)[None, :]
    wr = jnp.pad(w_router[0].T, ((0, LANES - N_EXPERTS), (0, 0)))
    wr_hi = wr.astype(BF16)
    wr_lo = (wr - wr_hi.astype(F32)).astype(BF16)
    x1, h2, aff_t = _out(yf, yb, xc, p_main, ya, x2, dsk, ssd_norm_w[0][None, :],
                         w_ssd_br[0].astype(BF16), w_attn_br[0].astype(BF16), w_o[0].astype(BF16),
                         b_o[0][None, :], g1, ln1_g[0][None, :], ln1_b[0][None, :], sh2, sc2, wr_hi, wr_lo,
                         _row_tile(L, 256))

    R = L // LANES
    rank, start_t, cnt_t = _route(aff_t.reshape(N_EXPERTS, R, LANES), cap)
    rank4 = rank.reshape(N_EXPERTS, R, 1, LANES)
    start = start_t[:, :N_EXPERTS].T
    cnt = cnt_t[:, :N_EXPERTS].T
    ts = min(cap, 512)
    capp = _padded_cap(cap, R, ts)
    xs = _dispatch(start, cnt, h2, rank4, capp)
    ye = _expert(start, cnt, xs, w_gate_up[0], w_down[0], capp, ts)
    out = _combine(start, cnt, ye, x1, aff_t, rank4, g2, ln2_g[0][None, :], ln2_b[0][None, :])
    return out[None]
```

```python
import functools

import numpy as np
import jax
import jax.numpy as jnp
from jax import lax
from jax.experimental import pallas as pl
from jax.experimental.pallas import tpu as pltpu

F32 = jnp.float32
BF16 = jnp.bfloat16
I32 = jnp.int32
HIGHEST = lax.Precision.HIGHEST
LOG2_E = 1.4426950408889634

LANES = 128
SUBLANES = 8
VMEM_LIMIT_BYTES = 56 * 1024 * 1024

D_MODEL = 1024
GRID_W = 64
LN_EPS = 1e-5
DEPTH = 1
DEEPNORM_ALPHA = (2.0 * DEPTH) ** 0.25
D_INNER = 2048
SSD_HEAD_DIM = 64
SSD_HEADS = 32
SSD_GROUPS = 4
SSD_STATE = 128
SSD_CHUNK = 128
CONV_W = 5
CONV_CH = D_INNER + 2 * SSD_GROUPS * SSD_STATE
ATTN_HEAD_DIM = 64
ATTN_Q_HEADS = 16
ATTN_KV_HEADS = 4
ATTN_WIDTH = 1024
KV_WIDTH = 256
WINDOW = 128
ATTN_BLOCK = 128
ATTN_SCALE = ATTN_HEAD_DIM ** -0.5
ROPE_BASE = 10000.0
ROPE_AXIS_DIM = ATTN_HEAD_DIM // 2
N_EXPERTS = 16
EXPERT_FF = 1024
CAPACITY_FACTOR = 2

OFF_Z = 0
OFF_G = OFF_Z + D_INNER
OFF_XBC = OFF_G + 2 * D_MODEL
OFF_Q = OFF_XBC + CONV_CH
OFF_K = OFF_Q + ATTN_WIDTH
OFF_V = OFF_K + KV_WIDTH
N_MAIN = OFF_V + KV_WIDTH
PROJ_TN = N_MAIN // 4
KV_DUP = 2 * KV_WIDTH


def _params(*sem):
    return pltpu.CompilerParams(dimension_semantics=sem, vmem_limit_bytes=VMEM_LIMIT_BYTES)


def _sigmoid(x):
    return 1.0 / (1.0 + jnp.exp2(x * (-LOG2_E)))


def _silu(x):
    return x * _sigmoid(x)


def _dot(a, b, **kw):
    return jnp.dot(a, b, preferred_element_type=F32, **kw)


def _dot_nt(a, b, **kw):
    return lax.dot_general(a, b, (((1,), (1,)), ((), ())), preferred_element_type=F32, **kw)


def _ln(x):
    mu = jnp.mean(x, -1, keepdims=True)
    xc = x - mu
    var = jnp.mean(xc * xc, -1, keepdims=True)
    return xc * lax.rsqrt(var + LN_EPS)


def _ada_kernel(s_ref, w_ref, b_ref, o_ref):
    s = _silu(s_ref[...])
    o_ref[...] = _dot(s, w_ref[...], precision=HIGHEST) + b_ref[...]


def _ada(cc, w_ada, b_ada):
    n = w_ada.shape[1]
    tn = 1024
    return pl.pallas_call(
        _ada_kernel,
        grid=(n // tn,),
        in_specs=[pl.BlockSpec((SUBLANES, D_MODEL), lambda j: (0, 0)),
                  pl.BlockSpec((D_MODEL, tn), lambda j: (0, j)),
                  pl.BlockSpec((1, tn), lambda j: (0, j))],
        out_specs=pl.BlockSpec((SUBLANES, tn), lambda j: (0, j)),
        out_shape=jax.ShapeDtypeStruct((SUBLANES, n), F32),
        compiler_params=_params("parallel"),
        name="ada",
    )(cc, w_ada, b_ada)


def _ln_proj_kernel(x_ref, sh_ref, sc_ref, w_ref, wdt_ref, o_ref, dt_ref, h_ref, *, tm):
    rows = 256 if tm % 256 == 0 else tm
    for r0 in range(0, tm, rows):
        h = _ln(x_ref[r0:r0 + rows, :]) * (1.0 + sc_ref[...]) + sh_ref[...]
        h_ref[r0:r0 + rows, :] = h.astype(BF16)
    dt_ref[...] = _dot(h_ref[...], wdt_ref[...])
    for n0 in range(0, N_MAIN, PROJ_TN):
        o_ref[:, n0:n0 + PROJ_TN] = _dot(h_ref[...], w_ref[:, n0:n0 + PROJ_TN]).astype(o_ref.dtype)


def _ln_proj(x, shift, scale, w_main, w_dt, tm):
    L = x.shape[0]
    resident = lambda a: pl.BlockSpec(a.shape, lambda i: (0, 0), pipeline_mode=pl.Buffered(1))
    return pl.pallas_call(
        functools.partial(_ln_proj_kernel, tm=tm),
        grid=(L // tm,),
        in_specs=[pl.BlockSpec((tm, D_MODEL), lambda i: (i, 0)),
                  pl.BlockSpec((1, D_MODEL), lambda i: (0, 0)),
                  pl.BlockSpec((1, D_MODEL), lambda i: (0, 0)),
                  resident(w_main), resident(w_dt)],
        out_specs=[pl.BlockSpec((tm, N_MAIN), lambda i: (i, 0)),
                   pl.BlockSpec((tm, LANES), lambda i: (i, 0))],
        out_shape=[jax.ShapeDtypeStruct((L, N_MAIN), BF16),
                   jax.ShapeDtypeStruct((L, LANES), F32)],
        scratch_shapes=[pltpu.VMEM((tm, D_MODEL), BF16)],
        compiler_params=_params("parallel"),
        name="ln_proj",
    )(x, shift, scale, w_main, w_dt)


CONV_HALO = 64
CONV_ROWS = 128


def _conv_kernel(xp_ref, x_ref, xn_ref, w_ref, b_ref, o_ref, ext_ref, *, tm):
    i = pl.program_id(0)
    last = pl.num_programs(0) - 1
    zero_halo = jnp.zeros_like(xp_ref[...])
    ext_ref[0:CONV_HALO, :] = jnp.where(i > 0, xp_ref[...], zero_halo)
    ext_ref[CONV_HALO:CONV_HALO + tm, :] = x_ref[...]
    ext_ref[CONV_HALO + tm:, :] = jnp.where(i < last, xn_ref[...], zero_halo)

    half = CONV_W // 2
    side_taps = [k for k in range(CONV_W) if k != half]
    win = CONV_ROWS + 2 * CONV_HALO
    rr = lax.broadcasted_iota(I32, (len(side_taps) * CONV_ROWS, win), 0)
    jj = lax.broadcasted_iota(I32, (len(side_taps) * CONV_ROWS, win), 1)
    blk = lax.shift_right_logical(rr, CONV_ROWS.bit_length() - 1)
    tap = jnp.where(blk < half, blk, blk + 1)
    selector = jnp.where(jj == (rr - blk * CONV_ROWS) + (CONV_HALO - half) + tap, 1.0, 0.0).astype(BF16)

    for r0 in range(0, tm, CONV_ROWS):
        shifted = _dot(selector, ext_ref[r0:r0 + win, :])
        centre = ext_ref[CONV_HALO + r0:CONV_HALO + r0 + CONV_ROWS, :].astype(F32)
        acc = b_ref[...] + w_ref[half:half + 1, :] * centre
        for n, k in enumerate(side_taps):
            acc = acc + w_ref[k:k + 1, :] * shifted[n * CONV_ROWS:(n + 1) * CONV_ROWS, :]
        o_ref[r0:r0 + CONV_ROWS, :] = _silu(acc).astype(o_ref.dtype)


def _conv(p_main, conv_w, conv_b, tm, tc=1024):
    L = p_main.shape[0]
    hb = tm // CONV_HALO
    n_halo = L // CONV_HALO
    c0 = OFF_XBC // tc
    return pl.pallas_call(
        functools.partial(_conv_kernel, tm=tm),
        grid=(L // tm, CONV_CH // tc),
        in_specs=[pl.BlockSpec((CONV_HALO, tc), lambda i, j: (jnp.maximum(i * hb - 1, 0), c0 + j)),
                  pl.BlockSpec((tm, tc), lambda i, j: (i, c0 + j)),
                  pl.BlockSpec((CONV_HALO, tc), lambda i, j: (jnp.minimum((i + 1) * hb, n_halo - 1), c0 + j)),
                  pl.BlockSpec((CONV_W, tc), lambda i, j: (0, j)),
                  pl.BlockSpec((1, tc), lambda i, j: (0, j))],
        out_specs=pl.BlockSpec((tm, tc), lambda i, j: (i, j)),
        out_shape=jax.ShapeDtypeStruct((L, CONV_CH), BF16),
        scratch_shapes=[pltpu.VMEM((tm + 2 * CONV_HALO, tc), BF16)],
        compiler_params=_params("parallel", "parallel"),
        name="conv",
    )(p_main, p_main, p_main, conv_w, conv_b)


def _qkv_kernel(*refs, tm, rope, with_q):
    if with_q:
        q_ref, k_ref, v_ref, rowtab_ref, coltab_ref, qo_ref, ko_ref, vo_ref = refs
    else:
        k_ref, v_ref, ko_ref, vo_ref = refs
    lane = lax.broadcasted_iota(I32, (tm, LANES), 1)
    first = (lane & 31) < 16
    lo = lane < 64

    if rope:
        rows_per_tile = tm // GRID_W
        r0 = pl.program_id(0) * rows_per_tile
        cos, sin = [jnp.concatenate([rowtab_ref[f, pl.ds(r0 + b, 1), :] + coltab_ref[f]
                                     for b in range(rows_per_tile)], axis=0) for f in range(2)]

    def rot(x):
        if not rope:
            return x
        r = jnp.where(first, -pltpu.roll(x, LANES - 16, 1), pltpu.roll(x, 16, 1))
        return x * cos + r * sin

    def dup(x, o_ref, p):
        r = pltpu.roll(x, 64, 1)
        o_ref[:, (2 * p) * LANES:(2 * p + 1) * LANES] = jnp.where(lo, x, r).astype(o_ref.dtype)
        o_ref[:, (2 * p + 1) * LANES:(2 * p + 2) * LANES] = jnp.where(lo, r, x).astype(o_ref.dtype)

    if with_q:
        for c in range(ATTN_WIDTH // LANES):
            sl = slice(c * LANES, (c + 1) * LANES)
            qo_ref[:, sl] = (rot(q_ref[:, sl].astype(F32)) * (ATTN_SCALE * LOG2_E)).astype(qo_ref.dtype)
    for p in range(KV_WIDTH // LANES):
        sl = slice(p * LANES, (p + 1) * LANES)
        dup(rot(k_ref[:, sl].astype(F32)), ko_ref, p)
        dup(v_ref[:, sl].astype(F32), vo_ref, p)


def _qkv(p_main, row_tab, col_tab, tm, rope, with_q):
    L = p_main.shape[0]
    kv_specs = [pl.BlockSpec((tm, KV_WIDTH), lambda i: (i, OFF_K // KV_WIDTH)),
                pl.BlockSpec((tm, KV_WIDTH), lambda i: (i, OFF_V // KV_WIDTH))]
    kv_out = [pl.BlockSpec((tm, KV_DUP), lambda i: (i, 0)), pl.BlockSpec((tm, KV_DUP), lambda i: (i, 0))]
    kv_shape = [jax.ShapeDtypeStruct((L, KV_DUP), BF16), jax.ShapeDtypeStruct((L, KV_DUP), BF16)]
    if with_q:
        in_specs = ([pl.BlockSpec((tm, ATTN_WIDTH), lambda i: (i, OFF_Q // ATTN_WIDTH))] + kv_specs
                    + [pl.BlockSpec(row_tab.shape, lambda i: (0, 0, 0)), pl.BlockSpec(col_tab.shape, lambda i: (0, 0, 0))])
        out_specs = [pl.BlockSpec((tm, ATTN_WIDTH), lambda i: (i, 0))] + kv_out
        out_shape = [jax.ShapeDtypeStruct((L, ATTN_WIDTH), BF16)] + kv_shape
        args = (p_main, p_main, p_main, row_tab, col_tab)
    else:
        in_specs, out_specs, out_shape, args = kv_specs, kv_out, kv_shape, (p_main, p_main)
    return pl.pallas_call(
        functools.partial(_qkv_kernel, tm=tm, rope=rope, with_q=with_q),
        grid=(L // tm,),
        in_specs=in_specs, out_specs=out_specs, out_shape=out_shape,
        compiler_params=_params("parallel"),
        name="qkv_rope" if with_q else "kv_ctx",
    )(*args)


PAIRS_PER_GROUP = (SSD_HEADS // SSD_GROUPS) // 2
GROUP_W = (SSD_HEADS // SSD_GROUPS) * SSD_HEAD_DIM


def _ssd_kernel(xf_ref, xb_ref, dtf_ref, dtb_ref, s0_ref, bias_ref, alog_ref, yf_ref, yb_ref, s_ref):
    Q = SSD_CHUNK

    @pl.when(pl.program_id(0) == 0)
    def _():
        s_ref[...] = s0_ref[...]

    row = lax.broadcasted_iota(I32, (Q, Q), 0)
    col = lax.broadcasted_iota(I32, (Q, Q), 1)
    lo = col < 64
    lo_row = lax.broadcasted_iota(I32, (1, Q), 1) < 64
    neg_a = -jnp.exp(alog_ref[...])

    for d, (x_ref, dt_ref, y_ref) in enumerate(((xf_ref, dtf_ref, yf_ref), (xb_ref, dtb_ref, yb_ref))):
        tri = (row >= col) if d == 0 else (row <= col)
        off = d * SSD_HEADS
        z = dt_ref[...] + bias_ref[...]
        dt = jnp.maximum(z, 0.0) + jnp.log1p(jnp.exp(-jnp.abs(z)))
        da = dt * neg_a
        a = _dot(jnp.where(tri, 1.0, 0.0), da, precision=HIGHEST) * LOG2_E
        a_t = a.T
        dt_t = dt.T
        edge = a_t[:, Q - 1:Q] if d == 0 else a_t[:, 0:1]
        w_t = (dt_t * jnp.exp2(edge - a_t)).astype(BF16)
        dec = jnp.exp2(edge)
        dt_tb = dt_t.astype(BF16)

        for g in range(SSD_GROUPS):
            bg = x_ref[:, D_INNER + g * SSD_STATE:D_INNER + (g + 1) * SSD_STATE]
            cg = x_ref[:, D_INNER + (SSD_GROUPS + g) * SSD_STATE:D_INNER + (SSD_GROUPS + g + 1) * SSD_STATE]
            cb = _dot_nt(cg, bg).astype(BF16)
            b_t = bg.astype(F32).T.astype(BF16)
            s_prev = s_ref[d, g]
            y_off = _dot(cg, s_prev.astype(BF16))
            for p in range(PAIRS_PER_GROUP):
                h0 = g * (SSD_HEADS // SSD_GROUPS) + 2 * p
                c0 = h0 * SSD_HEAD_DIM
                xpair = x_ref[:, c0:c0 + LANES]
                zero = jnp.zeros_like(xpair)
                y_diag = None
                st = None
                a_cols = []
                for t, xm in enumerate((jnp.where(lo, xpair, zero), jnp.where(lo, zero, xpair))):
                    h = off + h0 + t
                    a_col = jnp.broadcast_to(a[:, h:h + 1], (Q, Q))
                    a_cols.append(a_col)
                    seg = a_col - a_t[h:h + 1, :]
                    decay = jnp.exp2(jnp.where(tri, seg, -jnp.inf)).astype(BF16)
                    yd = _dot(decay * cb * dt_tb[h:h + 1, :], xm)
                    sd = _dot(b_t * w_t[h:h + 1, :], xm)
                    y_diag = yd if y_diag is None else y_diag + yd
                    st = sd if st is None else st + sd
                h = off + h0
                ea_pair = jnp.exp2(jnp.where(lo, a_cols[0], a_cols[1]))
                ps = slice(p * LANES, (p + 1) * LANES)
                y_ref[:, c0:c0 + LANES] = (y_diag + y_off[:, ps] * ea_pair).astype(y_ref.dtype)
                dec_pair = jnp.where(lo_row, dec[h:h + 1, :], dec[h + 1:h + 2, :])
                s_ref[d, g, :, ps] = s_prev[:, ps] * dec_pair + st


def _ssd(xc, dt, s0, bias, alog):
    L = xc.shape[0]
    nc = L // SSD_CHUNK
    sshape = (2, SSD_GROUPS, SSD_STATE, GROUP_W)
    return pl.pallas_call(
        _ssd_kernel,
        grid=(nc,),
        in_specs=[pl.BlockSpec((SSD_CHUNK, CONV_CH), lambda c: (c, 0)),
                  pl.BlockSpec((SSD_CHUNK, CONV_CH), lambda c: (nc - 1 - c, 0)),
                  pl.BlockSpec((SSD_CHUNK, LANES), lambda c: (c, 0)),
                  pl.BlockSpec((SSD_CHUNK, LANES), lambda c: (nc - 1 - c, 0)),
                  pl.BlockSpec(sshape, lambda c: (0, 0, 0, 0)),
                  pl.BlockSpec((1, LANES), lambda c: (0, 0)),
                  pl.BlockSpec((1, LANES), lambda c: (0, 0))],
        out_specs=[pl.BlockSpec((SSD_CHUNK, D_INNER), lambda c: (c, 0)),
                   pl.BlockSpec((SSD_CHUNK, D_INNER), lambda c: (nc - 1 - c, 0)),
                   pl.BlockSpec(sshape, lambda c: (0, 0, 0, 0))],
        out_shape=[jax.ShapeDtypeStruct((L, D_INNER), BF16),
                   jax.ShapeDtypeStruct((L, D_INNER), BF16),
                   jax.ShapeDtypeStruct(sshape, F32)],
        compiler_params=_params("arbitrary"),
        name="ssd",
    )(xc, xc, dt, dt, s0, bias, alog)


ATTN_BLOCKS_PER_STEP = 2


def _attn_kernel(sink_ref, q_ref, kp_ref, kc_ref, kn_ref, vp_ref, vc_ref, vn_ref, kx_ref, vx_ref, o_ref, *, s_ctx):
    i = pl.program_id(0)
    last = pl.num_programs(0) - 1
    B = ATTN_BLOCK
    nsub = ATTN_BLOCKS_PER_STEP
    span = 3 * B
    width = span + s_ctx
    t = lax.broadcasted_iota(I32, (B, width), 0)
    j = lax.broadcasted_iota(I32, (B, width), 1)
    in_window = jnp.abs(j - B - t) <= WINDOW
    lo = lax.broadcasted_iota(I32, (B, LANES), 1) < 64
    group = ATTN_Q_HEADS // ATTN_KV_HEADS

    for sb in range(nsub):
        rows = slice(sb * B, (sb + 1) * B)
        j_min = jnp.where(i > 0, 0, B) if sb == 0 else 0
        j_max = jnp.where(i < last, span, 2 * B) if sb == nsub - 1 else span
        valid = (j >= span) | (in_window & (j >= j_min) & (j < j_max))

        def window(prev_ref, cur_ref, next_ref, ctx_ref, sl):
            before = prev_ref[:, sl] if sb == 0 else cur_ref[(sb - 1) * B:sb * B, sl]
            after = next_ref[:, sl] if sb == nsub - 1 else cur_ref[(sb + 1) * B:(sb + 2) * B, sl]
            return jnp.concatenate([before, cur_ref[rows, sl], after, ctx_ref[:, sl]], axis=0)

        for kh in range(ATTN_KV_HEADS):
            sl = slice(kh * LANES, (kh + 1) * LANES)
            k_all = window(kp_ref, kc_ref, kn_ref, kx_ref, sl)
            v_all = window(vp_ref, vc_ref, vn_ref, vx_ref, sl)
            q_rows = []
            for u in range(group // 2):
                c = (group // 2) * kh + u
                qpair = q_ref[rows, c * LANES:(c + 1) * LANES]
                zero_q = jnp.zeros_like(qpair)
                q_rows += [jnp.where(lo, qpair, zero_q), jnp.where(lo, zero_q, qpair)]
            s_all = _dot_nt(jnp.concatenate(q_rows, axis=0), k_all)
            e_rows, inv = [], []
            for b in range(group):
                sink = sink_ref[group * kh + b] * LOG2_E
                s = jnp.where(valid, s_all[b * B:(b + 1) * B, :], -jnp.inf)
                m = jnp.maximum(jnp.max(s, axis=1, keepdims=True), sink)
                e = jnp.exp2(s - m)
                inv.append(1.0 / (jnp.sum(e, axis=1, keepdims=True) + jnp.exp2(sink - m)))
                e_rows.append(e.astype(BF16))
            o_all = _dot(jnp.concatenate(e_rows, axis=0), v_all)
            for u in range(group // 2):
                c = (group // 2) * kh + u
                b0 = 2 * u
                o = jnp.where(lo, o_all[b0 * B:(b0 + 1) * B, :] * inv[b0],
                              o_all[(b0 + 1) * B:(b0 + 2) * B, :] * inv[b0 + 1])
                o_ref[rows, c * LANES:(c + 1) * LANES] = o.astype(o_ref.dtype)


def _attn(sink, q, kd, vd, kx, vx):
    L = q.shape[0]
    s_ctx = kx.shape[0]
    nsub = ATTN_BLOCKS_PER_STEP
    nb = L // ATTN_BLOCK
    prev = lambda i: (jnp.maximum(nsub * i - 1, 0), 0)
    cur = lambda i: (i, 0)
    nxt = lambda i: (jnp.minimum(nsub * (i + 1), nb - 1), 0)
    edge = lambda f: pl.BlockSpec((ATTN_BLOCK, KV_DUP), f)
    body = pl.BlockSpec((nsub * ATTN_BLOCK, KV_DUP), cur)
    return pl.pallas_call(
        functools.partial(_attn_kernel, s_ctx=s_ctx),
        grid=(nb // nsub,),
        in_specs=[pl.BlockSpec(memory_space=pltpu.SMEM),
                  pl.BlockSpec((nsub * ATTN_BLOCK, ATTN_WIDTH), cur),
                  edge(prev), body, edge(nxt), edge(prev), body, edge(nxt),
                  pl.BlockSpec((s_ctx, KV_DUP), lambda i: (0, 0)),
                  pl.BlockSpec((s_ctx, KV_DUP), lambda i: (0, 0))],
        out_specs=pl.BlockSpec((nsub * ATTN_BLOCK, ATTN_WIDTH), cur),
        out_shape=jax.ShapeDtypeStruct((L, ATTN_WIDTH), BF16),
        compiler_params=_params("parallel"),
        name="attn",
    )(sink, q, kd, kd, kd, vd, vd, vd, kx, vx)


def _out_kernel(yf_ref, yb_ref, xs_ref, z_ref, g_ref, ya_ref, x_ref, dsk_ref, nw_ref, wssd_ref, wattn_ref, wo_ref,
                bo_ref, g1_ref, ln1g_ref, ln1b_ref, sh2_ref, sc2_ref, wrh_ref, wrl_ref, x1_ref, h2_ref, aff_ref):
    y = yf_ref[...].astype(F32) + yb_ref[...].astype(F32) + dsk_ref[...] * xs_ref[...].astype(F32)
    y = y * _silu(z_ref[...].astype(F32))
    y = y * lax.rsqrt(jnp.mean(y * y, -1, keepdims=True) + LN_EPS) * nw_ref[...]
    br_ssd = _dot(y.astype(BF16), wssd_ref[...])
    br_attn = _dot(ya_ref[...], wattn_ref[...])
    g_ssd = _sigmoid(g_ref[:, :D_MODEL].astype(F32))
    g_attn = _sigmoid(g_ref[:, D_MODEL:].astype(F32))
    m = g_ssd * br_ssd + g_attn * br_attn
    o = _dot(m.astype(BF16), wo_ref[...]) + bo_ref[...]
    x1 = _ln(DEEPNORM_ALPHA * x_ref[...] + g1_ref[...] * o) * ln1g_ref[...] + ln1b_ref[...]
    x1_ref[...] = x1
    h2 = _ln(x1) * (1.0 + sc2_ref[...]) + sh2_ref[...]
    h2_ref[...] = h2.astype(h2_ref.dtype)
    h_hi = h2.astype(BF16)
    h_lo = (h2 - h_hi.astype(F32)).astype(BF16)
    logits = _dot_nt(h_hi, wrh_ref[...]) + (_dot_nt(h_hi, wrl_ref[...]) + _dot_nt(h_lo, wrh_ref[...]))
    lane = lax.broadcasted_iota(I32, logits.shape, 1)
    logits = jnp.where(lane < N_EXPERTS, logits, -jnp.inf)
    e = jnp.exp(logits - jnp.max(logits, axis=1, keepdims=True))
    aff = e / jnp.sum(e, axis=1, keepdims=True)
    aff_ref[...] = aff.T[:N_EXPERTS, :]


def _out(yf, yb, xc, p_main, ya, x, dsk, nw, wssd, wattn, wo, bo, g1, ln1g, ln1b, sh2, sc2, wrh, wrl, tm):
    L = x.shape[0]
    row = lambda w: pl.BlockSpec((1, w), lambda i: (0, 0))
    full = lambda a: pl.BlockSpec(a.shape, lambda i: (0, 0))
    return pl.pallas_call(
        _out_kernel,
        grid=(L // tm,),
        in_specs=[pl.BlockSpec((tm, D_INNER), lambda i: (i, 0)),
                  pl.BlockSpec((tm, D_INNER), lambda i: (i, 0)),
                  pl.BlockSpec((tm, D_INNER), lambda i: (i, 0)),
                  pl.BlockSpec((tm, D_INNER), lambda i: (i, OFF_Z // D_INNER)),
                  pl.BlockSpec((tm, 2 * D_MODEL), lambda i: (i, OFF_G // (2 * D_MODEL))),
                  pl.BlockSpec((tm, ATTN_WIDTH), lambda i: (i, 0)),
                  pl.BlockSpec((tm, D_MODEL), lambda i: (i, 0)),
                  row(D_INNER), row(D_INNER), full(wssd), full(wattn), full(wo),
                  row(D_MODEL), row(D_MODEL), row(D_MODEL), row(D_MODEL), row(D_MODEL), row(D_MODEL),
                  full(wrh), full(wrl)],
        out_specs=[pl.BlockSpec((tm, D_MODEL), lambda i: (i, 0)),
                   pl.BlockSpec((tm, D_MODEL), lambda i: (i, 0)),
                   pl.BlockSpec((N_EXPERTS, tm), lambda i: (0, i))],
        out_shape=[jax.ShapeDtypeStruct((L, D_MODEL), F32),
                   jax.ShapeDtypeStruct((L, D_MODEL), BF16),
                   jax.ShapeDtypeStruct((N_EXPERTS, L), F32)],
        compiler_params=_params("parallel"),
        name="out_proj",
    )(yf, yb, xc, p_main, p_main, ya, x, dsk, nw, wssd, wattn, wo, bo, g1, ln1g, ln1b, sh2, sc2, wrh, wrl)


def _count(mask):
    c = jnp.sum(jnp.where(mask, 1.0, 0.0), axis=1, keepdims=True)
    return jnp.sum(c, axis=0, keepdims=True)


def _route_kernel(aff_ref, rank_ref, start_ref, cnt_ref, sel_ref, *, cap, nrows):
    E = N_EXPERTS
    R = nrows
    tok = (lax.broadcasted_iota(I32, (R, LANES), 0) * LANES + lax.broadcasted_iota(I32, (R, LANES), 1))
    capf = jnp.float32(cap)

    def as_f32(word):
        return pltpu.bitcast(word, F32)

    def thr_body(it, ts):
        bit = lax.shift_left(jnp.int32(1), 30 - it)
        out = []
        for e in range(E):
            cand = ts[e] | bit
            out.append(jnp.where(_count(aff_ref[e] >= as_f32(cand)) >= capf, cand, ts[e]))
        return tuple(out)

    ts = lax.fori_loop(0, 31, thr_body, tuple(jnp.zeros((1, 1), I32) for _ in range(E)))
    thr = [as_f32(t) for t in ts]
    need = [capf - _count(aff_ref[e] > thr[e]) for e in range(E)]

    nbits = max((R * LANES - 1).bit_length(), 1)

    def tie_body(it, js):
        bit = lax.shift_left(jnp.int32(1), nbits - 1 - it)
        out = []
        for e in range(E):
            cand = js[e] | bit
            f = _count((aff_ref[e] == thr[e]) & (tok < cand))
            out.append(jnp.where(f < need[e], cand, js[e]))
        return tuple(out)

    js = lax.fori_loop(0, nbits, tie_body, tuple(jnp.zeros((1, 1), I32) for _ in range(E)))
    for e in range(E):
        a = aff_ref[e]
        sel = (a > thr[e]) | ((a == thr[e]) & (tok <= js[e]))
        sel_ref[e] = jnp.where(sel, 1.0, 0.0)

    li = lax.broadcasted_iota(I32, (LANES, LANES), 0)
    lj = lax.broadcasted_iota(I32, (LANES, LANES), 1)
    upper_incl = jnp.where(li <= lj, 1.0, 0.0).astype(BF16)
    ri = lax.broadcasted_iota(I32, (R, R), 0)
    rj = lax.broadcasted_iota(I32, (R, R), 1)
    row_lower = jnp.where(ri >= rj, 1.0, 0.0).astype(BF16)
    lane_id = lax.broadcasted_iota(I32, (R, LANES), 1)

    def expert_body(e, carry):
        start_acc, cnt_acc = carry
        sel = sel_ref[e]
        w = _dot(sel.astype(BF16), upper_incl)
        tot_b = jnp.broadcast_to(jnp.sum(sel, axis=1, keepdims=True), (R, LANES))
        seg_b = jnp.floor((tot_b + (SUBLANES - 1)) * (1.0 / SUBLANES)) * SUBLANES
        cum_incl = _dot(row_lower, seg_b.astype(BF16))
        rank_ref[e] = jnp.where(sel > 0.0, w - 1.0, -1.0)
        start_acc = jnp.where(lane_id == e, cum_incl - seg_b, start_acc)
        cnt_acc = jnp.where(lane_id == e, tot_b, cnt_acc)
        return start_acc, cnt_acc

    zero = jnp.zeros((R, LANES), F32)
    start_acc, cnt_acc = lax.fori_loop(0, E, expert_body, (zero, zero))
    start_ref[...] = start_acc.astype(I32)
    cnt_ref[...] = cnt_acc.astype(I32)


def _route(aff3, cap):
    E, R, _ = aff3.shape
    return pl.pallas_call(
        functools.partial(_route_kernel, cap=cap, nrows=R),
        out_shape=[jax.ShapeDtypeStruct((E, R, LANES), F32),
                   jax.ShapeDtypeStruct((R, LANES), I32),
                   jax.ShapeDtypeStruct((R, LANES), I32)],
        scratch_shapes=[pltpu.VMEM((E, R, LANES), F32)],
        compiler_params=pltpu.CompilerParams(vmem_limit_bytes=VMEM_LIMIT_BYTES),
        name="route",
    )(aff3)


SLOTS = 32
WIN = N_EXPERTS * SLOTS


def _slot_onehot(tgt, values, k):
    sub = lax.broadcasted_iota(I32, (SLOTS, tgt.shape[1]), 0).astype(F32) + (SLOTS * k).astype(F32)
    rows = []
    for e in range(N_EXPERTS):
        hit = jnp.broadcast_to(tgt[e:e + 1, :], sub.shape) == sub
        rows.append(jnp.where(hit, jnp.broadcast_to(values[e:e + 1, :], sub.shape), 0.0))
    return jnp.concatenate(rows, axis=0).astype(BF16)


def _rounds(cnt_ref, r):
    n = cnt_ref[0, r]
    for e in range(1, N_EXPERTS):
        n = jnp.maximum(n, cnt_ref[e, r])
    return lax.shift_right_logical(n + (SLOTS - 1), SLOTS.bit_length() - 1)


def _dispatch_kernel(start_ref, cnt_ref, h_ref, rank_ref, xs_hbm, stage_ref, ovf_ref, sem, osem):
    r = pl.program_id(0)
    last = pl.num_programs(0) - 1
    slot = lax.rem(r, 2)
    rank = rank_ref[:, 0, 0, :]
    ones = jnp.ones_like(rank)

    def compact(k):
        return _dot(_slot_onehot(rank, ones, k), h_ref[...])

    def window_copy(src_ref, e, k, s):
        return pltpu.make_async_copy(src_ref.at[pl.ds(e * SLOTS, SLOTS), :],
                                     xs_hbm.at[e, pl.ds(pl.multiple_of(start_ref[e, r] + SLOTS * k, SUBLANES), SLOTS), :], s)

    def stage_wait(sl):
        pltpu.make_async_copy(stage_ref.at[sl], stage_ref.at[sl], sem.at[sl]).wait()

    stage_ref[slot] = compact(jnp.int32(0))

    @pl.when(r > 0)
    def _():
        stage_wait(1 - slot)

    for e in range(N_EXPERTS):
        window_copy(stage_ref.at[slot], e, 0, sem.at[slot]).start()

    def extra(k, _):
        ovf_ref[...] = compact(k)
        for e in range(N_EXPERTS):
            @pl.when(cnt_ref[e, r] > SLOTS * k)
            def _():
                window_copy(ovf_ref, e, k, osem).start()
        for e in range(N_EXPERTS):
            @pl.when(cnt_ref[e, r] > SLOTS * k)
            def _():
                window_copy(ovf_ref, e, k, osem).wait()
        return 0

    lax.fori_loop(1, _rounds(cnt_ref, r), extra, 0)

    def tail_copies(act):
        n_rows = xs_hbm.shape[1]
        for e in range(N_EXPERTS):
            n_win = jnp.maximum(lax.shift_right_logical(cnt_ref[e, r] + (SLOTS - 1), SLOTS.bit_length() - 1), 1)
            off0 = start_ref[e, r] + SLOTS * n_win
            length = n_rows - off0
            n_full = lax.div(length, WIN)

            def full(i, _, e=e, off0=off0):
                act(pltpu.make_async_copy(ovf_ref, xs_hbm.at[e, pl.ds(pl.multiple_of(off0 + i * WIN, SUBLANES), WIN), :],
                                          osem))
                return 0

            lax.fori_loop(0, n_full, full, 0)
            off = off0 + n_full * WIN
            rem = length - n_full * WIN
            size = WIN // 2
            while size >= SUBLANES:
                @pl.when((rem & size) != 0)
                def _(e=e, off=off, size=size):
                    act(pltpu.make_async_copy(ovf_ref.at[pl.ds(0, size), :],
                                              xs_hbm.at[e, pl.ds(pl.multiple_of(off, SUBLANES), size), :], osem))
                off = off + (rem & size)
                size //= 2

    @pl.when(r == last)
    def _():
        stage_wait(slot)
        ovf_ref[...] = jnp.zeros(ovf_ref.shape, F32)
        tail_copies(lambda cp: cp.start())
        tail_copies(lambda cp: cp.wait())


def _padded_cap(cap, n_tiles, ts):
    worst = cap + (SUBLANES - 1) * n_tiles
    return -(-worst // ts) * ts


def _dispatch(start, cnt, h2, rank4, cap):
    L = h2.shape[0]
    T = LANES
    return pl.pallas_call(
        _dispatch_kernel,
        grid_spec=pltpu.PrefetchScalarGridSpec(
            num_scalar_prefetch=2,
            grid=(L // T,),
            in_specs=[pl.BlockSpec((T, D_MODEL), lambda r, *_: (r, 0)),
                      pl.BlockSpec((N_EXPERTS, 1, 1, T), lambda r, *_: (0, r, 0, 0))],
            out_specs=pl.BlockSpec(memory_space=pl.ANY),
            scratch_shapes=[pltpu.VMEM((2, WIN, D_MODEL), F32),
                            pltpu.VMEM((WIN, D_MODEL), F32),
                            pltpu.SemaphoreType.DMA((2,)),
                            pltpu.SemaphoreType.DMA(())]),
        out_shape=jax.ShapeDtypeStruct((N_EXPERTS, cap + SLOTS, D_MODEL), F32),
        compiler_params=_params("arbitrary"),
        name="dispatch",
    )(start, cnt, h2, rank4)


def _expert_kernel(start_ref, cnt_ref, xs_ref, wgu_ref, wd_ref, o_ref, wgu_bf, wd_bf, *, ts):
    e = pl.program_id(0)
    j = pl.program_id(1)
    last_tile = start_ref.shape[1] - 1
    seg = lax.shift_left(lax.shift_right_logical(cnt_ref[e, last_tile] + (SUBLANES - 1), 3), 3)
    total = start_ref[e, last_tile] + seg
    live = total - j * ts

    @pl.when(j == 0)
    def _():
        rows = 128
        for r0 in range(0, D_MODEL, rows):
            wgu_bf[r0:r0 + rows, :] = wgu_ref[0, r0:r0 + rows, :].astype(BF16)
        for r0 in range(0, EXPERT_FF, rows):
            wd_bf[r0:r0 + rows, :] = wd_ref[0, r0:r0 + rows, :].astype(BF16)

    @pl.when(live > 0)
    def _():
        row = lax.broadcasted_iota(I32, (ts, D_MODEL), 0)
        xs = jnp.where(row < live, xs_ref[0], 0.0).astype(BF16)
        gu = _dot(xs, wgu_bf[...])
        act = _silu(gu[:, :EXPERT_FF]) * gu[:, EXPERT_FF:]
        o_ref[0] = _dot(act.astype(BF16), wd_bf[...])

    @pl.when(live <= 0)
    def _():
        o_ref[0] = jnp.zeros((ts, D_MODEL), F32)


def _expert(start, cnt, xs, wgu, wd, capp, ts):
    E = xs.shape[0]
    return pl.pallas_call(
        functools.partial(_expert_kernel, ts=ts),
        grid_spec=pltpu.PrefetchScalarGridSpec(
            num_scalar_prefetch=2,
            grid=(E, capp // ts),
            in_specs=[pl.BlockSpec((1, ts, D_MODEL), lambda e, j, *_: (e, j, 0)),
                      pl.BlockSpec((1, D_MODEL, 2 * EXPERT_FF), lambda e, j, *_: (e, 0, 0)),
                      pl.BlockSpec((1, EXPERT_FF, D_MODEL), lambda e, j, *_: (e, 0, 0))],
            out_specs=pl.BlockSpec((1, ts, D_MODEL), lambda e, j, *_: (e, j, 0)),
            scratch_shapes=[pltpu.VMEM((D_MODEL, 2 * EXPERT_FF), BF16),
                            pltpu.VMEM((EXPERT_FF, D_MODEL), BF16)]),
        out_shape=jax.ShapeDtypeStruct((E, capp, D_MODEL), F32),
        compiler_params=_params("parallel", "arbitrary"),
        name="expert_ffn",
    )(start, cnt, xs, wgu, wd)


def _combine_kernel(start_ref, cnt_ref, ye_hbm, x1_ref, aff_ref, rank_ref, g2_ref, lng_ref, lnb_ref,
                    o_ref, buf_ref, sem, *, cap):
    r = pl.program_id(0)
    last = pl.num_programs(0) - 1
    slot = lax.rem(r, 2)

    def win_start(e, rr, k):
        return pl.multiple_of(jnp.minimum(start_ref[e, rr] + SLOTS * k, cap - SLOTS), SUBLANES)

    def fetch(rr, k, sl):
        for e in range(N_EXPERTS):
            pltpu.make_async_copy(ye_hbm.at[e, pl.ds(win_start(e, rr, k), SLOTS), :],
                                  buf_ref.at[sl, pl.ds(e * SLOTS, SLOTS), :], sem.at[sl]).start()

    def fetch_wait(sl):
        pltpu.make_async_copy(buf_ref.at[sl], buf_ref.at[sl], sem.at[sl]).wait()

    @pl.when(r == 0)
    def _():
        fetch(r, 0, slot)

    @pl.when(r < last)
    def _():
        fetch(r + 1, 0, 1 - slot)

    rank = rank_ref[:, 0, 0, :]
    aff = aff_ref[...]
    g_hi = aff.astype(BF16).astype(F32)
    g_lo = aff - g_hi
    e_id = lax.broadcasted_iota(I32, rank.shape, 0)

    def contribution(k, sl):
        shift = jnp.zeros_like(rank)
        for e in range(N_EXPERTS):
            d = start_ref[e, r] + SLOTS * k - win_start(e, r, k)
            shift = jnp.where(e_id == e, d.astype(F32), shift)
        base = (SLOTS * k).astype(F32)
        in_round = (rank >= base) & (rank < base + SLOTS)
        tgt = jnp.where(in_round, rank + shift, -1.0)
        yb = buf_ref[sl].astype(BF16)
        tn = (((0,), (0,)), ((), ()))
        return (lax.dot_general(_slot_onehot(tgt, g_hi, k), yb, tn, preferred_element_type=F32)
                + lax.dot_general(_slot_onehot(tgt, g_lo, k), yb, tn, preferred_element_type=F32))

    fetch_wait(slot)
    moe = contribution(jnp.int32(0), slot)

    def extra(k, acc):
        fetch(r, k, slot)
        fetch_wait(slot)
        return acc + contribution(k, slot)

    moe = lax.fori_loop(1, _rounds(cnt_ref, r), extra, moe)
    y = _ln(DEEPNORM_ALPHA * x1_ref[...] + g2_ref[...] * moe)
    o_ref[...] = y * lng_ref[...] + lnb_ref[...]


def _combine(start, cnt, ye, x1, aff_t, rank4, g2, lng, lnb):
    L = x1.shape[0]
    E, cap, _ = ye.shape
    T = LANES
    row = pl.BlockSpec((1, D_MODEL), lambda r, *_: (0, 0))
    return pl.pallas_call(
        functools.partial(_combine_kernel, cap=cap),
        grid_spec=pltpu.PrefetchScalarGridSpec(
            num_scalar_prefetch=2,
            grid=(L // T,),
            in_specs=[pl.BlockSpec(memory_space=pl.ANY),
                      pl.BlockSpec((T, D_MODEL), lambda r, *_: (r, 0)),
                      pl.BlockSpec((E, T), lambda r, *_: (0, r)),
                      pl.BlockSpec((E, 1, 1, T), lambda r, *_: (0, r, 0, 0)),
                      row, row, row],
            out_specs=pl.BlockSpec((T, D_MODEL), lambda r, *_: (r, 0)),
            scratch_shapes=[pltpu.VMEM((2, WIN, D_MODEL), F32),
                            pltpu.SemaphoreType.DMA((2,))]),
        out_shape=jax.ShapeDtypeStruct((L, D_MODEL), F32),
        compiler_params=_params("arbitrary"),
        name="combine",
    )(start, cnt, ye, x1, aff_t, rank4, g2, lng, lnb)


def _rope_tables(L):
    inv_freq = ROPE_BASE ** (-jnp.arange(0, ROPE_AXIS_DIM, 2, dtype=F32) / ROPE_AXIS_DIM)

    def table(n, on_row_lanes):
        ang = jnp.arange(n, dtype=F32)[:, None] * inv_freq[None, :]
        one, zero = jnp.ones_like(ang), jnp.zeros_like(ang)
        mask = jnp.concatenate(([one, one, zero, zero] if on_row_lanes else [zero, zero, one, one]) * 2, -1)
        ang = jnp.concatenate([ang] * 8, -1)
        return jnp.stack([jnp.cos(ang) * mask, jnp.sin(ang) * mask])

    return table(L // GRID_W, True), table(GRID_W, False)


def _row_tile(L, pref):
    return pref if L % pref == 0 else L


def kernel(x, c, ctx, c_ctx, w_ada, b_ada, w_in, conv_w, conv_b, a_log, dt_bias, d_skip, ssd_norm_w, attn_sink,
           w_ssd_br, w_attn_br, w_o, b_o, ln1_g, ln1_b, w_router, w_gate_up, w_down, ln2_g, ln2_b):
    assert x.shape[0] == 1 and w_in.shape[0] == 1, "single batch element, depth 1"
    L = x.shape[1]
    S = ctx.shape[1]
    assert L % (LANES * SUBLANES) == 0 and S % SSD_CHUNK == 0
    cap = CAPACITY_FACTOR * L // N_EXPERTS
    x2, ctx2 = x[0], ctx[0]

    cc = jnp.zeros((SUBLANES, D_MODEL), F32).at[0].set(c[0]).at[1].set(c_ctx)
    mod = _ada(cc, w_ada[0], b_ada[0][None, :])
    sh1, sc1, g1, sh2, sc2, g2 = [mod[0:1, k * D_MODEL:(k + 1) * D_MODEL] for k in range(6)]
    csh1, csc1 = mod[1:2, 0:D_MODEL], mod[1:2, D_MODEL:2 * D_MODEL]

    w = w_in[0]
    o = 0
    parts = {}
    for name, width in (("z", D_INNER), ("xbc", CONV_CH), ("dt", 2 * SSD_HEADS), ("q", ATTN_WIDTH),
                        ("k", KV_WIDTH), ("v", KV_WIDTH), ("g", 2 * D_MODEL)):
        parts[name] = w[:, o:o + width]
        o += width
    w_main = jnp.concatenate([parts[n] for n in ("z", "g", "xbc", "q", "k", "v")], axis=1).astype(BF16)
    w_dt = jnp.pad(parts["dt"], ((0, 0), (0, LANES - 2 * SSD_HEADS))).astype(BF16)

    pad_heads = lambda v: jnp.pad(v.reshape(1, 2 * SSD_HEADS), ((0, 0), (0, LANES - 2 * SSD_HEADS)))
    bias_row, alog_row = pad_heads(dt_bias[0]), pad_heads(a_log[0])
    conv_b2 = conv_b[0][None, :]

    tm_c = _row_tile(S, 256)
    pc, dtc = _ln_proj(ctx2, csh1, csc1, w_main, w_dt, tm_c)
    xcc = _conv(pc, conv_w[0], conv_b2, tm_c)
    kx, vx = _qkv(pc, None, None, tm_c, rope=False, with_q=False)
    s_zero = jnp.zeros((2, SSD_GROUPS, SSD_STATE, GROUP_W), F32)
    _, _, s_ctx = _ssd(xcc, dtc, s_zero, bias_row, alog_row)

    p_main, dt = _ln_proj(x2, sh1, sc1, w_main, w_dt, _row_tile(L, 512))
    xc = _conv(p_main, conv_w[0], conv_b2, _row_tile(L, 512))
    row_tab, col_tab = _rope_tables(L)
    q, kd, vd = _qkv(p_main, row_tab, col_tab, _row_tile(L, 512), rope=True, with_q=True)
    yf, yb, _ = _ssd(xc, dt, s_ctx, bias_row, alog_row)
    ya = _attn(attn_sink[0], q, kd, vd, kx, vx)

    dsk = jnp.repeat(d_skip[0], SSD_HEAD_DIM)[None, :]
    wr = jnp.pad(w_router[0].T, ((0, LANES - N_EXPERTS), (0, 0)))
    wr_hi = wr.astype(BF16)
    wr_lo = (wr - wr_hi.astype(F32)).astype(BF16)
    x1, h2, aff_t = _out(yf, yb, xc, p_main, ya, x2, dsk, ssd_norm_w[0][None, :],
                         w_ssd_br[0].astype(BF16), w_attn_br[0].astype(BF16), w_o[0].astype(BF16),
                         b_o[0][None, :], g1, ln1_g[0][None, :], ln1_b[0][None, :], sh2, sc2, wr_hi, wr_lo,
                         _row_tile(L, 256))

    R = L // LANES
    rank, start_t, cnt_t = _route(aff_t.reshape(N_EXPERTS, R, LANES), cap)
    rank4 = rank.reshape(N_EXPERTS, R, 1, LANES)
    start = start_t[:, :N_EXPERTS].T
    cnt = cnt_t[:, :N_EXPERTS].T
    ts = min(cap, 512)
    capp = _padded_cap(cap, R, ts)
    xs = _dispatch(start, cnt, h2, rank4, capp)
    ye = _expert(start, cnt, xs, w_gate_up[0], w_down[0], capp, ts)
    out = _combine(start, cnt, ye, x1, aff_t, rank4, g2, ln2_g[0][None, :], ln2_b[0][None, :])
    return out[None]
```

```python
import functools

import numpy as np
import jax
import jax.numpy as jnp
from jax import lax
from jax.experimental import pallas as pl
from jax.experimental.pallas import tpu as pltpu

F32 = jnp.float32
BF16 = jnp.bfloat16
I32 = jnp.int32
HIGHEST = lax.Precision.HIGHEST
LOG2_E = 1.4426950408889634

LANES = 128
SUBLANES = 8
VMEM_LIMIT_BYTES = 56 * 1024 * 1024

D_MODEL = 1024
GRID_W = 64
LN_EPS = 1e-5
DEPTH = 1
DEEPNORM_ALPHA = (2.0 * DEPTH) ** 0.25
D_INNER = 2048
SSD_HEAD_DIM = 64
SSD_HEADS = 32
SSD_GROUPS = 4
SSD_STATE = 128
SSD_CHUNK = 128
CONV_W = 5
CONV_CH = D_INNER + 2 * SSD_GROUPS * SSD_STATE
ATTN_HEAD_DIM = 64
ATTN_Q_HEADS = 16
ATTN_KV_HEADS = 4
ATTN_WIDTH = 1024
KV_WIDTH = 256
WINDOW = 128
ATTN_BLOCK = 128
ATTN_SCALE = ATTN_HEAD_DIM ** -0.5
ROPE_BASE = 10000.0
ROPE_AXIS_DIM = ATTN_HEAD_DIM // 2
N_EXPERTS = 16
EXPERT_FF = 1024
CAPACITY_FACTOR = 2

OFF_Z = 0
OFF_G = OFF_Z + D_INNER
OFF_XBC = OFF_G + 2 * D_MODEL
OFF_Q = OFF_XBC + CONV_CH
OFF_K = OFF_Q + ATTN_WIDTH
OFF_V = OFF_K + KV_WIDTH
N_MAIN = OFF_V + KV_WIDTH
PROJ_TN = N_MAIN // 4
KV_DUP = 2 * KV_WIDTH


def _params(*sem):
    return pltpu.CompilerParams(dimension_semantics=sem, vmem_limit_bytes=VMEM_LIMIT_BYTES)


def _sigmoid(x):
    return 1.0 / (1.0 + jnp.exp2(x * (-LOG2_E)))


def _silu(x):
    return x * _sigmoid(x)


def _dot(a, b, **kw):
    return jnp.dot(a, b, preferred_element_type=F32, **kw)


def _dot_nt(a, b, **kw):
    return lax.dot_general(a, b, (((1,), (1,)), ((), ())), preferred_element_type=F32, **kw)


def _ln(x):
    mu = jnp.mean(x, -1, keepdims=True)
    xc = x - mu
    var = jnp.mean(xc * xc, -1, keepdims=True)
    return xc * lax.rsqrt(var + LN_EPS)


def _ada_kernel(s_ref, w_ref, b_ref, o_ref):
    s = _silu(s_ref[...])
    o_ref[...] = _dot(s, w_ref[...], precision=HIGHEST) + b_ref[...]


def _ada(cc, w_ada, b_ada):
    n = w_ada.shape[1]
    tn = 1024
    return pl.pallas_call(
        _ada_kernel,
        grid=(n // tn,),
        in_specs=[pl.BlockSpec((SUBLANES, D_MODEL), lambda j: (0, 0)),
                  pl.BlockSpec((D_MODEL, tn), lambda j: (0, j)),
                  pl.BlockSpec((1, tn), lambda j: (0, j))],
        out_specs=pl.BlockSpec((SUBLANES, tn), lambda j: (0, j)),
        out_shape=jax.ShapeDtypeStruct((SUBLANES, n), F32),
        compiler_params=_params("parallel"),
        name="ada",
    )(cc, w_ada, b_ada)


def _ln_proj_kernel(x_ref, sh_ref, sc_ref, w_ref, wdt_ref, o_ref, dt_ref, h_ref, *, tm):
    rows = 256 if tm % 256 == 0 else tm
    for r0 in range(0, tm, rows):
        rs = slice(r0, r0 + rows)
        h = _ln(x_ref[rs, :]) * (1.0 + sc_ref[...]) + sh_ref[...]
        h_ref[rs, :] = h.astype(BF16)
        dt_ref[rs, :] = _dot(h_ref[rs, :], wdt_ref[...])
        for n0 in range(0, N_MAIN, PROJ_TN):
            o_ref[rs, n0:n0 + PROJ_TN] = _dot(h_ref[rs, :], w_ref[:, n0:n0 + PROJ_TN]).astype(o_ref.dtype)


def _ln_proj(x, shift, scale, w_main, w_dt, tm):
    L = x.shape[0]
    resident = lambda a: pl.BlockSpec(a.shape, lambda i: (0, 0), pipeline_mode=pl.Buffered(1))
    return pl.pallas_call(
        functools.partial(_ln_proj_kernel, tm=tm),
        grid=(L // tm,),
        in_specs=[pl.BlockSpec((tm, D_MODEL), lambda i: (i, 0)),
                  pl.BlockSpec((1, D_MODEL), lambda i: (0, 0)),
                  pl.BlockSpec((1, D_MODEL), lambda i: (0, 0)),
                  resident(w_main), resident(w_dt)],
        out_specs=[pl.BlockSpec((tm, N_MAIN), lambda i: (i, 0)),
                   pl.BlockSpec((tm, LANES), lambda i: (i, 0))],
        out_shape=[jax.ShapeDtypeStruct((L, N_MAIN), BF16),
                   jax.ShapeDtypeStruct((L, LANES), F32)],
        scratch_shapes=[pltpu.VMEM((tm, D_MODEL), BF16)],
        compiler_params=_params("parallel"),
        name="ln_proj",
    )(x, shift, scale, w_main, w_dt)


CONV_HALO = 64
CONV_ROWS = 128


def _conv_kernel(xp_ref, x_ref, xn_ref, w_ref, b_ref, o_ref, ext_ref, *, tm):
    i = pl.program_id(0)
    last = pl.num_programs(0) - 1
    zero_halo = jnp.zeros_like(xp_ref[...])
    ext_ref[0:CONV_HALO, :] = jnp.where(i > 0, xp_ref[...], zero_halo)
    ext_ref[CONV_HALO:CONV_HALO + tm, :] = x_ref[...]
    ext_ref[CONV_HALO + tm:, :] = jnp.where(i < last, xn_ref[...], zero_halo)

    half = CONV_W // 2
    side_taps = [k for k in range(CONV_W) if k != half]
    win = CONV_ROWS + 2 * CONV_HALO
    rr = lax.broadcasted_iota(I32, (len(side_taps) * CONV_ROWS, win), 0)
    jj = lax.broadcasted_iota(I32, (len(side_taps) * CONV_ROWS, win), 1)
    blk = lax.shift_right_logical(rr, CONV_ROWS.bit_length() - 1)
    tap = jnp.where(blk < half, blk, blk + 1)
    selector = jnp.where(jj == (rr - blk * CONV_ROWS) + (CONV_HALO - half) + tap, 1.0, 0.0).astype(BF16)

    for r0 in range(0, tm, CONV_ROWS):
        shifted = _dot(selector, ext_ref[r0:r0 + win, :])
        centre = ext_ref[CONV_HALO + r0:CONV_HALO + r0 + CONV_ROWS, :].astype(F32)
        acc = b_ref[...] + w_ref[half:half + 1, :] * centre
        for n, k in enumerate(side_taps):
            acc = acc + w_ref[k:k + 1, :] * shifted[n * CONV_ROWS:(n + 1) * CONV_ROWS, :]
        o_ref[r0:r0 + CONV_ROWS, :] = _silu(acc).astype(o_ref.dtype)


def _conv(p_main, conv_w, conv_b, tm, tc=1024):
    L = p_main.shape[0]
    hb = tm // CONV_HALO
    n_halo = L // CONV_HALO
    c0 = OFF_XBC // tc
    return pl.pallas_call(
        functools.partial(_conv_kernel, tm=tm),
        grid=(L // tm, CONV_CH // tc),
        in_specs=[pl.BlockSpec((CONV_HALO, tc), lambda i, j: (jnp.maximum(i * hb - 1, 0), c0 + j)),
                  pl.BlockSpec((tm, tc), lambda i, j: (i, c0 + j)),
                  pl.BlockSpec((CONV_HALO, tc), lambda i, j: (jnp.minimum((i + 1) * hb, n_halo - 1), c0 + j)),
                  pl.BlockSpec((CONV_W, tc), lambda i, j: (0, j)),
                  pl.BlockSpec((1, tc), lambda i, j: (0, j))],
        out_specs=pl.BlockSpec((tm, tc), lambda i, j: (i, j)),
        out_shape=jax.ShapeDtypeStruct((L, CONV_CH), BF16),
        scratch_shapes=[pltpu.VMEM((tm + 2 * CONV_HALO, tc), BF16)],
        compiler_params=_params("parallel", "parallel"),
        name="conv",
    )(p_main, p_main, p_main, conv_w, conv_b)


def _qkv_kernel(*refs, tm, rope, with_q):
    if with_q:
        q_ref, k_ref, v_ref, rowtab_ref, coltab_ref, qo_ref, ko_ref, vo_ref = refs
    else:
        k_ref, v_ref, ko_ref, vo_ref = refs
    lane = lax.broadcasted_iota(I32, (tm, LANES), 1)
    first = (lane & 31) < 16
    lo = lane < 64

    if rope:
        rows_per_tile = tm // GRID_W
        r0 = pl.program_id(0) * rows_per_tile
        cos, sin = [jnp.concatenate([rowtab_ref[f, pl.ds(r0 + b, 1), :] + coltab_ref[f]
                                     for b in range(rows_per_tile)], axis=0) for f in range(2)]

    def rot(x):
        if not rope:
            return x
        r = jnp.where(first, -pltpu.roll(x, LANES - 16, 1), pltpu.roll(x, 16, 1))
        return x * cos + r * sin

    def dup(x, o_ref, p):
        r = pltpu.roll(x, 64, 1)
        o_ref[:, (2 * p) * LANES:(2 * p + 1) * LANES] = jnp.where(lo, x, r).astype(o_ref.dtype)
        o_ref[:, (2 * p + 1) * LANES:(2 * p + 2) * LANES] = jnp.where(lo, r, x).astype(o_ref.dtype)

    if with_q:
        for c in range(ATTN_WIDTH // LANES):
            sl = slice(c * LANES, (c + 1) * LANES)
            qo_ref[:, sl] = (rot(q_ref[:, sl].astype(F32)) * (ATTN_SCALE * LOG2_E)).astype(qo_ref.dtype)
    for p in range(KV_WIDTH // LANES):
        sl = slice(p * LANES, (p + 1) * LANES)
        dup(rot(k_ref[:, sl].astype(F32)), ko_ref, p)
        dup(v_ref[:, sl].astype(F32), vo_ref, p)


def _qkv(p_main, row_tab, col_tab, tm, rope, with_q):
    L = p_main.shape[0]
    kv_specs = [pl.BlockSpec((tm, KV_WIDTH), lambda i: (i, OFF_K // KV_WIDTH)),
                pl.BlockSpec((tm, KV_WIDTH), lambda i: (i, OFF_V // KV_WIDTH))]
    kv_out = [pl.BlockSpec((tm, KV_DUP), lambda i: (i, 0)), pl.BlockSpec((tm, KV_DUP), lambda i: (i, 0))]
    kv_shape = [jax.ShapeDtypeStruct((L, KV_DUP), BF16), jax.ShapeDtypeStruct((L, KV_DUP), BF16)]
    if with_q:
        in_specs = ([pl.BlockSpec((tm, ATTN_WIDTH), lambda i: (i, OFF_Q // ATTN_WIDTH))] + kv_specs
                    + [pl.BlockSpec(row_tab.shape, lambda i: (0, 0, 0)), pl.BlockSpec(col_tab.shape, lambda i: (0, 0, 0))])
        out_specs = [pl.BlockSpec((tm, ATTN_WIDTH), lambda i: (i, 0))] + kv_out
        out_shape = [jax.ShapeDtypeStruct((L, ATTN_WIDTH), BF16)] + kv_shape
        args = (p_main, p_main, p_main, row_tab, col_tab)
    else:
        in_specs, out_specs, out_shape, args = kv_specs, kv_out, kv_shape, (p_main, p_main)
    return pl.pallas_call(
        functools.partial(_qkv_kernel, tm=tm, rope=rope, with_q=with_q),
        grid=(L // tm,),
        in_specs=in_specs, out_specs=out_specs, out_shape=out_shape,
        compiler_params=_params("parallel"),
        name="qkv_rope" if with_q else "kv_ctx",
    )(*args)


PAIRS_PER_GROUP = (SSD_HEADS // SSD_GROUPS) // 2
GROUP_W = (SSD_HEADS // SSD_GROUPS) * SSD_HEAD_DIM


def _ssd_kernel(xf_ref, xb_ref, dtf_ref, dtb_ref, s0_ref, bias_ref, alog_ref, yf_ref, yb_ref, s_ref):
    Q = SSD_CHUNK

    @pl.when(pl.program_id(0) == 0)
    def _():
        s_ref[...] = s0_ref[...]

    row = lax.broadcasted_iota(I32, (Q, Q), 0)
    col = lax.broadcasted_iota(I32, (Q, Q), 1)
    lo = col < 64
    lo_row = lax.broadcasted_iota(I32, (1, Q), 1) < 64
    neg_a = -jnp.exp(alog_ref[...])

    for d, (x_ref, dt_ref, y_ref) in enumerate(((xf_ref, dtf_ref, yf_ref), (xb_ref, dtb_ref, yb_ref))):
        tri = (row >= col) if d == 0 else (row <= col)
        off = d * SSD_HEADS
        z = dt_ref[...] + bias_ref[...]
        dt = jnp.maximum(z, 0.0) + jnp.log1p(jnp.exp(-jnp.abs(z)))
        da = dt * neg_a
        a = _dot(jnp.where(tri, 1.0, 0.0), da, precision=HIGHEST) * LOG2_E
        a_t = a.T
        dt_t = dt.T
        edge = a_t[:, Q - 1:Q] if d == 0 else a_t[:, 0:1]
        w_t = (dt_t * jnp.exp2(edge - a_t)).astype(BF16)
        dec = jnp.exp2(edge)
        dt_tb = dt_t.astype(BF16)

        for g in range(SSD_GROUPS):
            bg = x_ref[:, D_INNER + g * SSD_STATE:D_INNER + (g + 1) * SSD_STATE]
            cg = x_ref[:, D_INNER + (SSD_GROUPS + g) * SSD_STATE:D_INNER + (SSD_GROUPS + g + 1) * SSD_STATE]
            cb = _dot_nt(cg, bg).astype(BF16)
            b_t = bg.astype(F32).T.astype(BF16)
            s_prev = s_ref[d, g]
            y_off = _dot(cg, s_prev.astype(BF16))
            for p in range(PAIRS_PER_GROUP):
                h0 = g * (SSD_HEADS // SSD_GROUPS) + 2 * p
                c0 = h0 * SSD_HEAD_DIM
                xpair = x_ref[:, c0:c0 + LANES]
                zero = jnp.zeros_like(xpair)
                y_diag = None
                st = None
                a_cols = []
                for t, xm in enumerate((jnp.where(lo, xpair, zero), jnp.where(lo, zero, xpair))):
                    h = off + h0 + t
                    a_col = jnp.broadcast_to(a[:, h:h + 1], (Q, Q))
                    a_cols.append(a_col)
                    seg = a_col - a_t[h:h + 1, :]
                    decay = jnp.exp2(jnp.where(tri, seg, -jnp.inf)).astype(BF16)
                    yd = _dot(decay * cb * dt_tb[h:h + 1, :], xm)
                    sd = _dot(b_t * w_t[h:h + 1, :], xm)
                    y_diag = yd if y_diag is None else y_diag + yd
                    st = sd if st is None else st + sd
                h = off + h0
                ea_pair = jnp.exp2(jnp.where(lo, a_cols[0], a_cols[1]))
                ps = slice(p * LANES, (p + 1) * LANES)
                y_ref[:, c0:c0 + LANES] = (y_diag + y_off[:, ps] * ea_pair).astype(y_ref.dtype)
                dec_pair = jnp.where(lo_row, dec[h:h + 1, :], dec[h + 1:h + 2, :])
                s_ref[d, g, :, ps] = s_prev[:, ps] * dec_pair + st


def _ssd(xc, dt, s0, bias, alog):
    L = xc.shape[0]
    nc = L // SSD_CHUNK
    sshape = (2, SSD_GROUPS, SSD_STATE, GROUP_W)
    return pl.pallas_call(
        _ssd_kernel,
        grid=(nc,),
        in_specs=[pl.BlockSpec((SSD_CHUNK, CONV_CH), lambda c: (c, 0)),
                  pl.BlockSpec((SSD_CHUNK, CONV_CH), lambda c: (nc - 1 - c, 0)),
                  pl.BlockSpec((SSD_CHUNK, LANES), lambda c: (c, 0)),
                  pl.BlockSpec((SSD_CHUNK, LANES), lambda c: (nc - 1 - c, 0)),
                  pl.BlockSpec(sshape, lambda c: (0, 0, 0, 0)),
                  pl.BlockSpec((1, LANES), lambda c: (0, 0)),
                  pl.BlockSpec((1, LANES), lambda c: (0, 0))],
        out_specs=[pl.BlockSpec((SSD_CHUNK, D_INNER), lambda c: (c, 0)),
                   pl.BlockSpec((SSD_CHUNK, D_INNER), lambda c: (nc - 1 - c, 0)),
                   pl.BlockSpec(sshape, lambda c: (0, 0, 0, 0))],
        out_shape=[jax.ShapeDtypeStruct((L, D_INNER), BF16),
                   jax.ShapeDtypeStruct((L, D_INNER), BF16),
                   jax.ShapeDtypeStruct(sshape, F32)],
        compiler_params=_params("arbitrary"),
        name="ssd",
    )(xc, xc, dt, dt, s0, bias, alog)


ATTN_BLOCKS_PER_STEP = 2


def _attn_kernel(sink_ref, q_ref, kp_ref, kc_ref, kn_ref, vp_ref, vc_ref, vn_ref, kx_ref, vx_ref, o_ref, *, s_ctx):
    i = pl.program_id(0)
    last = pl.num_programs(0) - 1
    B = ATTN_BLOCK
    nsub = ATTN_BLOCKS_PER_STEP
    span = 3 * B
    width = span + s_ctx
    t = lax.broadcasted_iota(I32, (B, width), 0)
    j = lax.broadcasted_iota(I32, (B, width), 1)
    in_window = jnp.abs(j - B - t) <= WINDOW
    lo = lax.broadcasted_iota(I32, (B, LANES), 1) < 64
    group = ATTN_Q_HEADS // ATTN_KV_HEADS

    for sb in range(nsub):
        rows = slice(sb * B, (sb + 1) * B)
        j_min = jnp.where(i > 0, 0, B) if sb == 0 else 0
        j_max = jnp.where(i < last, span, 2 * B) if sb == nsub - 1 else span
        valid = (j >= span) | (in_window & (j >= j_min) & (j < j_max))

        def window(prev_ref, cur_ref, next_ref, ctx_ref, sl):
            before = prev_ref[:, sl] if sb == 0 else cur_ref[(sb - 1) * B:sb * B, sl]
            after = next_ref[:, sl] if sb == nsub - 1 else cur_ref[(sb + 1) * B:(sb + 2) * B, sl]
            return jnp.concatenate([before, cur_ref[rows, sl], after, ctx_ref[:, sl]], axis=0)

        for kh in range(ATTN_KV_HEADS):
            sl = slice(kh * LANES, (kh + 1) * LANES)
            k_all = window(kp_ref, kc_ref, kn_ref, kx_ref, sl)
            v_all = window(vp_ref, vc_ref, vn_ref, vx_ref, sl)
            q_rows = []
            for u in range(group // 2):
                c = (group // 2) * kh + u
                qpair = q_ref[rows, c * LANES:(c + 1) * LANES]
                zero_q = jnp.zeros_like(qpair)
                q_rows += [jnp.where(lo, qpair, zero_q), jnp.where(lo, zero_q, qpair)]
            s_all = _dot_nt(jnp.concatenate(q_rows, axis=0), k_all)
            e_rows, inv = [], []
            for b in range(group):
                sink = sink_ref[group * kh + b] * LOG2_E
                s = jnp.where(valid, s_all[b * B:(b + 1) * B, :], -jnp.inf)
                m = jnp.maximum(jnp.max(s, axis=1, keepdims=True), sink)
                e = jnp.exp2(s - m)
                inv.append(1.0 / (jnp.sum(e, axis=1, keepdims=True) + jnp.exp2(sink - m)))
                e_rows.append(e.astype(BF16))
            o_all = _dot(jnp.concatenate(e_rows, axis=0), v_all)
            for u in range(group // 2):
                c = (group // 2) * kh + u
                b0 = 2 * u
                o = jnp.where(lo, o_all[b0 * B:(b0 + 1) * B, :] * inv[b0],
                              o_all[(b0 + 1) * B:(b0 + 2) * B, :] * inv[b0 + 1])
                o_ref[rows, c * LANES:(c + 1) * LANES] = o.astype(o_ref.dtype)


def _attn(sink, q, kd, vd, kx, vx):
    L = q.shape[0]
    s_ctx = kx.shape[0]
    nsub = ATTN_BLOCKS_PER_STEP
    nb = L // ATTN_BLOCK
    prev = lambda i: (jnp.maximum(nsub * i - 1, 0), 0)
    cur = lambda i: (i, 0)
    nxt = lambda i: (jnp.minimum(nsub * (i + 1), nb - 1), 0)
    edge = lambda f: pl.BlockSpec((ATTN_BLOCK, KV_DUP), f)
    body = pl.BlockSpec((nsub * ATTN_BLOCK, KV_DUP), cur)
    return pl.pallas_call(
        functools.partial(_attn_kernel, s_ctx=s_ctx),
        grid=(nb // nsub,),
        in_specs=[pl.BlockSpec(memory_space=pltpu.SMEM),
                  pl.BlockSpec((nsub * ATTN_BLOCK, ATTN_WIDTH), cur),
                  edge(prev), body, edge(nxt), edge(prev), body, edge(nxt),
                  pl.BlockSpec((s_ctx, KV_DUP), lambda i: (0, 0)),
                  pl.BlockSpec((s_ctx, KV_DUP), lambda i: (0, 0))],
        out_specs=pl.BlockSpec((nsub * ATTN_BLOCK, ATTN_WIDTH), cur),
        out_shape=jax.ShapeDtypeStruct((L, ATTN_WIDTH), BF16),
        compiler_params=_params("parallel"),
        name="attn",
    )(sink, q, kd, kd, kd, vd, vd, vd, kx, vx)


OUT_CHUNK = 256


def _out_kernel(yf_ref, yb_ref, xs_ref, z_ref, g_ref, ya_ref, x_ref, dsk_ref, nw_ref, wssd_ref, wattn_ref, wo_ref,
                bo_ref, g1_ref, ln1g_ref, ln1b_ref, sh2_ref, sc2_ref, wrh_ref, wrl_ref, x1_ref, h2_ref, aff_ref):
    br_attn = _dot(ya_ref[...], wattn_ref[...])
    sumsq = None
    br_ssd = None
    for c0 in range(0, D_INNER, OUT_CHUNK):
        cs = slice(c0, c0 + OUT_CHUNK)
        y = yf_ref[:, cs].astype(F32) + yb_ref[:, cs].astype(F32) + dsk_ref[:, cs] * xs_ref[:, cs].astype(F32)
        y = y * _silu(z_ref[:, cs].astype(F32))
        sq = jnp.sum(y * y, -1, keepdims=True)
        part = _dot((y * nw_ref[:, cs]).astype(BF16), wssd_ref[cs, :])
        sumsq = sq if sumsq is None else sumsq + sq
        br_ssd = part if br_ssd is None else br_ssd + part
    br_ssd = br_ssd * lax.rsqrt(sumsq * (1.0 / D_INNER) + LN_EPS)
    g_ssd = _sigmoid(g_ref[:, :D_MODEL].astype(F32))
    g_attn = _sigmoid(g_ref[:, D_MODEL:].astype(F32))
    m = g_ssd * br_ssd + g_attn * br_attn
    o = _dot(m.astype(BF16), wo_ref[...]) + bo_ref[...]
    x1 =_ln(DEEPNORM_ALPHA * x_ref[...] + g1_ref[...] * o) * ln1g_ref[...] + ln1b_ref[...]
    x1_ref[...] = x1
    h2 = _ln(x1) * (1.0 + sc2_ref[...]) + sh2_ref[...]
    h2_ref[...] = h2.astype(h2_ref.dtype)
    h_hi = h2.astype(BF16)
    h_lo = (h2 - h_hi.astype(F32)).astype(BF16)
    logits = _dot_nt(h_hi, wrh_ref[...]) + (_dot_nt(h_hi, wrl_ref[...]) + _dot_nt(h_lo, wrh_ref[...]))
    lane = lax.broadcasted_iota(I32, logits.shape, 1)
    logits = jnp.where(lane < N_EXPERTS, logits, -jnp.inf)
    e = jnp.exp(logits - jnp.max(logits, axis=1, keepdims=True))
    aff = e / jnp.sum(e, axis=1, keepdims=True)
    aff_ref[...] = aff.T[:N_EXPERTS, :]


def _out(yf, yb, xc, p_main, ya, x, dsk, nw, wssd, wattn, wo, bo, g1, ln1g, ln1b, sh2, sc2, wrh, wrl, tm):
    L = x.shape[0]
    row = lambda w: pl.BlockSpec((1, w), lambda i: (0, 0))
    full = lambda a: pl.BlockSpec(a.shape, lambda i: (0, 0))
    return pl.pallas_call(
        _out_kernel,
        grid=(L // tm,),
        in_specs=[pl.BlockSpec((tm, D_INNER), lambda i: (i, 0)),
                  pl.BlockSpec((tm, D_INNER), lambda i: (i, 0)),
                  pl.BlockSpec((tm, D_INNER), lambda i: (i, 0)),
                  pl.BlockSpec((tm, D_INNER), lambda i: (i, OFF_Z // D_INNER)),
                  pl.BlockSpec((tm, 2 * D_MODEL), lambda i: (i, OFF_G // (2 * D_MODEL))),
                  pl.BlockSpec((tm, ATTN_WIDTH), lambda i: (i, 0)),
                  pl.BlockSpec((tm, D_MODEL), lambda i: (i, 0)),
                  row(D_INNER), row(D_INNER), full(wssd), full(wattn), full(wo),
                  row(D_MODEL), row(D_MODEL), row(D_MODEL), row(D_MODEL), row(D_MODEL), row(D_MODEL),
                  full(wrh), full(wrl)],
        out_specs=[pl.BlockSpec((tm, D_MODEL), lambda i: (i, 0)),
                   pl.BlockSpec((tm, D_MODEL), lambda i: (i, 0)),
                   pl.BlockSpec((N_EXPERTS, tm), lambda i: (0, i))],
        out_shape=[jax.ShapeDtypeStruct((L, D_MODEL), F32),
                   jax.ShapeDtypeStruct((L, D_MODEL), BF16),
                   jax.ShapeDtypeStruct((N_EXPERTS, L), F32)],
        compiler_params=_params("parallel"),
        name="out_proj",
    )(yf, yb, xc, p_main, p_main, ya, x, dsk, nw, wssd, wattn, wo, bo, g1, ln1g, ln1b, sh2, sc2, wrh, wrl)


def _count(mask):
    c = jnp.sum(jnp.where(mask, 1.0, 0.0), axis=1, keepdims=True)
    return jnp.sum(c, axis=0, keepdims=True)


def _route_kernel(aff_ref, rank_ref, start_ref, cnt_ref, sel_ref, *, cap, nrows):
    E = N_EXPERTS
    R = nrows
    tok = (lax.broadcasted_iota(I32, (R, LANES), 0) * LANES + lax.broadcasted_iota(I32, (R, LANES), 1))
    capf = jnp.float32(cap)

    def as_f32(word):
        return pltpu.bitcast(word, F32)

    def thr_body(it, ts):
        bit = lax.shift_left(jnp.int32(1), 30 - it)
        out = []
        for e in range(E):
            cand = ts[e] | bit
            out.append(jnp.where(_count(aff_ref[e] >= as_f32(cand)) >= capf, cand, ts[e]))
        return tuple(out)

    ts = lax.fori_loop(0, 31, thr_body, tuple(jnp.zeros((1, 1), I32) for _ in range(E)))
    thr = [as_f32(t) for t in ts]
    need = [capf - _count(aff_ref[e] > thr[e]) for e in range(E)]

    nbits = max((R * LANES - 1).bit_length(), 1)

    def tie_body(it, js):
        bit = lax.shift_left(jnp.int32(1), nbits - 1 - it)
        out = []
        for e in range(E):
            cand = js[e] | bit
            f = _count((aff_ref[e] == thr[e]) & (tok < cand))
            out.append(jnp.where(f < need[e], cand, js[e]))
        return tuple(out)

    js = lax.fori_loop(0, nbits, tie_body, tuple(jnp.zeros((1, 1), I32) for _ in range(E)))
    for e in range(E):
        a = aff_ref[e]
        sel = (a > thr[e]) | ((a == thr[e]) & (tok <= js[e]))
        sel_ref[e] = jnp.where(sel, 1.0, 0.0)

    li = lax.broadcasted_iota(I32, (LANES, LANES), 0)
    lj = lax.broadcasted_iota(I32, (LANES, LANES), 1)
    upper_incl = jnp.where(li <= lj, 1.0, 0.0).astype(BF16)
    ri = lax.broadcasted_iota(I32, (R, R), 0)
    rj = lax.broadcasted_iota(I32, (R, R), 1)
    row_lower = jnp.where(ri >= rj, 1.0, 0.0).astype(BF16)
    lane_id = lax.broadcasted_iota(I32, (R, LANES), 1)

    def expert_body(e, carry):
        start_acc, cnt_acc = carry
        sel = sel_ref[e]
        w = _dot(sel.astype(BF16), upper_incl)
        tot_b = jnp.broadcast_to(jnp.sum(sel, axis=1, keepdims=True), (R, LANES))
        seg_b = jnp.floor((tot_b + (SUBLANES - 1)) * (1.0 / SUBLANES)) * SUBLANES
        cum_incl = _dot(row_lower, seg_b.astype(BF16))
        rank_ref[e] = jnp.where(sel > 0.0, w - 1.0, -1.0)
        start_acc = jnp.where(lane_id == e, cum_incl - seg_b, start_acc)
        cnt_acc = jnp.where(lane_id == e, tot_b, cnt_acc)
        return start_acc, cnt_acc

    zero = jnp.zeros((R, LANES), F32)
    start_acc, cnt_acc = lax.fori_loop(0, E, expert_body, (zero, zero))
    start_ref[...] = start_acc.astype(I32)
    cnt_ref[...] = cnt_acc.astype(I32)


def _route(aff3, cap):
    E, R, _ = aff3.shape
    return pl.pallas_call(
        functools.partial(_route_kernel, cap=cap, nrows=R),
        out_shape=[jax.ShapeDtypeStruct((E, R, LANES), F32),
                   jax.ShapeDtypeStruct((R, LANES), I32),
                   jax.ShapeDtypeStruct((R, LANES), I32)],
        scratch_shapes=[pltpu.VMEM((E, R, LANES), F32)],
        compiler_params=pltpu.CompilerParams(vmem_limit_bytes=VMEM_LIMIT_BYTES),
        name="route",
    )(aff3)


SLOTS = 32
WIN = N_EXPERTS * SLOTS


def _slot_onehot(tgt, values, k):
    sub = lax.broadcasted_iota(I32, (SLOTS, tgt.shape[1]), 0).astype(F32) + (SLOTS * k).astype(F32)
    rows = []
    for e in range(N_EXPERTS):
        hit = jnp.broadcast_to(tgt[e:e + 1, :], sub.shape) == sub
        rows.append(jnp.where(hit, jnp.broadcast_to(values[e:e + 1, :], sub.shape), 0.0))
    return jnp.concatenate(rows, axis=0).astype(BF16)


def _rounds(cnt_ref, r):
    n = cnt_ref[0, r]
    for e in range(1, N_EXPERTS):
        n = jnp.maximum(n, cnt_ref[e, r])
    return lax.shift_right_logical(n + (SLOTS - 1), SLOTS.bit_length() - 1)


def _dispatch_kernel(start_ref, cnt_ref, h_ref, rank_ref, xs_hbm, stage_ref, ovf_ref, sem, osem):
    r = pl.program_id(0)
    last = pl.num_programs(0) - 1
    slot = lax.rem(r, 2)
    rank = rank_ref[:, 0, 0, :]
    ones = jnp.ones_like(rank)

    def compact(k):
        return _dot(_slot_onehot(rank, ones, k), h_ref[...])

    def window_copy(src_ref, e, k, s):
        return pltpu.make_async_copy(src_ref.at[pl.ds(e * SLOTS, SLOTS), :],
                                     xs_hbm.at[e, pl.ds(pl.multiple_of(start_ref[e, r] + SLOTS * k, SUBLANES), SLOTS), :], s)

    def stage_wait(sl):
        pltpu.make_async_copy(stage_ref.at[sl], stage_ref.at[sl], sem.at[sl]).wait()

    stage_ref[slot] = compact(jnp.int32(0))

    @pl.when(r > 0)
    def _():
        stage_wait(1 - slot)

    for e in range(N_EXPERTS):
        window_copy(stage_ref.at[slot], e, 0, sem.at[slot]).start()

    def extra(k, _):
        ovf_ref[...] = compact(k)
        for e in range(N_EXPERTS):
            @pl.when(cnt_ref[e, r] > SLOTS * k)
            def _():
                window_copy(ovf_ref, e, k, osem).start()
        for e in range(N_EXPERTS):
            @pl.when(cnt_ref[e, r] > SLOTS * k)
            def _():
                window_copy(ovf_ref, e, k, osem).wait()
        return 0

    lax.fori_loop(1, _rounds(cnt_ref, r), extra, 0)

    def tail_copies(act):
        n_rows = xs_hbm.shape[1]
        for e in range(N_EXPERTS):
            n_win = jnp.maximum(lax.shift_right_logical(cnt_ref[e, r] + (SLOTS - 1), SLOTS.bit_length() - 1), 1)
            off0 = start_ref[e, r] + SLOTS * n_win
            length = n_rows - off0
            n_full = lax.div(length, WIN)

            def full(i, _, e=e, off0=off0):
                act(pltpu.make_async_copy(ovf_ref, xs_hbm.at[e, pl.ds(pl.multiple_of(off0 + i * WIN, SUBLANES), WIN), :],
                                          osem))
                return 0

            lax.fori_loop(0, n_full, full, 0)
            off = off0 + n_full * WIN
            rem = length - n_full * WIN
            size = WIN // 2
            while size >= SUBLANES:
                @pl.when((rem & size) != 0)
                def _(e=e, off=off, size=size):
                    act(pltpu.make_async_copy(ovf_ref.at[pl.ds(0, size), :],
                                              xs_hbm.at[e, pl.ds(pl.multiple_of(off, SUBLANES), size), :], osem))
                off = off + (rem & size)
                size //= 2

    @pl.when(r == last)
    def _():
        stage_wait(slot)
        ovf_ref[...] = jnp.zeros(ovf_ref.shape, F32)
        tail_copies(lambda cp: cp.start())
        tail_copies(lambda cp: cp.wait())


def _padded_cap(cap, n_tiles, ts):
    worst = cap + (SUBLANES - 1) * n_tiles
    return -(-worst // ts) * ts


def _dispatch(start, cnt, h2, rank4, cap):
    L = h2.shape[0]
    T = LANES
    return pl.pallas_call(
        _dispatch_kernel,
        grid_spec=pltpu.PrefetchScalarGridSpec(
            num_scalar_prefetch=2,
            grid=(L // T,),
            in_specs=[pl.BlockSpec((T, D_MODEL), lambda r, *_: (r, 0)),
                      pl.BlockSpec((N_EXPERTS, 1, 1, T), lambda r, *_: (0, r, 0, 0))],
            out_specs=pl.BlockSpec(memory_space=pl.ANY),
            scratch_shapes=[pltpu.VMEM((2, WIN, D_MODEL), F32),
                            pltpu.VMEM((WIN, D_MODEL), F32),
                            pltpu.SemaphoreType.DMA((2,)),
                            pltpu.SemaphoreType.DMA(())]),
        out_shape=jax.ShapeDtypeStruct((N_EXPERTS, cap + SLOTS, D_MODEL), F32),
        compiler_params=_params("arbitrary"),
        name="dispatch",
    )(start, cnt, h2, rank4)


def _expert_rows(start_ref, cnt_ref, e):
    last_tile = start_ref.shape[1] - 1
    shift = SUBLANES.bit_length() - 1
    seg = lax.shift_left(lax.shift_right_logical(cnt_ref[e, last_tile] + (SUBLANES - 1), shift), shift)
    return start_ref[e, last_tile] + seg


def _expert_kernel(start_ref, cnt_ref, xs_ref, wgu_hbm, wd_hbm, o_ref, wgu_f32, wd_f32, wgu_bf, wd_bf, sem, *, ts):
    e = pl.program_id(0)
    j = pl.program_id(1)
    n_experts = pl.num_programs(0)
    live = _expert_rows(start_ref, cnt_ref, e) - j * ts

    def weight_copies(ee, slot):
        return (pltpu.make_async_copy(wgu_hbm.at[ee], wgu_f32.at[slot], sem.at[0, slot]),
                pltpu.make_async_copy(wd_hbm.at[ee], wd_f32.at[slot], sem.at[1, slot]))

    @pl.when(j == 0)
    def _():
        slot = lax.rem(e, 2)

        @pl.when(e == 0)
        def _():
            for cp in weight_copies(e, slot):
                cp.start()

        for cp in weight_copies(e, slot):
            cp.wait()

        @pl.when(e + 1 < n_experts)
        def _():
            for cp in weight_copies(e + 1, 1 - slot):
                cp.start()

        rows = 128
        for r0 in range(0, D_MODEL, rows):
            wgu_bf[r0:r0 + rows, :] = wgu_f32[slot, r0:r0 + rows, :].astype(BF16)
        for r0 in range(0, EXPERT_FF, rows):
            wd_bf[r0:r0 + rows, :] = wd_f32[slot, r0:r0 + rows, :].astype(BF16)

    @pl.when(live > 0)
    def _():
        row = lax.broadcasted_iota(I32, (ts, D_MODEL), 0)
        xs = jnp.where(row < live, xs_ref[0], 0.0).astype(BF16)
        gu = _dot(xs, wgu_bf[...])
        act = _silu(gu[:, :EXPERT_FF]) * gu[:, EXPERT_FF:]
        o_ref[0] = _dot(act.astype(BF16), wd_bf[...])

    @pl.when(live <= 0)
    def _():
        o_ref[0] = jnp.zeros((ts, D_MODEL), F32)


def _expert(start, cnt, xs, wgu, wd, capp, ts):
    E = xs.shape[0]
    return pl.pallas_call(
        functools.partial(_expert_kernel, ts=ts),
        grid_spec=pltpu.PrefetchScalarGridSpec(
            num_scalar_prefetch=2,
            grid=(E, capp // ts),
            in_specs=[pl.BlockSpec((1, ts, D_MODEL), lambda e, j, start, cnt: (
                          e, jnp.minimum(j, jnp.maximum(_expert_rows(start, cnt, e) - 1, 0) // ts), 0)),
                      pl.BlockSpec(memory_space=pl.ANY),
                      pl.BlockSpec(memory_space=pl.ANY)],
            out_specs=pl.BlockSpec((1, ts, D_MODEL), lambda e, j, *_: (e, j, 0)),
            scratch_shapes=[pltpu.VMEM((2, D_MODEL, 2 * EXPERT_FF), F32),
                            pltpu.VMEM((2, EXPERT_FF, D_MODEL), F32),
                            pltpu.VMEM((D_MODEL, 2 * EXPERT_FF), BF16),
                            pltpu.VMEM((EXPERT_FF, D_MODEL), BF16),
                            pltpu.SemaphoreType.DMA((2, 2))]),
        out_shape=jax.ShapeDtypeStruct((E, capp, D_MODEL), F32),
        compiler_params=_params("arbitrary", "arbitrary"),
        name="expert_ffn",
    )(start, cnt, xs, wgu, wd)


def _combine_kernel(start_ref, cnt_ref, ye_hbm, x1_ref, aff_ref, rank_ref, g2_ref, lng_ref, lnb_ref,
                    o_ref, buf_ref, sem, *, cap):
    r = pl.program_id(0)
    last = pl.num_programs(0) - 1
    slot = lax.rem(r, 2)

    def win_start(e, rr, k):
        return pl.multiple_of(jnp.minimum(start_ref[e, rr] + SLOTS * k, cap - SLOTS), SUBLANES)

    def fetch(rr, k, sl):
        for e in range(N_EXPERTS):
            pltpu.make_async_copy(ye_hbm.at[e, pl.ds(win_start(e, rr, k), SLOTS), :],
                                  buf_ref.at[sl, pl.ds(e * SLOTS, SLOTS), :], sem.at[sl]).start()

    def fetch_wait(sl):
        pltpu.make_async_copy(buf_ref.at[sl], buf_ref.at[sl], sem.at[sl]).wait()

    @pl.when(r == 0)
    def _():
        fetch(r, 0, slot)

    @pl.when(r < last)
    def _():
        fetch(r + 1, 0, 1 - slot)

    rank = rank_ref[:, 0, 0, :]
    aff = aff_ref[...]
    g_hi = aff.astype(BF16).astype(F32)
    g_lo = aff - g_hi
    e_id = lax.broadcasted_iota(I32, rank.shape, 0)

    def contribution(k, sl):
        shift = jnp.zeros_like(rank)
        for e in range(N_EXPERTS):
            d = start_ref[e, r] + SLOTS * k - win_start(e, r, k)
            shift = jnp.where(e_id == e, d.astype(F32), shift)
        base = (SLOTS * k).astype(F32)
        in_round = (rank >= base) & (rank < base + SLOTS)
        tgt = jnp.where(in_round, rank + shift, -1.0)
        yb = buf_ref[sl].astype(BF16)
        tn = (((0,), (0,)), ((), ()))
        return (lax.dot_general(_slot_onehot(tgt, g_hi, k), yb, tn, preferred_element_type=F32)
                + lax.dot_general(_slot_onehot(tgt, g_lo, k), yb, tn, preferred_element_type=F32))

    fetch_wait(slot)
    moe = contribution(jnp.int32(0), slot)

    def extra(k, acc):
        fetch(r, k, slot)
        fetch_wait(slot)
        return acc + contribution(k, slot)

    moe = lax.fori_loop(1, _rounds(cnt_ref, r), extra, moe)
    y = _ln(DEEPNORM_ALPHA * x1_ref[...] + g2_ref[...] * moe)
    o_ref[...] = y * lng_ref[...] + lnb_ref[...]


def _combine(start, cnt, ye, x1, aff_t, rank4, g2, lng, lnb):
    L = x1.shape[0]
    E, cap, _ = ye.shape
    T = LANES
    row = pl.BlockSpec((1, D_MODEL), lambda r, *_: (0, 0))
    return pl.pallas_call(
        functools.partial(_combine_kernel, cap=cap),
        grid_spec=pltpu.PrefetchScalarGridSpec(
            num_scalar_prefetch=2,
            grid=(L // T,),
            in_specs=[pl.BlockSpec(memory_space=pl.ANY),
                      pl.BlockSpec((T, D_MODEL), lambda r, *_: (r, 0)),
                      pl.BlockSpec((E, T), lambda r, *_: (0, r)),
                      pl.BlockSpec((E, 1, 1, T), lambda r, *_: (0, r, 0, 0)),
                      row, row, row],
            out_specs=pl.BlockSpec((T, D_MODEL), lambda r, *_: (r, 0)),
            scratch_shapes=[pltpu.VMEM((2, WIN, D_MODEL), F32),
                            pltpu.SemaphoreType.DMA((2,))]),
        out_shape=jax.ShapeDtypeStruct((L, D_MODEL), F32),
        compiler_params=_params("arbitrary"),
        name="combine",
    )(start, cnt, ye, x1, aff_t, rank4, g2, lng, lnb)


def _rope_tables(L):
    inv_freq = ROPE_BASE ** (-jnp.arange(0, ROPE_AXIS_DIM, 2, dtype=F32) / ROPE_AXIS_DIM)

    def table(n, on_row_lanes):
        ang = jnp.arange(n, dtype=F32)[:, None] * inv_freq[None, :]
        one, zero = jnp.ones_like(ang), jnp.zeros_like(ang)
        mask = jnp.concatenate(([one, one, zero, zero] if on_row_lanes else [zero, zero, one, one]) * 2, -1)
        ang = jnp.concatenate([ang] * 8, -1)
        return jnp.stack([jnp.cos(ang) * mask, jnp.sin(ang) * mask])

    return table(L // GRID_W, True), table(GRID_W, False)


def _row_tile(L, pref):
    return pref if L % pref == 0 else L


def kernel(x, c, ctx, c_ctx, w_ada, b_ada, w_in, conv_w, conv_b, a_log, dt_bias, d_skip, ssd_norm_w, attn_sink,
           w_ssd_br, w_attn_br, w_o, b_o, ln1_g, ln1_b, w_router, w_gate_up, w_down, ln2_g, ln2_b):
    assert x.shape[0] == 1 and w_in.shape[0] == 1, "single batch element, depth 1"
    L = x.shape[1]
    S = ctx.shape[1]
    assert L % (LANES * SUBLANES) == 0 and S % SSD_CHUNK == 0
    cap = CAPACITY_FACTOR * L // N_EXPERTS
    x2, ctx2 = x[0], ctx[0]

    cc = jnp.zeros((SUBLANES, D_MODEL), F32).at[0].set(c[0]).at[1].set(c_ctx)
    mod = _ada(cc, w_ada[0], b_ada[0][None, :])
    sh1, sc1, g1, sh2, sc2, g2 = [mod[0:1, k * D_MODEL:(k + 1) * D_MODEL] for k in range(6)]
    csh1, csc1 = mod[1:2, 0:D_MODEL], mod[1:2, D_MODEL:2 * D_MODEL]

    w = w_in[0]
    o = 0
    parts = {}
    for name, width in (("z", D_INNER), ("xbc", CONV_CH), ("dt", 2 * SSD_HEADS), ("q", ATTN_WIDTH),
                        ("k", KV_WIDTH), ("v", KV_WIDTH), ("g", 2 * D_MODEL)):
        parts[name] = w[:, o:o + width]
        o += width
    w_main = jnp.concatenate([parts[n] for n in ("z", "g", "xbc", "q", "k", "v")], axis=1).astype(BF16)
    w_dt = jnp.pad(parts["dt"], ((0, 0), (0, LANES - 2 * SSD_HEADS))).astype(BF16)

    pad_heads = lambda v: jnp.pad(v.reshape(1, 2 * SSD_HEADS), ((0, 0), (0, LANES - 2 * SSD_HEADS)))
    bias_row, alog_row = pad_heads(dt_bias[0]), pad_heads(a_log[0])
    conv_b2 = conv_b[0][None, :]

    tm_c = _row_tile(S, 256)
    pc, dtc = _ln_proj(ctx2, csh1, csc1, w_main, w_dt, tm_c)
    xcc = _conv(pc, conv_w[0], conv_b2, tm_c)
    kx, vx = _qkv(pc, None, None, tm_c, rope=False, with_q=False)
    s_zero = jnp.zeros((2, SSD_GROUPS, SSD_STATE, GROUP_W), F32)
    _, _, s_ctx = _ssd(xcc, dtc, s_zero, bias_row, alog_row)

    p_main, dt = _ln_proj(x2, sh1, sc1, w_main, w_dt, _row_tile(L, 512))
    xc = _conv(p_main, conv_w[0], conv_b2, _row_tile(L, 512))
    row_tab, col_tab = _rope_tables(L)
    q, kd, vd = _qkv(p_main, row_tab, col_tab, _row_tile(L, 512), rope=True, with_q=True)
    yf, yb, _ = _ssd(xc, dt, s_ctx, bias_row, alog_row)
    ya = _attn(attn_sink[0], q, kd, vd, kx, vx)

    dsk = jnp.repeat(d_skip[0], SSD_HEAD_DIM)[None, :]
    wr = jnp.pad(w_router[0].T, ((0, LANES - N_EXPERTS), (0, 0)))
    wr_hi = wr.astype(BF16)
    wr_lo = (wr - wr_hi.astype(F32)).astype(BF16)
    x1, h2, aff_t = _out(yf, yb, xc, p_main, ya, x2, dsk, ssd_norm_w[0][None, :],
                         w_ssd_br[0].astype(BF16), w_attn_br[0].astype(BF16), w_o[0].astype(BF16),
                         b_o[0][None, :], g1, ln1_g[0][None, :], ln1_b[0][None, :], sh2, sc2, wr_hi, wr_lo,
                         _row_tile(L, 256))

    R = L // LANES
    rank, start_t, cnt_t = _route(aff_t.reshape(N_EXPERTS, R, LANES), cap)
    rank4 = rank.reshape(N_EXPERTS, R, 1, LANES)
    start = start_t[:, :N_EXPERTS].T
    cnt = cnt_t[:, :N_EXPERTS].T
    ts = min(cap, 512)
    capp = _padded_cap(cap, R, ts)
    xs = _dispatch(start, cnt, h2, rank4, capp)
    ye = _expert(start, cnt, xs, w_gate_up[0], w_down[0], capp, ts)
    out = _combine(start, cnt, ye, x1, aff_t, rank4, g2, ln2_g[0][None, :], ln2_b[0][None, :])
    return out[None]
```

```python
import functools

import numpy as np
import jax
import jax.numpy as jnp
from jax import lax
from jax.experimental import pallas as pl
from jax.experimental.pallas import tpu as pltpu

F32 = jnp.float32
BF16 = jnp.bfloat16
I32 = jnp.int32
HIGHEST = lax.Precision.HIGHEST
LOG2_E = 1.4426950408889634

LANES = 128
SUBLANES = 8
VMEM_LIMIT_BYTES = 56 * 1024 * 1024

D_MODEL = 1024
GRID_W = 64
LN_EPS = 1e-5
DEPTH = 1
DEEPNORM_ALPHA = (2.0 * DEPTH) ** 0.25
D_INNER = 2048
SSD_HEAD_DIM = 64
SSD_HEADS = 32
SSD_GROUPS = 4
SSD_STATE = 128
SSD_CHUNK = 128
CONV_W = 5
CONV_CH = D_INNER + 2 * SSD_GROUPS * SSD_STATE
ATTN_HEAD_DIM = 64
ATTN_Q_HEADS = 16
ATTN_KV_HEADS = 4
ATTN_WIDTH = 1024
KV_WIDTH = 256
WINDOW = 128
ATTN_BLOCK = 128
ATTN_SCALE = ATTN_HEAD_DIM ** -0.5
ROPE_BASE = 10000.0
ROPE_AXIS_DIM = ATTN_HEAD_DIM // 2
N_EXPERTS = 16
EXPERT_FF = 1024
CAPACITY_FACTOR = 2

OFF_Z = 0
OFF_G = OFF_Z + D_INNER
OFF_XBC = OFF_G + 2 * D_MODEL
OFF_Q = OFF_XBC + CONV_CH
OFF_K = OFF_Q + ATTN_WIDTH
OFF_V = OFF_K + KV_WIDTH
N_MAIN = OFF_V + KV_WIDTH
PROJ_TN = N_MAIN // 4
KV_DUP = 2 * KV_WIDTH


def _params(*sem):
    return pltpu.CompilerParams(dimension_semantics=sem, vmem_limit_bytes=VMEM_LIMIT_BYTES)


def _sigmoid(x):
    return 1.0 / (1.0 + jnp.exp2(x * (-LOG2_E)))


def _silu(x):
    return x * _sigmoid(x)


def _dot(a, b, **kw):
    return jnp.dot(a, b, preferred_element_type=F32, **kw)


def _dot_nt(a, b, **kw):
    return lax.dot_general(a, b, (((1,), (1,)), ((), ())), preferred_element_type=F32, **kw)


def _ln(x):
    mu = jnp.mean(x, -1, keepdims=True)
    xc = x - mu
    var = jnp.mean(xc * xc, -1, keepdims=True)
    return xc * lax.rsqrt(var + LN_EPS)


def _ada_kernel(s_ref, w_ref, b_ref, o_ref):
    s = _silu(s_ref[...])
    o_ref[...] = _dot(s, w_ref[...], precision=HIGHEST) + b_ref[...]


def _ada(cc, w_ada, b_ada):
    n = w_ada.shape[1]
    tn = 1024
    return pl.pallas_call(
        _ada_kernel,
        grid=(n // tn,),
        in_specs=[pl.BlockSpec((SUBLANES, D_MODEL), lambda j: (0, 0)),
                  pl.BlockSpec((D_MODEL, tn), lambda j: (0, j)),
                  pl.BlockSpec((1, tn), lambda j: (0, j))],
        out_specs=pl.BlockSpec((SUBLANES, tn), lambda j: (0, j)),
        out_shape=jax.ShapeDtypeStruct((SUBLANES, n), F32),
        compiler_params=_params("parallel"),
        name="ada",
    )(cc, w_ada, b_ada)


def _ln_proj_kernel(x_ref, sh_ref, sc_ref, w_ref, wdt_ref, o_ref, dt_ref, h_ref, *, tm):
    rows = 256 if tm % 256 == 0 else tm
    for r0 in range(0, tm, rows):
        rs = slice(r0, r0 + rows)
        h = _ln(x_ref[rs, :]) * (1.0 + sc_ref[...]) + sh_ref[...]
        h_ref[rs, :] = h.astype(BF16)
        dt_ref[rs, :] = _dot(h_ref[rs, :], wdt_ref[...])
        for n0 in range(0, N_MAIN, PROJ_TN):
            o_ref[rs, n0:n0 + PROJ_TN] = _dot(h_ref[rs, :], w_ref[:, n0:n0 + PROJ_TN]).astype(o_ref.dtype)


def _ln_proj(x, shift, scale, w_main, w_dt, tm):
    L = x.shape[0]
    resident = lambda a: pl.BlockSpec(a.shape, lambda i: (0, 0), pipeline_mode=pl.Buffered(1))
    return pl.pallas_call(
        functools.partial(_ln_proj_kernel, tm=tm),
        grid=(L // tm,),
        in_specs=[pl.BlockSpec((tm, D_MODEL), lambda i: (i, 0)),
                  pl.BlockSpec((1, D_MODEL), lambda i: (0, 0)),
                  pl.BlockSpec((1, D_MODEL), lambda i: (0, 0)),
                  resident(w_main), resident(w_dt)],
        out_specs=[pl.BlockSpec((tm, N_MAIN), lambda i: (i, 0)),
                   pl.BlockSpec((tm, LANES), lambda i: (i, 0))],
        out_shape=[jax.ShapeDtypeStruct((L, N_MAIN), BF16),
                   jax.ShapeDtypeStruct((L, LANES), F32)],
        scratch_shapes=[pltpu.VMEM((tm, D_MODEL), BF16)],
        compiler_params=_params("parallel"),
        name="ln_proj",
    )(x, shift, scale, w_main, w_dt)


CONV_HALO = 64
CONV_ROWS = 128


def _conv_kernel(xp_ref, x_ref, xn_ref, w_ref, b_ref, o_ref, ext_ref, *, tm):
    i = pl.program_id(0)
    last = pl.num_programs(0) - 1
    zero_halo = jnp.zeros_like(xp_ref[...])
    ext_ref[0:CONV_HALO, :] = jnp.where(i > 0, xp_ref[...], zero_halo)
    ext_ref[CONV_HALO:CONV_HALO + tm, :] = x_ref[...]
    ext_ref[CONV_HALO + tm:, :] = jnp.where(i < last, xn_ref[...], zero_halo)

    half = CONV_W // 2
    side_taps = [k for k in range(CONV_W) if k != half]
    win = CONV_ROWS + 2 * CONV_HALO
    rr = lax.broadcasted_iota(I32, (len(side_taps) * CONV_ROWS, win), 0)
    jj = lax.broadcasted_iota(I32, (len(side_taps) * CONV_ROWS, win), 1)
    blk = lax.shift_right_logical(rr, CONV_ROWS.bit_length() - 1)
    tap = jnp.where(blk < half, blk, blk + 1)
    selector = jnp.where(jj == (rr - blk * CONV_ROWS) + (CONV_HALO - half) + tap, 1.0, 0.0).astype(BF16)

    for r0 in range(0, tm, CONV_ROWS):
        shifted = _dot(selector, ext_ref[r0:r0 + win, :])
        centre = ext_ref[CONV_HALO + r0:CONV_HALO + r0 + CONV_ROWS, :].astype(F32)
        acc = b_ref[...] + w_ref[half:half + 1, :] * centre
        for n, k in enumerate(side_taps):
            acc = acc + w_ref[k:k + 1, :] * shifted[n * CONV_ROWS:(n + 1) * CONV_ROWS, :]
        o_ref[r0:r0 + CONV_ROWS, :] = _silu(acc).astype(o_ref.dtype)


def _conv(p_main, conv_w, conv_b, tm, tc=1024):
    L = p_main.shape[0]
    hb = tm // CONV_HALO
    n_halo = L // CONV_HALO
    c0 = OFF_XBC // tc
    return pl.pallas_call(
        functools.partial(_conv_kernel, tm=tm),
        grid=(L // tm, CONV_CH // tc),
        in_specs=[pl.BlockSpec((CONV_HALO, tc), lambda i, j: (jnp.maximum(i * hb - 1, 0), c0 + j)),
                  pl.BlockSpec((tm, tc), lambda i, j: (i, c0 + j)),
                  pl.BlockSpec((CONV_HALO, tc), lambda i, j: (jnp.minimum((i + 1) * hb, n_halo - 1), c0 + j)),
                  pl.BlockSpec((CONV_W, tc), lambda i, j: (0, j)),
                  pl.BlockSpec((1, tc), lambda i, j: (0, j))],
        out_specs=pl.BlockSpec((tm, tc), lambda i, j: (i, j)),
        out_shape=jax.ShapeDtypeStruct((L, CONV_CH), BF16),
        scratch_shapes=[pltpu.VMEM((tm + 2 * CONV_HALO, tc), BF16)],
        compiler_params=_params("parallel", "parallel"),
        name="conv",
    )(p_main, p_main, p_main, conv_w, conv_b)


def _qkv_kernel(*refs, tm, rope, with_q):
    if with_q:
        q_ref, k_ref, v_ref, rowtab_ref, coltab_ref, qo_ref, ko_ref, vo_ref = refs
    else:
        k_ref, v_ref, ko_ref, vo_ref = refs
    lane = lax.broadcasted_iota(I32, (tm, LANES), 1)
    first = (lane & 31) < 16
    lo = lane < 64

    if rope:
        rows_per_tile = tm // GRID_W
        r0 = pl.program_id(0) * rows_per_tile
        cos, sin = [jnp.concatenate([rowtab_ref[f, pl.ds(r0 + b, 1), :] + coltab_ref[f]
                                     for b in range(rows_per_tile)], axis=0) for f in range(2)]

    def rot(x):
        if not rope:
            return x
        r = jnp.where(first, -pltpu.roll(x, LANES - 16, 1), pltpu.roll(x, 16, 1))
        return x * cos + r * sin

    def dup(x, o_ref, p):
        r = pltpu.roll(x, 64, 1)
        o_ref[:, (2 * p) * LANES:(2 * p + 1) * LANES] = jnp.where(lo, x, r).astype(o_ref.dtype)
        o_ref[:, (2 * p + 1) * LANES:(2 * p + 2) * LANES] = jnp.where(lo, r, x).astype(o_ref.dtype)

    if with_q:
        for c in range(ATTN_WIDTH // LANES):
            sl = slice(c * LANES, (c + 1) * LANES)
            qo_ref[:, sl] = (rot(q_ref[:, sl].astype(F32)) * (ATTN_SCALE * LOG2_E)).astype(qo_ref.dtype)
    for p in range(KV_WIDTH // LANES):
        sl = slice(p * LANES, (p + 1) * LANES)
        dup(rot(k_ref[:, sl].astype(F32)), ko_ref, p)
        dup(v_ref[:, sl].astype(F32), vo_ref, p)


def _qkv(p_main, row_tab, col_tab, tm, rope, with_q):
    L = p_main.shape[0]
    kv_specs = [pl.BlockSpec((tm, KV_WIDTH), lambda i: (i, OFF_K // KV_WIDTH)),
                pl.BlockSpec((tm, KV_WIDTH), lambda i: (i, OFF_V // KV_WIDTH))]
    kv_out = [pl.BlockSpec((tm, KV_DUP), lambda i: (i, 0)), pl.BlockSpec((tm, KV_DUP), lambda i: (i, 0))]
    kv_shape = [jax.ShapeDtypeStruct((L, KV_DUP), BF16), jax.ShapeDtypeStruct((L, KV_DUP), BF16)]
    if with_q:
        in_specs = ([pl.BlockSpec((tm, ATTN_WIDTH), lambda i: (i, OFF_Q // ATTN_WIDTH))] + kv_specs
                    + [pl.BlockSpec(row_tab.shape, lambda i: (0, 0, 0)), pl.BlockSpec(col_tab.shape, lambda i: (0, 0, 0))])
        out_specs = [pl.BlockSpec((tm, ATTN_WIDTH), lambda i: (i, 0))] + kv_out
        out_shape = [jax.ShapeDtypeStruct((L, ATTN_WIDTH), BF16)] + kv_shape
        args = (p_main, p_main, p_main, row_tab, col_tab)
    else:
        in_specs, out_specs, out_shape, args = kv_specs, kv_out, kv_shape, (p_main, p_main)
    return pl.pallas_call(
        functools.partial(_qkv_kernel, tm=tm, rope=rope, with_q=with_q),
        grid=(L // tm,),
        in_specs=in_specs, out_specs=out_specs, out_shape=out_shape,
        compiler_params=_params("parallel"),
        name="qkv_rope" if with_q else "kv_ctx",
    )(*args)


PAIRS_PER_GROUP = (SSD_HEADS // SSD_GROUPS) // 2
GROUP_W = (SSD_HEADS // SSD_GROUPS) * SSD_HEAD_DIM


def _ssd_kernel(xf_ref, xb_ref, dtf_ref, dtb_ref, s0_ref, bias_ref, alog_ref, yf_ref, yb_ref, s_ref):
    Q = SSD_CHUNK

    @pl.when(pl.program_id(0) == 0)
    def _():
        s_ref[...] = s0_ref[...]

    row = lax.broadcasted_iota(I32, (Q, Q), 0)
    col = lax.broadcasted_iota(I32, (Q, Q), 1)
    lo = col < 64
    lo_row = lax.broadcasted_iota(I32, (1, Q), 1) < 64
    neg_a = -jnp.exp(alog_ref[...])

    for d, (x_ref, dt_ref, y_ref) in enumerate(((xf_ref, dtf_ref, yf_ref), (xb_ref, dtb_ref, yb_ref))):
        tri = (row >= col) if d == 0 else (row <= col)
        off = d * SSD_HEADS
        z = dt_ref[...] + bias_ref[...]
        dt = jnp.maximum(z, 0.0) + jnp.log1p(jnp.exp(-jnp.abs(z)))
        da = dt * neg_a
        ones_tri = jnp.where(tri, 1.0, 0.0).astype(BF16)
        da_hi = da.astype(BF16)
        da_mid = (da - da_hi.astype(F32)).astype(BF16)
        da_lo = (da - da_hi.astype(F32) - da_mid.astype(F32)).astype(BF16)
        a = (_dot(ones_tri, da_hi) + (_dot(ones_tri, da_mid) + _dot(ones_tri, da_lo))) * LOG2_E
        a_t = a.T
        dt_t = dt.T
        edge = a_t[:, Q - 1:Q] if d == 0 else a_t[:, 0:1]
        w_t = (dt_t * jnp.exp2(edge - a_t)).astype(BF16)
        dec = jnp.exp2(edge)
        dt_tb = dt_t.astype(BF16)

        for g in range(SSD_GROUPS):
            bg = x_ref[:, D_INNER + g * SSD_STATE:D_INNER + (g + 1) * SSD_STATE]
            cg = x_ref[:, D_INNER + (SSD_GROUPS + g) * SSD_STATE:D_INNER + (SSD_GROUPS + g + 1) * SSD_STATE]
            cb = _dot_nt(cg, bg).astype(BF16)
            b_t = bg.astype(F32).T.astype(BF16)
            s_prev = s_ref[d, g]
            y_off = _dot(cg, s_prev.astype(BF16))
            for p in range(PAIRS_PER_GROUP):
                h0 = g * (SSD_HEADS // SSD_GROUPS) + 2 * p
                c0 = h0 * SSD_HEAD_DIM
                xpair = x_ref[:, c0:c0 + LANES]
                zero = jnp.zeros_like(xpair)
                y_diag = None
                st = None
                a_cols = []
                for t, xm in enumerate((jnp.where(lo, xpair, zero), jnp.where(lo, zero, xpair))):
                    h = off + h0 + t
                    a_col = jnp.broadcast_to(a[:, h:h + 1], (Q, Q))
                    a_cols.append(a_col)
                    seg = a_col - a_t[h:h + 1, :]
                    decay = jnp.exp2(jnp.where(tri, seg, -jnp.inf)).astype(BF16)
                    yd = _dot(decay * cb * dt_tb[h:h + 1, :], xm)
                    sd = _dot(b_t * w_t[h:h + 1, :], xm)
                    y_diag = yd if y_diag is None else y_diag + yd
                    st = sd if st is None else st + sd
                h = off + h0
                ea_pair = jnp.exp2(jnp.where(lo, a_cols[0], a_cols[1]))
                ps = slice(p * LANES, (p + 1) * LANES)
                y_ref[:, c0:c0 + LANES] = (y_diag + y_off[:, ps] * ea_pair).astype(y_ref.dtype)
                dec_pair = jnp.where(lo_row, dec[h:h + 1, :], dec[h + 1:h + 2, :])
                s_ref[d, g, :, ps] = s_prev[:, ps] * dec_pair + st


def _ssd(xc, dt, s0, bias, alog):
    L = xc.shape[0]
    nc = L // SSD_CHUNK
    sshape = (2, SSD_GROUPS, SSD_STATE, GROUP_W)
    return pl.pallas_call(
        _ssd_kernel,
        grid=(nc,),
        in_specs=[pl.BlockSpec((SSD_CHUNK, CONV_CH), lambda c: (c, 0)),
                  pl.BlockSpec((SSD_CHUNK, CONV_CH), lambda c: (nc - 1 - c, 0)),
                  pl.BlockSpec((SSD_CHUNK, LANES), lambda c: (c, 0)),
                  pl.BlockSpec((SSD_CHUNK, LANES), lambda c: (nc - 1 - c, 0)),
                  pl.BlockSpec(sshape, lambda c: (0, 0, 0, 0)),
                  pl.BlockSpec((1, LANES), lambda c: (0, 0)),
                  pl.BlockSpec((1, LANES), lambda c: (0, 0))],
        out_specs=[pl.BlockSpec((SSD_CHUNK, D_INNER), lambda c: (c, 0)),
                   pl.BlockSpec((SSD_CHUNK, D_INNER), lambda c: (nc - 1 - c, 0)),
                   pl.BlockSpec(sshape, lambda c: (0, 0, 0, 0))],
        out_shape=[jax.ShapeDtypeStruct((L, D_INNER), BF16),
                   jax.ShapeDtypeStruct((L, D_INNER), BF16),
                   jax.ShapeDtypeStruct(sshape, F32)],
        compiler_params=_params("arbitrary"),
        name="ssd",
    )(xc, xc, dt, dt, s0, bias, alog)


ATTN_BLOCKS_PER_STEP = 2


def _attn_kernel(sink_ref, q_ref, kp_ref, kc_ref, kn_ref, vp_ref, vc_ref, vn_ref, kx_ref, vx_ref, o_ref, *, s_ctx):
    i = pl.program_id(0)
    last = pl.num_programs(0) - 1
    B = ATTN_BLOCK
    nsub = ATTN_BLOCKS_PER_STEP
    span = 3 * B
    width = span + s_ctx
    t = lax.broadcasted_iota(I32, (B, width), 0)
    j = lax.broadcasted_iota(I32, (B, width), 1)
    in_window = jnp.abs(j - B - t) <= WINDOW
    lo = lax.broadcasted_iota(I32, (B, LANES), 1) < 64
    group = ATTN_Q_HEADS // ATTN_KV_HEADS

    for sb in range(nsub):
        rows = slice(sb * B, (sb + 1) * B)
        j_min = jnp.where(i > 0, 0, B) if sb == 0 else 0
        j_max = jnp.where(i < last, span, 2 * B) if sb == nsub - 1 else span
        valid = (j >= span) | (in_window & (j >= j_min) & (j < j_max))

        def window(prev_ref, cur_ref, next_ref, ctx_ref, sl):
            before = prev_ref[:, sl] if sb == 0 else cur_ref[(sb - 1) * B:sb * B, sl]
            after = next_ref[:, sl] if sb == nsub - 1 else cur_ref[(sb + 1) * B:(sb + 2) * B, sl]
            return jnp.concatenate([before, cur_ref[rows, sl], after, ctx_ref[:, sl]], axis=0)

        for kh in range(ATTN_KV_HEADS):
            sl = slice(kh * LANES, (kh + 1) * LANES)
            k_all = window(kp_ref, kc_ref, kn_ref, kx_ref, sl)
            v_all = window(vp_ref, vc_ref, vn_ref, vx_ref, sl)
            q_rows = []
            for u in range(group // 2):
                c = (group // 2) * kh + u
                qpair = q_ref[rows, c * LANES:(c + 1) * LANES]
                zero_q = jnp.zeros_like(qpair)
                q_rows += [jnp.where(lo, qpair, zero_q), jnp.where(lo, zero_q, qpair)]
            s_all = _dot_nt(jnp.concatenate(q_rows, axis=0), k_all)
            e_rows, inv = [], []
            for b in range(group):
                sink = sink_ref[group * kh + b] * LOG2_E
                s = jnp.where(valid, s_all[b * B:(b + 1) * B, :], -jnp.inf)
                m = jnp.maximum(jnp.max(s, axis=1, keepdims=True), sink)
                e = jnp.exp2(s - m)
                inv.append(1.0 / (jnp.sum(e, axis=1, keepdims=True) + jnp.exp2(sink - m)))
                e_rows.append(e.astype(BF16))
            o_all = _dot(jnp.concatenate(e_rows, axis=0), v_all)
            for u in range(group // 2):
                c = (group // 2) * kh + u
                b0 = 2 * u
                o = jnp.where(lo, o_all[b0 * B:(b0 + 1) * B, :] * inv[b0],
                              o_all[(b0 + 1) * B:(b0 + 2) * B, :] * inv[b0 + 1])
                o_ref[rows, c * LANES:(c + 1) * LANES] = o.astype(o_ref.dtype)


def _attn(sink, q, kd, vd, kx, vx):
    L = q.shape[0]
    s_ctx = kx.shape[0]
    nsub = ATTN_BLOCKS_PER_STEP
    nb = L // ATTN_BLOCK
    prev = lambda i: (jnp.maximum(nsub * i - 1, 0), 0)
    cur = lambda i: (i, 0)
    nxt = lambda i: (jnp.minimum(nsub * (i + 1), nb - 1), 0)
    edge = lambda f: pl.BlockSpec((ATTN_BLOCK, KV_DUP), f)
    body = pl.BlockSpec((nsub * ATTN_BLOCK, KV_DUP), cur)
    return pl.pallas_call(
        functools.partial(_attn_kernel, s_ctx=s_ctx),
        grid=(nb // nsub,),
        in_specs=[pl.BlockSpec(memory_space=pltpu.SMEM),
                  pl.BlockSpec((nsub * ATTN_BLOCK, ATTN_WIDTH), cur),
                  edge(prev), body, edge(nxt), edge(prev), body, edge(nxt),
                  pl.BlockSpec((s_ctx, KV_DUP), lambda i: (0, 0)),
                  pl.BlockSpec((s_ctx, KV_DUP), lambda i: (0, 0))],
        out_specs=pl.BlockSpec((nsub * ATTN_BLOCK, ATTN_WIDTH), cur),
        out_shape=jax.ShapeDtypeStruct((L, ATTN_WIDTH), BF16),
        compiler_params=_params("parallel"),
        name="attn",
    )(sink, q, kd, kd, kd, vd, vd, vd, kx, vx)


OUT_CHUNK = 256


def _out_kernel(yf_ref, yb_ref, xs_ref, z_ref, g_ref, ya_ref, x_ref, dsk_ref, nw_ref, wssd_ref, wattn_ref, wo_ref,
                bo_ref, g1_ref, ln1g_ref, ln1b_ref, sh2_ref, sc2_ref, wrh_ref, wrl_ref, x1_ref, h2_ref, aff_ref):
    br_attn = _dot(ya_ref[...], wattn_ref[...])
    sumsq = None
    br_ssd = None
    for c0 in range(0, D_INNER, OUT_CHUNK):
        cs = slice(c0, c0 + OUT_CHUNK)
        y = yf_ref[:, cs].astype(F32) + yb_ref[:, cs].astype(F32) + dsk_ref[:, cs] * xs_ref[:, cs].astype(F32)
        y = y * _silu(z_ref[:, cs].astype(F32))
        sq = jnp.sum(y * y, -1, keepdims=True)
        part = _dot((y * nw_ref[:, cs]).astype(BF16), wssd_ref[cs, :])
        sumsq = sq if sumsq is None else sumsq + sq
        br_ssd = part if br_ssd is None else br_ssd + part
    br_ssd = br_ssd * lax.rsqrt(sumsq * (1.0 / D_INNER) + LN_EPS)
    g_ssd = _sigmoid(g_ref[:, :D_MODEL].astype(F32))
    g_attn = _sigmoid(g_ref[:, D_MODEL:].astype(F32))
    m = g_ssd * br_ssd + g_attn * br_attn
    o = _dot(m.astype(BF16), wo_ref[...]) + bo_ref[...]
    x1 =_ln(DEEPNORM_ALPHA * x_ref[...] + g1_ref[...] * o) * ln1g_ref[...] + ln1b_ref[...]
    x1_ref[...] = x1
    h2 = _ln(x1) * (1.0 + sc2_ref[...]) + sh2_ref[...]
    h2_ref[...] = h2.astype(h2_ref.dtype)
    h_hi = h2.astype(BF16)
    h_lo = (h2 - h_hi.astype(F32)).astype(BF16)
    logits = _dot_nt(h_hi, wrh_ref[...]) + (_dot_nt(h_hi, wrl_ref[...]) + _dot_nt(h_lo, wrh_ref[...]))
    lane = lax.broadcasted_iota(I32, logits.shape, 1)
    logits = jnp.where(lane < N_EXPERTS, logits, -jnp.inf)
    e = jnp.exp(logits - jnp.max(logits, axis=1, keepdims=True))
    aff = e / jnp.sum(e, axis=1, keepdims=True)
    aff_ref[...] = aff.T[:N_EXPERTS, :]


def _out(yf, yb, xc, p_main, ya, x, dsk, nw, wssd, wattn, wo, bo, g1, ln1g, ln1b, sh2, sc2, wrh, wrl, tm):
    L = x.shape[0]
    row = lambda w: pl.BlockSpec((1, w), lambda i: (0, 0))
    full = lambda a: pl.BlockSpec(a.shape, lambda i: (0, 0))
    return pl.pallas_call(
        _out_kernel,
        grid=(L // tm,),
        in_specs=[pl.BlockSpec((tm, D_INNER), lambda i: (i, 0)),
                  pl.BlockSpec((tm, D_INNER), lambda i: (i, 0)),
                  pl.BlockSpec((tm, D_INNER), lambda i: (i, 0)),
                  pl.BlockSpec((tm, D_INNER), lambda i: (i, OFF_Z // D_INNER)),
                  pl.BlockSpec((tm, 2 * D_MODEL), lambda i: (i, OFF_G // (2 * D_MODEL))),
                  pl.BlockSpec((tm, ATTN_WIDTH), lambda i: (i, 0)),
                  pl.BlockSpec((tm, D_MODEL), lambda i: (i, 0)),
                  row(D_INNER), row(D_INNER), full(wssd), full(wattn), full(wo),
                  row(D_MODEL), row(D_MODEL), row(D_MODEL), row(D_MODEL), row(D_MODEL), row(D_MODEL),
                  full(wrh), full(wrl)],
        out_specs=[pl.BlockSpec((tm, D_MODEL), lambda i: (i, 0)),
                   pl.BlockSpec((tm, D_MODEL), lambda i: (i, 0)),
                   pl.BlockSpec((N_EXPERTS, tm), lambda i: (0, i))],
        out_shape=[jax.ShapeDtypeStruct((L, D_MODEL), F32),
                   jax.ShapeDtypeStruct((L, D_MODEL), BF16),
                   jax.ShapeDtypeStruct((N_EXPERTS, L), F32)],
        compiler_params=_params("parallel"),
        name="out_proj",
    )(yf, yb, xc, p_main, p_main, ya, x, dsk, nw, wssd, wattn, wo, bo, g1, ln1g, ln1b, sh2, sc2, wrh, wrl)


def _count(mask):
    c = jnp.sum(jnp.where(mask, 1.0, 0.0), axis=1, keepdims=True)
    return jnp.sum(c, axis=0, keepdims=True)


def _route_kernel(aff_ref, rank_ref, start_ref, cnt_ref, sel_ref, *, cap, nrows):
    E = N_EXPERTS
    R = nrows
    tok = (lax.broadcasted_iota(I32, (R, LANES), 0) * LANES + lax.broadcasted_iota(I32, (R, LANES), 1))
    capf = jnp.float32(cap)

    def as_f32(word):
        return pltpu.bitcast(word, F32)

    def thr_body(it, ts):
        bit = lax.shift_left(jnp.int32(1), 30 - it)
        out = []
        for e in range(E):
            cand = ts[e] | bit
            out.append(jnp.where(_count(aff_ref[e] >= as_f32(cand)) >= capf, cand, ts[e]))
        return tuple(out)

    ts = lax.fori_loop(0, 31, thr_body, tuple(jnp.zeros((1, 1), I32) for _ in range(E)))
    thr = [as_f32(t) for t in ts]
    need = [capf - _count(aff_ref[e] > thr[e]) for e in range(E)]

    nbits = max((R * LANES - 1).bit_length(), 1)

    def tie_body(it, js):
        bit = lax.shift_left(jnp.int32(1), nbits - 1 - it)
        out = []
        for e in range(E):
            cand = js[e] | bit
            f = _count((aff_ref[e] == thr[e]) & (tok < cand))
            out.append(jnp.where(f < need[e], cand, js[e]))
        return tuple(out)

    js = lax.fori_loop(0, nbits, tie_body, tuple(jnp.zeros((1, 1), I32) for _ in range(E)))
    for e in range(E):
        a = aff_ref[e]
        sel = (a > thr[e]) | ((a == thr[e]) & (tok <= js[e]))
        sel_ref[e] = jnp.where(sel, 1.0, 0.0)

    li = lax.broadcasted_iota(I32, (LANES, LANES), 0)
    lj = lax.broadcasted_iota(I32, (LANES, LANES), 1)
    upper_incl = jnp.where(li <= lj, 1.0, 0.0).astype(BF16)
    ri = lax.broadcasted_iota(I32, (R, R), 0)
    rj = lax.broadcasted_iota(I32, (R, R), 1)
    row_lower = jnp.where(ri >= rj, 1.0, 0.0).astype(BF16)
    lane_id = lax.broadcasted_iota(I32, (R, LANES), 1)

    def expert_body(e, carry):
        start_acc, cnt_acc = carry
        sel = sel_ref[e]
        w = _dot(sel.astype(BF16), upper_incl)
        tot_b = jnp.broadcast_to(jnp.sum(sel, axis=1, keepdims=True), (R, LANES))
        seg_b = jnp.floor((tot_b + (SUBLANES - 1)) * (1.0 / SUBLANES)) * SUBLANES
        cum_incl = _dot(row_lower, seg_b.astype(BF16))
        rank_ref[e] = jnp.where(sel > 0.0, w - 1.0, -1.0)
        start_acc = jnp.where(lane_id == e, cum_incl - seg_b, start_acc)
        cnt_acc = jnp.where(lane_id == e, tot_b, cnt_acc)
        return start_acc, cnt_acc

    zero = jnp.zeros((R, LANES), F32)
    start_acc, cnt_acc = lax.fori_loop(0, E, expert_body, (zero, zero))
    start_ref[...] = start_acc.astype(I32)
    cnt_ref[...] = cnt_acc.astype(I32)


def _route(aff3, cap):
    E, R, _ = aff3.shape
    return pl.pallas_call(
        functools.partial(_route_kernel, cap=cap, nrows=R),
        out_shape=[jax.ShapeDtypeStruct((E, R, LANES), F32),
                   jax.ShapeDtypeStruct((R, LANES), I32),
                   jax.ShapeDtypeStruct((R, LANES), I32)],
        scratch_shapes=[pltpu.VMEM((E, R, LANES), F32)],
        compiler_params=pltpu.CompilerParams(vmem_limit_bytes=VMEM_LIMIT_BYTES),
        name="route",
    )(aff3)


SLOTS = 32
WIN = N_EXPERTS * SLOTS
DISPATCH_TILES = 2
D_SLOTS = DISPATCH_TILES * SLOTS
D_WIN = N_EXPERTS * D_SLOTS


def _slot_onehot(tgt, values, k, slots=SLOTS):
    sub = lax.broadcasted_iota(I32, (slots, tgt.shape[1]), 0).astype(F32) + (slots * k).astype(F32)
    rows = []
    for e in range(N_EXPERTS):
        hit = jnp.broadcast_to(tgt[e:e + 1, :], sub.shape) == sub
        rows.append(jnp.where(hit, jnp.broadcast_to(values[e:e + 1, :], sub.shape), 0.0))
    return jnp.concatenate(rows, axis=0).astype(BF16)


def _windows(n, slots):
    return lax.shift_right_logical(n + (slots - 1), slots.bit_length() - 1)


def _rounds(cnt_ref, r):
    n = cnt_ref[0, r]
    for e in range(1, N_EXPERTS):
        n = jnp.maximum(n, cnt_ref[e, r])
    return _windows(n, SLOTS)


def _dispatch_kernel(start_ref, cnt_ref, h_ref, rank_ref, xs_hbm, stage_ref, ovf_ref, sem, osem):
    p = pl.program_id(0)
    last = pl.num_programs(0) - 1
    slot = lax.rem(p, 2)
    r0 = p * DISPATCH_TILES
    r_end = r0 + DISPATCH_TILES - 1

    def rows_used(e):
        return start_ref[e, r_end] - start_ref[e, r0] + cnt_ref[e, r_end]

    e_id = lax.broadcasted_iota(I32, (N_EXPERTS, LANES), 0)
    tgts = []
    for b in range(DISPATCH_TILES):
        rank = rank_ref[:, b, 0, :]
        off = jnp.zeros_like(rank)
        if b > 0:
            for e in range(N_EXPERTS):
                d = start_ref[e, r0 + b] - start_ref[e, r0]
                off = jnp.where(e_id == e, d.astype(F32), off)
        tgts.append(jnp.where(rank >= 0.0, rank + off, -1.0))
    tgt = jnp.concatenate(tgts, axis=1)
    ones = jnp.ones_like(tgt)

    def compact(k):
        return _dot(_slot_onehot(tgt, ones, k, D_SLOTS), h_ref[...])

    def window_copy(src_ref, e, k, s):
        dst = pl.multiple_of(start_ref[e, r0] + D_SLOTS * k, SUBLANES)
        return pltpu.make_async_copy(src_ref.at[pl.ds(e * D_SLOTS, D_SLOTS), :],
                                     xs_hbm.at[e, pl.ds(dst, D_SLOTS), :], s)

    def stage_wait(sl):
        pltpu.make_async_copy(stage_ref.at[sl], stage_ref.at[sl], sem.at[sl]).wait()

    stage_ref[slot] = compact(jnp.int32(0))

    @pl.when(p > 0)
    def _():
        stage_wait(1 - slot)

    for e in range(N_EXPERTS):
        window_copy(stage_ref.at[slot], e, 0, sem.at[slot]).start()

    def extra(k, _):
        ovf_ref[...] = compact(k)
        for e in range(N_EXPERTS):
            @pl.when(rows_used(e) > D_SLOTS * k)
            def _():
                window_copy(ovf_ref, e, k, osem).start()
        for e in range(N_EXPERTS):
            @pl.when(rows_used(e) > D_SLOTS * k)
            def _():
                window_copy(ovf_ref, e, k, osem).wait()
        return 0

    n_max = rows_used(0)
    for e in range(1, N_EXPERTS):
        n_max = jnp.maximum(n_max, rows_used(e))
    lax.fori_loop(1, _windows(n_max, D_SLOTS), extra, 0)

    def tail_copies(act):
        n_rows = xs_hbm.shape[1]
        for e in range(N_EXPERTS):
            off0 = start_ref[e, r0] + D_SLOTS * jnp.maximum(_windows(rows_used(e), D_SLOTS), 1)
            length = n_rows - off0
            n_full = lax.div(length, D_WIN)

            def full(i, _, e=e, off0=off0):
                dst = pl.multiple_of(off0 + i * D_WIN, SUBLANES)
                act(pltpu.make_async_copy(ovf_ref, xs_hbm.at[e, pl.ds(dst, D_WIN), :], osem))
                return 0

            lax.fori_loop(0, n_full, full, 0)
            off = off0 + n_full * D_WIN
            rem = length - n_full * D_WIN
            size = D_WIN // 2
            while size >= SUBLANES:
                @pl.when((rem & size) != 0)
                def _(e=e, off=off, size=size):
                    act(pltpu.make_async_copy(ovf_ref.at[pl.ds(0, size), :],
                                              xs_hbm.at[e, pl.ds(pl.multiple_of(off, SUBLANES), size), :], osem))
                off = off + (rem & size)
                size //= 2

    @pl.when(p == last)
    def _():
        stage_wait(slot)
        ovf_ref[...] = jnp.zeros(ovf_ref.shape, F32)
        tail_copies(lambda cp: cp.start())
        tail_copies(lambda cp: cp.wait())


def _padded_cap(cap, n_tiles, ts):
    worst = cap + (SUBLANES - 1) * n_tiles
    return -(-worst // ts) * ts


def _dispatch(start, cnt, h2, rank4, cap):
    L = h2.shape[0]
    T = DISPATCH_TILES * LANES
    return pl.pallas_call(
        _dispatch_kernel,
        grid_spec=pltpu.PrefetchScalarGridSpec(
            num_scalar_prefetch=2,
            grid=(L // T,),
            in_specs=[pl.BlockSpec((T, D_MODEL), lambda p, *_: (p, 0)),
                      pl.BlockSpec((N_EXPERTS, DISPATCH_TILES, 1, LANES), lambda p, *_: (0, p, 0, 0))],
            out_specs=pl.BlockSpec(memory_space=pl.ANY),
            scratch_shapes=[pltpu.VMEM((2, D_WIN, D_MODEL), F32),
                            pltpu.VMEM((D_WIN, D_MODEL), F32),
                            pltpu.SemaphoreType.DMA((2,)),
                            pltpu.SemaphoreType.DMA(())]),
        out_shape=jax.ShapeDtypeStruct((N_EXPERTS, cap + D_SLOTS, D_MODEL), F32),
        compiler_params=_params("arbitrary"),
        name="dispatch",
    )(start, cnt, h2, rank4)


def _expert_rows(start_ref, cnt_ref, e):
    last_tile = start_ref.shape[1] - 1
    shift = SUBLANES.bit_length() - 1
    seg = lax.shift_left(lax.shift_right_logical(cnt_ref[e, last_tile] + (SUBLANES - 1), shift), shift)
    return start_ref[e, last_tile] + seg


def _expert_kernel(start_ref, cnt_ref, xs_ref, wgu_hbm, wd_hbm, o_ref, wgu_f32, wd_f32, wgu_bf, wd_bf, sem, *, ts):
    e = pl.program_id(0)
    j = pl.program_id(1)
    n_experts = pl.num_programs(0)
    live = _expert_rows(start_ref, cnt_ref, e) - j * ts

    def weight_copies(ee, slot):
        return (pltpu.make_async_copy(wgu_hbm.at[ee], wgu_f32.at[slot], sem.at[0, slot]),
                pltpu.make_async_copy(wd_hbm.at[ee], wd_f32.at[slot], sem.at[1, slot]))

    @pl.when(j == 0)
    def _():
        slot = lax.rem(e, 2)

        @pl.when(e == 0)
        def _():
            for cp in weight_copies(e, slot):
                cp.start()

        for cp in weight_copies(e, slot):
            cp.wait()

        @pl.when(e + 1 < n_experts)
        def _():
            for cp in weight_copies(e + 1, 1 - slot):
                cp.start()

        rows = 128
        for r0 in range(0, D_MODEL, rows):
            wgu_bf[r0:r0 + rows, :] = wgu_f32[slot, r0:r0 + rows, :].astype(BF16)
        for r0 in range(0, EXPERT_FF, rows):
            wd_bf[r0:r0 + rows, :] = wd_f32[slot, r0:r0 + rows, :].astype(BF16)

    @pl.when(live > 0)
    def _():
        row = lax.broadcasted_iota(I32, (ts, D_MODEL), 0)
        xs = jnp.where(row < live, xs_ref[0], 0.0).astype(BF16)
        gu = _dot(xs, wgu_bf[...])
        act = _silu(gu[:, :EXPERT_FF]) * gu[:, EXPERT_FF:]
        o_ref[0] = _dot(act.astype(BF16), wd_bf[...])

    @pl.when(live <= 0)
    def _():
        o_ref[0] = jnp.zeros((ts, D_MODEL), F32)


def _expert(start, cnt, xs, wgu, wd, capp, ts):
    E = xs.shape[0]
    return pl.pallas_call(
        functools.partial(_expert_kernel, ts=ts),
        grid_spec=pltpu.PrefetchScalarGridSpec(
            num_scalar_prefetch=2,
            grid=(E, capp // ts),
            in_specs=[pl.BlockSpec((1, ts, D_MODEL), lambda e, j, start, cnt: (
                          e, jnp.minimum(j, jnp.maximum(_expert_rows(start, cnt, e) - 1, 0) // ts), 0)),
                      pl.BlockSpec(memory_space=pl.ANY),
                      pl.BlockSpec(memory_space=pl.ANY)],
            out_specs=pl.BlockSpec((1, ts, D_MODEL), lambda e, j, *_: (e, j, 0)),
            scratch_shapes=[pltpu.VMEM((2, D_MODEL, 2 * EXPERT_FF), F32),
                            pltpu.VMEM((2, EXPERT_FF, D_MODEL), F32),
                            pltpu.VMEM((D_MODEL, 2 * EXPERT_FF), BF16),
                            pltpu.VMEM((EXPERT_FF, D_MODEL), BF16),
                            pltpu.SemaphoreType.DMA((2, 2))]),
        out_shape=jax.ShapeDtypeStruct((E, capp, D_MODEL), F32),
        compiler_params=_params("arbitrary", "arbitrary"),
        name="expert_ffn",
    )(start, cnt, xs, wgu, wd)


def _combine_kernel(start_ref, cnt_ref, ye_hbm, x1_ref, aff_ref, rank_ref, g2_ref, lng_ref, lnb_ref,
                    o_ref, buf_ref, sem, *, cap):
    r = pl.program_id(0)
    last = pl.num_programs(0) - 1
    slot = lax.rem(r, 2)

    def win_start(e, rr, k):
        return pl.multiple_of(jnp.minimum(start_ref[e, rr] + SLOTS * k, cap - SLOTS), SUBLANES)

    def fetch(rr, k, sl):
        for e in range(N_EXPERTS):
            pltpu.make_async_copy(ye_hbm.at[e, pl.ds(win_start(e, rr, k), SLOTS), :],
                                  buf_ref.at[sl, pl.ds(e * SLOTS, SLOTS), :], sem.at[sl]).start()

    def fetch_wait(sl):
        pltpu.make_async_copy(buf_ref.at[sl], buf_ref.at[sl], sem.at[sl]).wait()

    @pl.when(r == 0)
    def _():
        fetch(r, 0, slot)

    @pl.when(r < last)
    def _():
        fetch(r + 1, 0, 1 - slot)

    rank = rank_ref[:, 0, 0, :]
    aff = aff_ref[...]
    g_hi = aff.astype(BF16).astype(F32)
    g_lo = aff - g_hi
    e_id = lax.broadcasted_iota(I32, rank.shape, 0)

    def contribution(k, sl):
        shift = jnp.zeros_like(rank)
        for e in range(N_EXPERTS):
            d = start_ref[e, r] + SLOTS * k - win_start(e, r, k)
            shift = jnp.where(e_id == e, d.astype(F32), shift)
        base = (SLOTS * k).astype(F32)
        in_round = (rank >= base) & (rank < base + SLOTS)
        tgt = jnp.where(in_round, rank + shift, -1.0)
        yb = buf_ref[sl].astype(BF16)
        tn = (((0,), (0,)), ((), ()))
        return (lax.dot_general(_slot_onehot(tgt, g_hi, k), yb, tn, preferred_element_type=F32)
                + lax.dot_general(_slot_onehot(tgt, g_lo, k), yb, tn, preferred_element_type=F32))

    fetch_wait(slot)
    moe = contribution(jnp.int32(0), slot)

    def extra(k, acc):
        fetch(r, k, slot)
        fetch_wait(slot)
        return acc + contribution(k, slot)

    moe = lax.fori_loop(1, _rounds(cnt_ref, r), extra, moe)
    y = _ln(DEEPNORM_ALPHA * x1_ref[...] + g2_ref[...] * moe)
    o_ref[...] = y * lng_ref[...] + lnb_ref[...]


def _combine(start, cnt, ye, x1, aff_t, rank4, g2, lng, lnb):
    L = x1.shape[0]
    E, cap, _ = ye.shape
    T = LANES
    row = pl.BlockSpec((1, D_MODEL), lambda r, *_: (0, 0))
    return pl.pallas_call(
        functools.partial(_combine_kernel, cap=cap),
        grid_spec=pltpu.PrefetchScalarGridSpec(
            num_scalar_prefetch=2,
            grid=(L // T,),
            in_specs=[pl.BlockSpec(memory_space=pl.ANY),
                      pl.BlockSpec((T, D_MODEL), lambda r, *_: (r, 0)),
                      pl.BlockSpec((E, T), lambda r, *_: (0, r)),
                      pl.BlockSpec((E, 1, 1, T), lambda r, *_: (0, r, 0, 0)),
                      row, row, row],
            out_specs=pl.BlockSpec((T, D_MODEL), lambda r, *_: (r, 0)),
            scratch_shapes=[pltpu.VMEM((2, WIN, D_MODEL), F32),
                            pltpu.SemaphoreType.DMA((2,))]),
        out_shape=jax.ShapeDtypeStruct((L, D_MODEL), F32),
        compiler_params=_params("arbitrary"),
        name="combine",
    )(start, cnt, ye, x1, aff_t, rank4, g2, lng, lnb)


def _rope_tables(L):
    inv_freq = ROPE_BASE ** (-jnp.arange(0, ROPE_AXIS_DIM, 2, dtype=F32) / ROPE_AXIS_DIM)

    def table(n, on_row_lanes):
        ang = jnp.arange(n, dtype=F32)[:, None] * inv_freq[None, :]
        one, zero = jnp.ones_like(ang), jnp.zeros_like(ang)
        mask = jnp.concatenate(([one, one, zero, zero] if on_row_lanes else [zero, zero, one, one]) * 2, -1)
        ang = jnp.concatenate([ang] * 8, -1)
        return jnp.stack([jnp.cos(ang) * mask, jnp.sin(ang) * mask])

    return table(L // GRID_W, True), table(GRID_W, False)


def _row_tile(L, pref):
    return pref if L % pref == 0 else L


def kernel(x, c, ctx, c_ctx, w_ada, b_ada, w_in, conv_w, conv_b, a_log, dt_bias, d_skip, ssd_norm_w, attn_sink,
           w_ssd_br, w_attn_br, w_o, b_o, ln1_g, ln1_b, w_router, w_gate_up, w_down, ln2_g, ln2_b):
    assert x.shape[0] == 1 and w_in.shape[0] == 1, "single batch element, depth 1"
    L = x.shape[1]
    S = ctx.shape[1]
    assert L % (LANES * SUBLANES) == 0 and S % SSD_CHUNK == 0
    cap = CAPACITY_FACTOR * L // N_EXPERTS
    x2, ctx2 = x[0], ctx[0]

    cc = jnp.zeros((SUBLANES, D_MODEL), F32).at[0].set(c[0]).at[1].set(c_ctx)
    mod = _ada(cc, w_ada[0], b_ada[0][None, :])
    sh1, sc1, g1, sh2, sc2, g2 = [mod[0:1, k * D_MODEL:(k + 1) * D_MODEL] for k in range(6)]
    csh1, csc1 = mod[1:2, 0:D_MODEL], mod[1:2, D_MODEL:2 * D_MODEL]

    w = w_in[0]
    o = 0
    parts = {}
    for name, width in (("z", D_INNER), ("xbc", CONV_CH), ("dt", 2 * SSD_HEADS), ("q", ATTN_WIDTH),
                        ("k", KV_WIDTH), ("v", KV_WIDTH), ("g", 2 * D_MODEL)):
        parts[name] = w[:, o:o + width]
        o += width
    w_main = jnp.concatenate([parts[n] for n in ("z", "g", "xbc", "q", "k", "v")], axis=1).astype(BF16)
    w_dt = jnp.pad(parts["dt"], ((0, 0), (0, LANES - 2 * SSD_HEADS))).astype(BF16)

    pad_heads = lambda v: jnp.pad(v.reshape(1, 2 * SSD_HEADS), ((0, 0), (0, LANES - 2 * SSD_HEADS)))
    bias_row, alog_row = pad_heads(dt_bias[0]), pad_heads(a_log[0])
    conv_b2 = conv_b[0][None, :]

    tm_c = _row_tile(S, 256)
    pc, dtc = _ln_proj(ctx2, csh1, csc1, w_main, w_dt, tm_c)
    xcc = _conv(pc, conv_w[0], conv_b2, tm_c)
    kx, vx = _qkv(pc, None, None, tm_c, rope=False, with_q=False)
    s_zero = jnp.zeros((2, SSD_GROUPS, SSD_STATE, GROUP_W), F32)
    _, _, s_ctx = _ssd(xcc, dtc, s_zero, bias_row, alog_row)

    p_main, dt = _ln_proj(x2, sh1, sc1, w_main, w_dt, _row_tile(L, 512))
    xc = _conv(p_main, conv_w[0], conv_b2, _row_tile(L, 512))
    row_tab, col_tab = _rope_tables(L)
    q, kd, vd = _qkv(p_main, row_tab, col_tab, _row_tile(L, 512), rope=True, with_q=True)
    yf, yb, _ = _ssd(xc, dt, s_ctx, bias_row, alog_row)
    ya = _attn(attn_sink[0], q, kd, vd, kx, vx)

    dsk = jnp.repeat(d_skip[0], SSD_HEAD_DIM)[None, :]
    wr = jnp.pad(w_router[0].T, ((0, LANES - N_EXPERTS), (0, 0)))
    wr_hi = wr.astype(BF16)
    wr_lo = (wr - wr_hi.astype(F32)).astype(BF16)
    x1, h2, aff_t = _out(yf, yb, xc, p_main, ya, x2, dsk, ssd_norm_w[0][None, :],
                         w_ssd_br[0].astype(BF16), w_attn_br[0].astype(BF16), w_o[0].astype(BF16),
                         b_o[0][None, :], g1, ln1_g[0][None, :], ln1_b[0][None, :], sh2, sc2, wr_hi, wr_lo,
                         _row_tile(L, 256))

    R = L // LANES
    rank, start_t, cnt_t = _route(aff_t.reshape(N_EXPERTS, R, LANES), cap)
    rank4 = rank.reshape(N_EXPERTS, R, 1, LANES)
    start = start_t[:, :N_EXPERTS].T
    cnt = cnt_t[:, :N_EXPERTS].T
    ts = min(cap, 512)
    capp = _padded_cap(cap, R, ts)
    xs = _dispatch(start, cnt, h2, rank4, capp)
    ye = _expert(start, cnt, xs, w_gate_up[0], w_down[0], capp, ts)
    out = _combine(start, cnt, ye, x1, aff_t, rank4, g2, ln2_g[0][None, :], ln2_b[0][None, :])
    return out[None]
```

```python
import functools

import numpy as np
import jax
import jax.numpy as jnp
from jax import lax
from jax.experimental import pallas as pl
from jax.experimental.pallas import tpu as pltpu

F32 = jnp.float32
BF16 = jnp.bfloat16
I32 = jnp.int32
HIGHEST = lax.Precision.HIGHEST
LOG2_E = 1.4426950408889634

LANES = 128
SUBLANES = 8
VMEM_LIMIT_BYTES = 56 * 1024 * 1024

D_MODEL = 1024
GRID_W = 64
LN_EPS = 1e-5
DEPTH = 1
DEEPNORM_ALPHA = (2.0 * DEPTH) ** 0.25
D_INNER = 2048
SSD_HEAD_DIM = 64
SSD_HEADS = 32
SSD_GROUPS = 4
SSD_STATE = 128
SSD_CHUNK = 128
CONV_W = 5
CONV_CH = D_INNER + 2 * SSD_GROUPS * SSD_STATE
ATTN_HEAD_DIM = 64
ATTN_Q_HEADS = 16
ATTN_KV_HEADS = 4
ATTN_WIDTH = 1024
KV_WIDTH = 256
WINDOW = 128
ATTN_BLOCK = 128
ATTN_SCALE = ATTN_HEAD_DIM ** -0.5
ROPE_BASE = 10000.0
ROPE_AXIS_DIM = ATTN_HEAD_DIM // 2
N_EXPERTS = 16
EXPERT_FF = 1024
CAPACITY_FACTOR = 2

OFF_Z = 0
OFF_G = OFF_Z + D_INNER
OFF_XBC = OFF_G + 2 * D_MODEL
OFF_Q = OFF_XBC + CONV_CH
OFF_K = OFF_Q + ATTN_WIDTH
OFF_V = OFF_K + KV_WIDTH
N_MAIN = OFF_V + KV_WIDTH
PROJ_TN = N_MAIN // 4
KV_DUP = 2 * KV_WIDTH


def _params(*sem):
    return pltpu.CompilerParams(dimension_semantics=sem, vmem_limit_bytes=VMEM_LIMIT_BYTES)


def _sigmoid(x):
    return 1.0 / (1.0 + jnp.exp2(x * (-LOG2_E)))


def _silu(x):
    return x * _sigmoid(x)


def _dot(a, b, **kw):
    return jnp.dot(a, b, preferred_element_type=F32, **kw)


def _dot_nt(a, b, **kw):
    return lax.dot_general(a, b, (((1,), (1,)), ((), ())), preferred_element_type=F32, **kw)


def _ln(x):
    mu = jnp.mean(x, -1, keepdims=True)
    xc = x - mu
    var = jnp.mean(xc * xc, -1, keepdims=True)
    return xc * lax.rsqrt(var + LN_EPS)


def _ada_kernel(s_ref, w_ref, b_ref, o_ref):
    s = _silu(s_ref[...])
    o_ref[...] = _dot(s, w_ref[...], precision=HIGHEST) + b_ref[...]


def _ada(cc, w_ada, b_ada):
    n = w_ada.shape[1]
    tn = 1024
    return pl.pallas_call(
        _ada_kernel,
        grid=(n // tn,),
        in_specs=[pl.BlockSpec((SUBLANES, D_MODEL), lambda j: (0, 0)),
                  pl.BlockSpec((D_MODEL, tn), lambda j: (0, j)),
                  pl.BlockSpec((1, tn), lambda j: (0, j))],
        out_specs=pl.BlockSpec((SUBLANES, tn), lambda j: (0, j)),
        out_shape=jax.ShapeDtypeStruct((SUBLANES, n), F32),
        compiler_params=_params("parallel"),
        name="ada",
    )(cc, w_ada, b_ada)


def _ln_proj_kernel(x_ref, sh_ref, sc_ref, w_ref, wdt_ref, o_ref, dt_ref, h_ref, *, tm):
    rows = 256 if tm % 256 == 0 else tm
    for r0 in range(0, tm, rows):
        rs = slice(r0, r0 + rows)
        h = _ln(x_ref[rs, :]) * (1.0 + sc_ref[...]) + sh_ref[...]
        h_ref[rs, :] = h.astype(BF16)
        dt_ref[rs, :] = _dot(h_ref[rs, :], wdt_ref[...])
        for n0 in range(0, N_MAIN, PROJ_TN):
            o_ref[rs, n0:n0 + PROJ_TN] = _dot(h_ref[rs, :], w_ref[:, n0:n0 + PROJ_TN]).astype(o_ref.dtype)


def _ln_proj(x, shift, scale, w_main, w_dt, tm):
    L = x.shape[0]
    resident = lambda a: pl.BlockSpec(a.shape, lambda i: (0, 0), pipeline_mode=pl.Buffered(1))
    return pl.pallas_call(
        functools.partial(_ln_proj_kernel, tm=tm),
        grid=(L // tm,),
        in_specs=[pl.BlockSpec((tm, D_MODEL), lambda i: (i, 0)),
                  pl.BlockSpec((1, D_MODEL), lambda i: (0, 0)),
                  pl.BlockSpec((1, D_MODEL), lambda i: (0, 0)),
                  resident(w_main), resident(w_dt)],
        out_specs=[pl.BlockSpec((tm, N_MAIN), lambda i: (i, 0)),
                   pl.BlockSpec((tm, LANES), lambda i: (i, 0))],
        out_shape=[jax.ShapeDtypeStruct((L, N_MAIN), BF16),
                   jax.ShapeDtypeStruct((L, LANES), F32)],
        scratch_shapes=[pltpu.VMEM((tm, D_MODEL), BF16)],
        compiler_params=_params("parallel"),
        name="ln_proj",
    )(x, shift, scale, w_main, w_dt)


CONV_HALO = 64
CONV_ROWS = 128


def _conv_kernel(xp_ref, x_ref, xn_ref, w_ref, b_ref, o_ref, ext_ref, *, tm):
    i = pl.program_id(0)
    last = pl.num_programs(0) - 1
    zero_halo = jnp.zeros_like(xp_ref[...])
    ext_ref[0:CONV_HALO, :] = jnp.where(i > 0, xp_ref[...], zero_halo)
    ext_ref[CONV_HALO:CONV_HALO + tm, :] = x_ref[...]
    ext_ref[CONV_HALO + tm:, :] = jnp.where(i < last, xn_ref[...], zero_halo)

    half = CONV_W // 2
    side_taps = [k for k in range(CONV_W) if k != half]
    win = CONV_ROWS + 2 * CONV_HALO
    rr = lax.broadcasted_iota(I32, (len(side_taps) * CONV_ROWS, win), 0)
    jj = lax.broadcasted_iota(I32, (len(side_taps) * CONV_ROWS, win), 1)
    blk = lax.shift_right_logical(rr, CONV_ROWS.bit_length() - 1)
    tap = jnp.where(blk < half, blk, blk + 1)
    selector = jnp.where(jj == (rr - blk * CONV_ROWS) + (CONV_HALO - half) + tap, 1.0, 0.0).astype(BF16)

    for r0 in range(0, tm, CONV_ROWS):
        shifted = _dot(selector, ext_ref[r0:r0 + win, :])
        centre = ext_ref[CONV_HALO + r0:CONV_HALO + r0 + CONV_ROWS, :].astype(F32)
        acc = b_ref[...] + w_ref[half:half + 1, :] * centre
        for n, k in enumerate(side_taps):
            acc = acc + w_ref[k:k + 1, :] * shifted[n * CONV_ROWS:(n + 1) * CONV_ROWS, :]
        o_ref[r0:r0 + CONV_ROWS, :] = _silu(acc).astype(o_ref.dtype)


def _conv(p_main, conv_w, conv_b, tm, tc=1024):
    L = p_main.shape[0]
    hb = tm // CONV_HALO
    n_halo = L // CONV_HALO
    c0 = OFF_XBC // tc
    return pl.pallas_call(
        functools.partial(_conv_kernel, tm=tm),
        grid=(L // tm, CONV_CH // tc),
        in_specs=[pl.BlockSpec((CONV_HALO, tc), lambda i, j: (jnp.maximum(i * hb - 1, 0), c0 + j)),
                  pl.BlockSpec((tm, tc), lambda i, j: (i, c0 + j)),
                  pl.BlockSpec((CONV_HALO, tc), lambda i, j: (jnp.minimum((i + 1) * hb, n_halo - 1), c0 + j)),
                  pl.BlockSpec((CONV_W, tc), lambda i, j: (0, j)),
                  pl.BlockSpec((1, tc), lambda i, j: (0, j))],
        out_specs=pl.BlockSpec((tm, tc), lambda i, j: (i, j)),
        out_shape=jax.ShapeDtypeStruct((L, CONV_CH), BF16),
        scratch_shapes=[pltpu.VMEM((tm + 2 * CONV_HALO, tc), BF16)],
        compiler_params=_params("parallel", "parallel"),
        name="conv",
    )(p_main, p_main, p_main, conv_w, conv_b)


def _qkv_kernel(*refs, tm, rope, with_q):
    if with_q:
        q_ref, k_ref, v_ref, rowtab_ref, coltab_ref, qo_ref, ko_ref, vo_ref = refs
    else:
        k_ref, v_ref, ko_ref, vo_ref = refs
    lane = lax.broadcasted_iota(I32, (tm, LANES), 1)
    first = (lane & 31) < 16
    lo = lane < 64

    if rope:
        rows_per_tile = tm // GRID_W
        r0 = pl.program_id(0) * rows_per_tile
        cos, sin = [jnp.concatenate([rowtab_ref[f, pl.ds(r0 + b, 1), :] + coltab_ref[f]
                                     for b in range(rows_per_tile)], axis=0) for f in range(2)]

    def rot(x):
        if not rope:
            return x
        r = jnp.where(first, -pltpu.roll(x, LANES - 16, 1), pltpu.roll(x, 16, 1))
        return x * cos + r * sin

    def dup(x, o_ref, p):
        r = pltpu.roll(x, 64, 1)
        o_ref[:, (2 * p) * LANES:(2 * p + 1) * LANES] = jnp.where(lo, x, r).astype(o_ref.dtype)
        o_ref[:, (2 * p + 1) * LANES:(2 * p + 2) * LANES] = jnp.where(lo, r, x).astype(o_ref.dtype)

    if with_q:
        for c in range(ATTN_WIDTH // LANES):
            sl = slice(c * LANES, (c + 1) * LANES)
            qo_ref[:, sl] = (rot(q_ref[:, sl].astype(F32)) * (ATTN_SCALE * LOG2_E)).astype(qo_ref.dtype)
    for p in range(KV_WIDTH // LANES):
        sl = slice(p * LANES, (p + 1) * LANES)
        dup(rot(k_ref[:, sl].astype(F32)), ko_ref, p)
        dup(v_ref[:, sl].astype(F32), vo_ref, p)


def _qkv(p_main, row_tab, col_tab, tm, rope, with_q):
    L = p_main.shape[0]
    kv_specs = [pl.BlockSpec((tm, KV_WIDTH), lambda i: (i, OFF_K // KV_WIDTH)),
                pl.BlockSpec((tm, KV_WIDTH), lambda i: (i, OFF_V // KV_WIDTH))]
    kv_out = [pl.BlockSpec((tm, KV_DUP), lambda i: (i, 0)), pl.BlockSpec((tm, KV_DUP), lambda i: (i, 0))]
    kv_shape = [jax.ShapeDtypeStruct((L, KV_DUP), BF16), jax.ShapeDtypeStruct((L, KV_DUP), BF16)]
    if with_q:
        in_specs = ([pl.BlockSpec((tm, ATTN_WIDTH), lambda i: (i, OFF_Q // ATTN_WIDTH))] + kv_specs
                    + [pl.BlockSpec(row_tab.shape, lambda i: (0, 0, 0)), pl.BlockSpec(col_tab.shape, lambda i: (0, 0, 0))])
        out_specs = [pl.BlockSpec((tm, ATTN_WIDTH), lambda i: (i, 0))] + kv_out
        out_shape = [jax.ShapeDtypeStruct((L, ATTN_WIDTH), BF16)] + kv_shape
        args = (p_main, p_main, p_main, row_tab, col_tab)
    else:
        in_specs, out_specs, out_shape, args = kv_specs, kv_out, kv_shape, (p_main, p_main)
    return pl.pallas_call(
        functools.partial(_qkv_kernel, tm=tm, rope=rope, with_q=with_q),
        grid=(L // tm,),
        in_specs=in_specs, out_specs=out_specs, out_shape=out_shape,
        compiler_params=_params("parallel"),
        name="qkv_rope" if with_q else "kv_ctx",
    )(*args)


PAIRS_PER_GROUP = (SSD_HEADS // SSD_GROUPS) // 2
GROUP_W = (SSD_HEADS // SSD_GROUPS) * SSD_HEAD_DIM


def _ssd_kernel(xf_ref, xb_ref, dtf_ref, dtb_ref, s0_ref, bias_ref, alog_ref, yf_ref, yb_ref, s_ref):
    Q = SSD_CHUNK

    @pl.when(pl.program_id(0) == 0)
    def _():
        s_ref[...] = s0_ref[...]

    row = lax.broadcasted_iota(I32, (Q, Q), 0)
    col = lax.broadcasted_iota(I32, (Q, Q), 1)
    lo = col < 64
    lo_row = lax.broadcasted_iota(I32, (1, Q), 1) < 64
    neg_a = -jnp.exp(alog_ref[...])

    for d, (x_ref, dt_ref, y_ref) in enumerate(((xf_ref, dtf_ref, yf_ref), (xb_ref, dtb_ref, yb_ref))):
        tri = (row >= col) if d == 0 else (row <= col)
        off = d * SSD_HEADS
        z = dt_ref[...] + bias_ref[...]
        dt = jnp.maximum(z, 0.0) + jnp.log1p(jnp.exp(-jnp.abs(z)))
        da = dt * neg_a
        ones_tri = jnp.where(tri, 1.0, 0.0).astype(BF16)
        da_hi = da.astype(BF16)
        da_mid = (da - da_hi.astype(F32)).astype(BF16)
        da_lo = (da - da_hi.astype(F32) - da_mid.astype(F32)).astype(BF16)
        a = (_dot(ones_tri, da_hi) + (_dot(ones_tri, da_mid) + _dot(ones_tri, da_lo))) * LOG2_E
        a_t = a.T
        dt_t = dt.T
        edge = a_t[:, Q - 1:Q] if d == 0 else a_t[:, 0:1]
        w_t = (dt_t * jnp.exp2(edge - a_t)).astype(BF16)
        dec = jnp.exp2(edge)
        dt_tb = dt_t.astype(BF16)

        for g in range(SSD_GROUPS):
            bg = x_ref[:, D_INNER + g * SSD_STATE:D_INNER + (g + 1) * SSD_STATE]
            cg = x_ref[:, D_INNER + (SSD_GROUPS + g) * SSD_STATE:D_INNER + (SSD_GROUPS + g + 1) * SSD_STATE]
            cb = _dot_nt(cg, bg).astype(BF16)
            b_t = bg.astype(F32).T.astype(BF16)
            s_prev = s_ref[d, g]
            y_off = _dot(cg, s_prev.astype(BF16))
            for p in range(PAIRS_PER_GROUP):
                h0 = g * (SSD_HEADS // SSD_GROUPS) + 2 * p
                c0 = h0 * SSD_HEAD_DIM
                xpair = x_ref[:, c0:c0 + LANES]
                zero = jnp.zeros_like(xpair)
                y_diag = None
                st = None
                a_cols = []
                for t, xm in enumerate((jnp.where(lo, xpair, zero), jnp.where(lo, zero, xpair))):
                    h = off + h0 + t
                    a_col = jnp.broadcast_to(a[:, h:h + 1], (Q, Q))
                    a_cols.append(a_col)
                    seg = a_col - a_t[h:h + 1, :]
                    decay = jnp.exp2(jnp.where(tri, seg, -jnp.inf)).astype(BF16)
                    yd = _dot(decay * cb * dt_tb[h:h + 1, :], xm)
                    sd = _dot(b_t * w_t[h:h + 1, :], xm)
                    y_diag = yd if y_diag is None else y_diag + yd
                    st = sd if st is None else st + sd
                h = off + h0
                ea_pair = jnp.exp2(jnp.where(lo, a_cols[0], a_cols[1]))
                ps = slice(p * LANES, (p + 1) * LANES)
                y_ref[:, c0:c0 + LANES] = (y_diag + y_off[:, ps] * ea_pair).astype(y_ref.dtype)
                dec_pair = jnp.where(lo_row, dec[h:h + 1, :], dec[h + 1:h + 2, :])
                s_ref[d, g, :, ps] = s_prev[:, ps] * dec_pair + st


def _ssd(xc, dt, s0, bias, alog):
    L = xc.shape[0]
    nc = L // SSD_CHUNK
    sshape = (2, SSD_GROUPS, SSD_STATE, GROUP_W)
    return pl.pallas_call(
        _ssd_kernel,
        grid=(nc,),
        in_specs=[pl.BlockSpec((SSD_CHUNK, CONV_CH), lambda c: (c, 0)),
                  pl.BlockSpec((SSD_CHUNK, CONV_CH), lambda c: (nc - 1 - c, 0)),
                  pl.BlockSpec((SSD_CHUNK, LANES), lambda c: (c, 0)),
                  pl.BlockSpec((SSD_CHUNK, LANES), lambda c: (nc - 1 - c, 0)),
                  pl.BlockSpec(sshape, lambda c: (0, 0, 0, 0)),
                  pl.BlockSpec((1, LANES), lambda c: (0, 0)),
                  pl.BlockSpec((1, LANES), lambda c: (0, 0))],
        out_specs=[pl.BlockSpec((SSD_CHUNK, D_INNER), lambda c: (c, 0)),
                   pl.BlockSpec((SSD_CHUNK, D_INNER), lambda c: (nc - 1 - c, 0)),
                   pl.BlockSpec(sshape, lambda c: (0, 0, 0, 0))],
        out_shape=[jax.ShapeDtypeStruct((L, D_INNER), BF16),
                   jax.ShapeDtypeStruct((L, D_INNER), BF16),
                   jax.ShapeDtypeStruct(sshape, F32)],
        compiler_params=_params("arbitrary"),
        name="ssd",
    )(xc, xc, dt, dt, s0, bias, alog)


ATTN_BLOCKS_PER_STEP = 2


def _attn_kernel(sink_ref, q_ref, kp_ref, kc_ref, kn_ref, vp_ref, vc_ref, vn_ref, kx_ref, vx_ref, o_ref, *, s_ctx):
    i = pl.program_id(0)
    last = pl.num_programs(0) - 1
    B = ATTN_BLOCK
    nsub = ATTN_BLOCKS_PER_STEP
    span = 3 * B
    width = span + s_ctx
    t = lax.broadcasted_iota(I32, (B, width), 0)
    j = lax.broadcasted_iota(I32, (B, width), 1)
    in_window = jnp.abs(j - B - t) <= WINDOW
    lo = lax.broadcasted_iota(I32, (B, LANES), 1) < 64
    group = ATTN_Q_HEADS // ATTN_KV_HEADS

    for sb in range(nsub):
        rows = slice(sb * B, (sb + 1) * B)
        j_min = jnp.where(i > 0, 0, B) if sb == 0 else 0
        j_max = jnp.where(i < last, span, 2 * B) if sb == nsub - 1 else span
        valid = (j >= span) | (in_window & (j >= j_min) & (j < j_max))

        def window(prev_ref, cur_ref, next_ref, ctx_ref, sl):
            before = prev_ref[:, sl] if sb == 0 else cur_ref[(sb - 1) * B:sb * B, sl]
            after = next_ref[:, sl] if sb == nsub - 1 else cur_ref[(sb + 1) * B:(sb + 2) * B, sl]
            return jnp.concatenate([before, cur_ref[rows, sl], after, ctx_ref[:, sl]], axis=0)

        for kh in range(ATTN_KV_HEADS):
            sl = slice(kh * LANES, (kh + 1) * LANES)
            k_all = window(kp_ref, kc_ref, kn_ref, kx_ref, sl)
            v_all = window(vp_ref, vc_ref, vn_ref, vx_ref, sl)
            q_rows = []
            for u in range(group // 2):
                c = (group // 2) * kh + u
                qpair = q_ref[rows, c * LANES:(c + 1) * LANES]
                zero_q = jnp.zeros_like(qpair)
                q_rows += [jnp.where(lo, qpair, zero_q), jnp.where(lo, zero_q, qpair)]
            s_all = _dot_nt(jnp.concatenate(q_rows, axis=0), k_all)
            e_rows, inv = [], []
            for b in range(group):
                sink = sink_ref[group * kh + b] * LOG2_E
                s = jnp.where(valid, s_all[b * B:(b + 1) * B, :], -jnp.inf)
                m = jnp.maximum(jnp.max(s, axis=1, keepdims=True), sink)
                e = jnp.exp2(s - m)
                inv.append(1.0 / (jnp.sum(e, axis=1, keepdims=True) + jnp.exp2(sink - m)))
                e_rows.append(e.astype(BF16))
            o_all = _dot(jnp.concatenate(e_rows, axis=0), v_all)
            for u in range(group // 2):
                c = (group // 2) * kh + u
                b0 = 2 * u
                o = jnp.where(lo, o_all[b0 * B:(b0 + 1) * B, :] * inv[b0],
                              o_all[(b0 + 1) * B:(b0 + 2) * B, :] * inv[b0 + 1])
                o_ref[rows, c * LANES:(c + 1) * LANES] = o.astype(o_ref.dtype)


def _attn(sink, q, kd, vd, kx, vx):
    L = q.shape[0]
    s_ctx = kx.shape[0]
    nsub = ATTN_BLOCKS_PER_STEP
    nb = L // ATTN_BLOCK
    prev = lambda i: (jnp.maximum(nsub * i - 1, 0), 0)
    cur = lambda i: (i, 0)
    nxt = lambda i: (jnp.minimum(nsub * (i + 1), nb - 1), 0)
    edge = lambda f: pl.BlockSpec((ATTN_BLOCK, KV_DUP), f)
    body = pl.BlockSpec((nsub * ATTN_BLOCK, KV_DUP), cur)
    return pl.pallas_call(
        functools.partial(_attn_kernel, s_ctx=s_ctx),
        grid=(nb // nsub,),
        in_specs=[pl.BlockSpec(memory_space=pltpu.SMEM),
                  pl.BlockSpec((nsub * ATTN_BLOCK, ATTN_WIDTH), cur),
                  edge(prev), body, edge(nxt), edge(prev), body, edge(nxt),
                  pl.BlockSpec((s_ctx, KV_DUP), lambda i: (0, 0)),
                  pl.BlockSpec((s_ctx, KV_DUP), lambda i: (0, 0))],
        out_specs=pl.BlockSpec((nsub * ATTN_BLOCK, ATTN_WIDTH), cur),
        out_shape=jax.ShapeDtypeStruct((L, ATTN_WIDTH), BF16),
        compiler_params=_params("parallel"),
        name="attn",
    )(sink, q, kd, kd, kd, vd, vd, vd, kx, vx)


OUT_CHUNK = 256


def _out_kernel(yf_ref, yb_ref, xs_ref, z_ref, g_ref, ya_ref, x_ref, dsk_ref, nw_ref, wssd_ref, wattn_ref, wo_ref,
                bo_ref, g1_ref, ln1g_ref, ln1b_ref, sh2_ref, sc2_ref, wrh_ref, wrl_ref, x1_ref, h2_ref, aff_ref):
    br_attn = _dot(ya_ref[...], wattn_ref[...])
    sumsq = None
    br_ssd = None
    for c0 in range(0, D_INNER, OUT_CHUNK):
        cs = slice(c0, c0 + OUT_CHUNK)
        y = yf_ref[:, cs].astype(F32) + yb_ref[:, cs].astype(F32) + dsk_ref[:, cs] * xs_ref[:, cs].astype(F32)
        y = y * _silu(z_ref[:, cs].astype(F32))
        sq = jnp.sum(y * y, -1, keepdims=True)
        part = _dot((y * nw_ref[:, cs]).astype(BF16), wssd_ref[cs, :])
        sumsq = sq if sumsq is None else sumsq + sq
        br_ssd = part if br_ssd is None else br_ssd + part
    br_ssd = br_ssd * lax.rsqrt(sumsq * (1.0 / D_INNER) + LN_EPS)
    g_ssd = _sigmoid(g_ref[:, :D_MODEL].astype(F32))
    g_attn = _sigmoid(g_ref[:, D_MODEL:].astype(F32))
    m = g_ssd * br_ssd + g_attn * br_attn
    o = _dot(m.astype(BF16), wo_ref[...]) + bo_ref[...]
    x1 =_ln(DEEPNORM_ALPHA * x_ref[...] + g1_ref[...] * o) * ln1g_ref[...] + ln1b_ref[...]
    x1_ref[...] = x1
    h2 = _ln(x1) * (1.0 + sc2_ref[...]) + sh2_ref[...]
    h2_ref[...] = h2.astype(h2_ref.dtype)
    h_hi = h2.astype(BF16)
    h_lo = (h2 - h_hi.astype(F32)).astype(BF16)
    logits = _dot_nt(h_hi, wrh_ref[...]) + (_dot_nt(h_hi, wrl_ref[...]) + _dot_nt(h_lo, wrh_ref[...]))
    lane = lax.broadcasted_iota(I32, logits.shape, 1)
    logits = jnp.where(lane < N_EXPERTS, logits, -jnp.inf)
    e = jnp.exp(logits - jnp.max(logits, axis=1, keepdims=True))
    aff = e / jnp.sum(e, axis=1, keepdims=True)
    aff_ref[...] = aff.T[:N_EXPERTS, :]


def _out(yf, yb, xc, p_main, ya, x, dsk, nw, wssd, wattn, wo, bo, g1, ln1g, ln1b, sh2, sc2, wrh, wrl, tm):
    L = x.shape[0]
    row = lambda w: pl.BlockSpec((1, w), lambda i: (0, 0))
    full = lambda a: pl.BlockSpec(a.shape, lambda i: (0, 0))
    return pl.pallas_call(
        _out_kernel,
        grid=(L // tm,),
        in_specs=[pl.BlockSpec((tm, D_INNER), lambda i: (i, 0)),
                  pl.BlockSpec((tm, D_INNER), lambda i: (i, 0)),
                  pl.BlockSpec((tm, D_INNER), lambda i: (i, 0)),
                  pl.BlockSpec((tm, D_INNER), lambda i: (i, OFF_Z // D_INNER)),
                  pl.BlockSpec((tm, 2 * D_MODEL), lambda i: (i, OFF_G // (2 * D_MODEL))),
                  pl.BlockSpec((tm, ATTN_WIDTH), lambda i: (i, 0)),
                  pl.BlockSpec((tm, D_MODEL), lambda i: (i, 0)),
                  row(D_INNER), row(D_INNER), full(wssd), full(wattn), full(wo),
                  row(D_MODEL), row(D_MODEL), row(D_MODEL), row(D_MODEL), row(D_MODEL), row(D_MODEL),
                  full(wrh), full(wrl)],
        out_specs=[pl.BlockSpec((tm, D_MODEL), lambda i: (i, 0)),
                   pl.BlockSpec((tm, D_MODEL), lambda i: (i, 0)),
                   pl.BlockSpec((N_EXPERTS, tm), lambda i: (0, i))],
        out_shape=[jax.ShapeDtypeStruct((L, D_MODEL), F32),
                   jax.ShapeDtypeStruct((L, D_MODEL), BF16),
                   jax.ShapeDtypeStruct((N_EXPERTS, L), F32)],
        compiler_params=_params("parallel"),
        name="out_proj",
    )(yf, yb, xc, p_main, p_main, ya, x, dsk, nw, wssd, wattn, wo, bo, g1, ln1g, ln1b, sh2, sc2, wrh, wrl)


def _count(mask):
    c = jnp.sum(jnp.where(mask, 1.0, 0.0), axis=1, keepdims=True)
    return jnp.sum(c, axis=0, keepdims=True)


def _route_kernel(aff_ref, rank_ref, start_ref, cnt_ref, sel_ref, *, cap, nrows):
    E = N_EXPERTS
    R = nrows
    tok = (lax.broadcasted_iota(I32, (R, LANES), 0) * LANES + lax.broadcasted_iota(I32, (R, LANES), 1))
    capf = jnp.float32(cap)

    def as_f32(word):
        return pltpu.bitcast(word, F32)

    def thr_body(it, ts):
        bit = lax.shift_left(jnp.int32(1), 30 - it)
        out = []
        for e in range(E):
            cand = ts[e] | bit
            out.append(jnp.where(_count(aff_ref[e] >= as_f32(cand)) >= capf, cand, ts[e]))
        return tuple(out)

    ts = lax.fori_loop(0, 31, thr_body, tuple(jnp.zeros((1, 1), I32) for _ in range(E)))
    thr = [as_f32(t) for t in ts]
    need = [capf - _count(aff_ref[e] > thr[e]) for e in range(E)]

    nbits = max((R * LANES - 1).bit_length(), 1)

    def tie_body(it, js):
        bit = lax.shift_left(jnp.int32(1), nbits - 1 - it)
        out = []
        for e in range(E):
            cand = js[e] | bit
            f = _count((aff_ref[e] == thr[e]) & (tok < cand))
            out.append(jnp.where(f < need[e], cand, js[e]))
        return tuple(out)

    js = lax.fori_loop(0, nbits, tie_body, tuple(jnp.zeros((1, 1), I32) for _ in range(E)))
    for e in range(E):
        a = aff_ref[e]
        sel = (a > thr[e]) | ((a == thr[e]) & (tok <= js[e]))
        sel_ref[e] = jnp.where(sel, 1.0, 0.0)

    li = lax.broadcasted_iota(I32, (LANES, LANES), 0)
    lj = lax.broadcasted_iota(I32, (LANES, LANES), 1)
    upper_incl = jnp.where(li <= lj, 1.0, 0.0).astype(BF16)
    ri = lax.broadcasted_iota(I32, (R, R), 0)
    rj = lax.broadcasted_iota(I32, (R, R), 1)
    row_lower = jnp.where(ri >= rj, 1.0, 0.0).astype(BF16)
    lane_id = lax.broadcasted_iota(I32, (R, LANES), 1)

    def expert_body(e, carry):
        start_acc, cnt_acc = carry
        sel = sel_ref[e]
        w = _dot(sel.astype(BF16), upper_incl)
        tot_b = jnp.broadcast_to(jnp.sum(sel, axis=1, keepdims=True), (R, LANES))
        seg_b = jnp.floor((tot_b + (SUBLANES - 1)) * (1.0 / SUBLANES)) * SUBLANES
        cum_incl = _dot(row_lower, seg_b.astype(BF16))
        rank_ref[e] = jnp.where(sel > 0.0, w - 1.0, -1.0)
        start_acc = jnp.where(lane_id == e, cum_incl - seg_b, start_acc)
        cnt_acc = jnp.where(lane_id == e, tot_b, cnt_acc)
        return start_acc, cnt_acc

    zero = jnp.zeros((R, LANES), F32)
    start_acc, cnt_acc = lax.fori_loop(0, E, expert_body, (zero, zero))
    start_ref[...] = start_acc.astype(I32)
    cnt_ref[...] = cnt_acc.astype(I32)


def _route(aff3, cap):
    E, R, _ = aff3.shape
    return pl.pallas_call(
        functools.partial(_route_kernel, cap=cap, nrows=R),
        out_shape=[jax.ShapeDtypeStruct((E, R, LANES), F32),
                   jax.ShapeDtypeStruct((R, LANES), I32),
                   jax.ShapeDtypeStruct((R, LANES), I32)],
        scratch_shapes=[pltpu.VMEM((E, R, LANES), F32)],
        compiler_params=pltpu.CompilerParams(vmem_limit_bytes=VMEM_LIMIT_BYTES),
        name="route",
    )(aff3)


SLOTS = 32
WIN = N_EXPERTS * SLOTS
DISPATCH_TILES = 2
D_SLOTS = DISPATCH_TILES * SLOTS
D_WIN = N_EXPERTS * D_SLOTS


def _slot_onehot(tgt, values, k, slots=SLOTS):
    sub = lax.broadcasted_iota(I32, (slots, tgt.shape[1]), 0).astype(F32) + (slots * k).astype(F32)
    rows = []
    for e in range(N_EXPERTS):
        hit = jnp.broadcast_to(tgt[e:e + 1, :], sub.shape) == sub
        rows.append(jnp.where(hit, jnp.broadcast_to(values[e:e + 1, :], sub.shape), 0.0))
    return jnp.concatenate(rows, axis=0).astype(BF16)


def _windows(n, slots):
    return lax.shift_right_logical(n + (slots - 1), slots.bit_length() - 1)


def _rounds(cnt_ref, r):
    n = cnt_ref[0, r]
    for e in range(1, N_EXPERTS):
        n = jnp.maximum(n, cnt_ref[e, r])
    return _windows(n, SLOTS)


def _dispatch_kernel(start_ref, cnt_ref, h_ref, rank_ref, xs_hbm, stage_ref, ovf_ref, sem, osem):
    p = pl.program_id(0)
    last = pl.num_programs(0) - 1
    slot = lax.rem(p, 2)
    r0 = p * DISPATCH_TILES
    r_end = r0 + DISPATCH_TILES - 1

    def rows_used(e):
        return start_ref[e, r_end] - start_ref[e, r0] + cnt_ref[e, r_end]

    e_id = lax.broadcasted_iota(I32, (N_EXPERTS, LANES), 0)
    tgts = []
    for b in range(DISPATCH_TILES):
        rank = rank_ref[:, b, 0, :]
        off = jnp.zeros_like(rank)
        if b > 0:
            for e in range(N_EXPERTS):
                d = start_ref[e, r0 + b] - start_ref[e, r0]
                off = jnp.where(e_id == e, d.astype(F32), off)
        tgts.append(jnp.where(rank >= 0.0, rank + off, -1.0))
    tgt = jnp.concatenate(tgts, axis=1)
    ones = jnp.ones_like(tgt)

    def compact(k):
        return _dot(_slot_onehot(tgt, ones, k, D_SLOTS), h_ref[...])

    def window_copy(src_ref, e, k, s):
        dst = pl.multiple_of(start_ref[e, r0] + D_SLOTS * k, SUBLANES)
        return pltpu.make_async_copy(src_ref.at[pl.ds(e * D_SLOTS, D_SLOTS), :],
                                     xs_hbm.at[e, pl.ds(dst, D_SLOTS), :], s)

    def stage_wait(sl):
        pltpu.make_async_copy(stage_ref.at[sl], stage_ref.at[sl], sem.at[sl]).wait()

    stage_ref[slot] = compact(jnp.int32(0))

    @pl.when(p > 0)
    def _():
        stage_wait(1 - slot)

    for e in range(N_EXPERTS):
        window_copy(stage_ref.at[slot], e, 0, sem.at[slot]).start()

    def extra(k, _):
        ovf_ref[...] = compact(k)
        for e in range(N_EXPERTS):
            @pl.when(rows_used(e) > D_SLOTS * k)
            def _():
                window_copy(ovf_ref, e, k, osem).start()
        for e in range(N_EXPERTS):
            @pl.when(rows_used(e) > D_SLOTS * k)
            def _():
                window_copy(ovf_ref, e, k, osem).wait()
        return 0

    n_max = rows_used(0)
    for e in range(1, N_EXPERTS):
        n_max = jnp.maximum(n_max, rows_used(e))
    lax.fori_loop(1, _windows(n_max, D_SLOTS), extra, 0)

    def tail_copies(act):
        n_rows = xs_hbm.shape[1]
        for e in range(N_EXPERTS):
            off0 = start_ref[e, r0] + D_SLOTS * jnp.maximum(_windows(rows_used(e), D_SLOTS), 1)
            length = n_rows - off0
            n_full = lax.div(length, D_WIN)

            def full(i, _, e=e, off0=off0):
                dst = pl.multiple_of(off0 + i * D_WIN, SUBLANES)
                act(pltpu.make_async_copy(ovf_ref, xs_hbm.at[e, pl.ds(dst, D_WIN), :], osem))
                return 0

            lax.fori_loop(0, n_full, full, 0)
            off = off0 + n_full * D_WIN
            rem = length - n_full * D_WIN
            size = D_WIN // 2
            while size >= SUBLANES:
                @pl.when((rem & size) != 0)
                def _(e=e, off=off, size=size):
                    act(pltpu.make_async_copy(ovf_ref.at[pl.ds(0, size), :],
                                              xs_hbm.at[e, pl.ds(pl.multiple_of(off, SUBLANES), size), :], osem))
                off = off + (rem & size)
                size //= 2

    @pl.when(p == last)
    def _():
        stage_wait(slot)
        ovf_ref[...] = jnp.zeros(ovf_ref.shape, F32)
        tail_copies(lambda cp: cp.start())
        tail_copies(lambda cp: cp.wait())


def _padded_cap(cap, n_tiles, ts):
    worst = cap + (SUBLANES - 1) * n_tiles
    return -(-worst // ts) * ts


def _dispatch(start, cnt, h2, rank4, cap):
    L = h2.shape[0]
    T = DISPATCH_TILES * LANES
    return pl.pallas_call(
        _dispatch_kernel,
        grid_spec=pltpu.PrefetchScalarGridSpec(
            num_scalar_prefetch=2,
            grid=(L // T,),
            in_specs=[pl.BlockSpec((T, D_MODEL), lambda p, *_: (p, 0)),
                      pl.BlockSpec((N_EXPERTS, DISPATCH_TILES, 1, LANES), lambda p, *_: (0, p, 0, 0))],
            out_specs=pl.BlockSpec(memory_space=pl.ANY),
            scratch_shapes=[pltpu.VMEM((2, D_WIN, D_MODEL), F32),
                            pltpu.VMEM((D_WIN, D_MODEL), F32),
                            pltpu.SemaphoreType.DMA((2,)),
                            pltpu.SemaphoreType.DMA(())]),
        out_shape=jax.ShapeDtypeStruct((N_EXPERTS, cap + D_SLOTS, D_MODEL), F32),
        compiler_params=_params("arbitrary"),
        name="dispatch",
    )(start, cnt, h2, rank4)


def _expert_rows(start_ref, cnt_ref, e):
    last_tile = start_ref.shape[1] - 1
    shift = SUBLANES.bit_length() - 1
    seg = lax.shift_left(lax.shift_right_logical(cnt_ref[e, last_tile] + (SUBLANES - 1), shift), shift)
    return start_ref[e, last_tile] + seg


def _expert_kernel(start_ref, cnt_ref, xs_ref, wgu_hbm, wd_hbm, o_ref, wgu_f32, wd_f32, wgu_bf, wd_bf, sem, *, ts):
    e = pl.program_id(0)
    j = pl.program_id(1)
    n_experts = pl.num_programs(0)
    live = _expert_rows(start_ref, cnt_ref, e) - j * ts

    def weight_copies(ee, slot):
        return (pltpu.make_async_copy(wgu_hbm.at[ee], wgu_f32.at[slot], sem.at[0, slot]),
                pltpu.make_async_copy(wd_hbm.at[ee], wd_f32.at[slot], sem.at[1, slot]))

    @pl.when(j == 0)
    def _():
        slot = lax.rem(e, 2)

        @pl.when(e == 0)
        def _():
            for cp in weight_copies(e, slot):
                cp.start()

        for cp in weight_copies(e, slot):
            cp.wait()

        @pl.when(e + 1 < n_experts)
        def _():
            for cp in weight_copies(e + 1, 1 - slot):
                cp.start()

        rows = 128
        for r0 in range(0, D_MODEL, rows):
            wgu_bf[r0:r0 + rows, :] = wgu_f32[slot, r0:r0 + rows, :].astype(BF16)
        for r0 in range(0, EXPERT_FF, rows):
            wd_bf[r0:r0 + rows, :] = wd_f32[slot, r0:r0 + rows, :].astype(BF16)

    @pl.when(live > 0)
    def _():
        row = lax.broadcasted_iota(I32, (ts, D_MODEL), 0)
        xs = jnp.where(row < live, xs_ref[0], 0.0).astype(BF16)
        gu = _dot(xs, wgu_bf[...])
        act = _silu(gu[:, :EXPERT_FF]) * gu[:, EXPERT_FF:]
        o_ref[0] = _dot(act.astype(BF16), wd_bf[...])

    @pl.when(live <= 0)
    def _():
        o_ref[0] = jnp.zeros((ts, D_MODEL), F32)


def _expert(start, cnt, xs, wgu, wd, capp, ts):
    E = xs.shape[0]
    return pl.pallas_call(
        functools.partial(_expert_kernel, ts=ts),
        grid_spec=pltpu.PrefetchScalarGridSpec(
            num_scalar_prefetch=2,
            grid=(E, capp // ts),
            in_specs=[pl.BlockSpec((1, ts, D_MODEL), lambda e, j, start, cnt: (
                          e, jnp.minimum(j, jnp.maximum(_expert_rows(start, cnt, e) - 1, 0) // ts), 0)),
                      pl.BlockSpec(memory_space=pl.ANY),
                      pl.BlockSpec(memory_space=pl.ANY)],
            out_specs=pl.BlockSpec((1, ts, D_MODEL), lambda e, j, *_: (e, j, 0)),
            scratch_shapes=[pltpu.VMEM((2, D_MODEL, 2 * EXPERT_FF), F32),
                            pltpu.VMEM((2, EXPERT_FF, D_MODEL), F32),
                            pltpu.VMEM((D_MODEL, 2 * EXPERT_FF), BF16),
                            pltpu.VMEM((EXPERT_FF, D_MODEL), BF16),
                            pltpu.SemaphoreType.DMA((2, 2))]),
        out_shape=jax.ShapeDtypeStruct((E, capp, D_MODEL), F32),
        compiler_params=_params("arbitrary", "arbitrary"),
        name="expert_ffn",
    )(start, cnt, xs, wgu, wd)


COMBINE_TILES = 2


def _combine_kernel(start_ref, cnt_ref, ye_hbm, x1_ref, aff_ref, rank_ref, g2_ref, lng_ref, lnb_ref,
                    o_ref, buf_ref, ext_ref, sem, esem, *, cap):
    p = pl.program_id(0)
    last = pl.num_programs(0) - 1
    slot = lax.rem(p, 2)
    T = LANES

    def win_start(e, r, k):
        return pl.multiple_of(jnp.minimum(start_ref[e, r] + SLOTS * k, cap - SLOTS), SUBLANES)

    def window_copies(dst_ref, r, k, s):
        return [pltpu.make_async_copy(ye_hbm.at[e, pl.ds(win_start(e, r, k), SLOTS), :],
                                      dst_ref.at[pl.ds(e * SLOTS, SLOTS), :], s) for e in range(N_EXPERTS)]

    def fetch(step, sl):
        for b in range(COMBINE_TILES):
            for cp in window_copies(buf_ref.at[sl, b], step * COMBINE_TILES + b, 0, sem.at[sl]):
                cp.start()

    @pl.when(p == 0)
    def _():
        fetch(p, slot)

    @pl.when(p < last)
    def _():
        fetch(p + 1, 1 - slot)

    pltpu.make_async_copy(buf_ref.at[slot], buf_ref.at[slot], sem.at[slot]).wait()

    for b in range(COMBINE_TILES):
        r = p * COMBINE_TILES + b
        rank = rank_ref[:, b, 0, :]
        aff = aff_ref[:, b * T:(b + 1) * T]
        g_hi = aff.astype(BF16).astype(F32)
        g_lo = aff - g_hi
        e_id = lax.broadcasted_iota(I32, rank.shape, 0)

        def contribution(k, src_ref, r=r, rank=rank, g_hi=g_hi, g_lo=g_lo, e_id=e_id):
            shift = jnp.zeros_like(rank)
            for e in range(N_EXPERTS):
                d = start_ref[e, r] + SLOTS * k - win_start(e, r, k)
                shift = jnp.where(e_id == e, d.astype(F32), shift)
            base = (SLOTS * k).astype(F32)
            in_round = (rank >= base) & (rank < base + SLOTS)
            tgt = jnp.where(in_round, rank + shift, -1.0)
            yb = src_ref[...].astype(BF16)
            tn = (((0,), (0,)), ((), ()))
            return (lax.dot_general(_slot_onehot(tgt, g_hi, k), yb, tn, preferred_element_type=F32)
                    + lax.dot_general(_slot_onehot(tgt, g_lo, k), yb, tn, preferred_element_type=F32))

        def extra(k, acc, r=r, contribution=contribution):
            for cp in window_copies(ext_ref, r, k, esem):
                cp.start()
            pltpu.make_async_copy(ext_ref, ext_ref, esem).wait()
            return acc + contribution(k, ext_ref)

        moe = lax.fori_loop(1, _rounds(cnt_ref, r), extra, contribution(jnp.int32(0), buf_ref.at[slot, b]))
        rows = slice(b * T, (b + 1) * T)
        y = _ln(DEEPNORM_ALPHA * x1_ref[rows, :] + g2_ref[...] * moe)
        o_ref[rows, :] = y * lng_ref[...] + lnb_ref[...]


def _combine(start, cnt, ye, x1, aff_t, rank4, g2, lng, lnb):
    L = x1.shape[0]
    E, cap, _ = ye.shape
    T = COMBINE_TILES * LANES
    row = pl.BlockSpec((1, D_MODEL), lambda p, *_: (0, 0))
    return pl.pallas_call(
        functools.partial(_combine_kernel, cap=cap),
        grid_spec=pltpu.PrefetchScalarGridSpec(
            num_scalar_prefetch=2,
            grid=(L // T,),
            in_specs=[pl.BlockSpec(memory_space=pl.ANY),
                      pl.BlockSpec((T, D_MODEL), lambda p, *_: (p, 0)),
                      pl.BlockSpec((E, T), lambda p, *_: (0, p)),
                      pl.BlockSpec((E, COMBINE_TILES, 1, LANES), lambda p, *_: (0, p, 0, 0)),
                      row, row, row],
            out_specs=pl.BlockSpec((T, D_MODEL), lambda p, *_: (p, 0)),
            scratch_shapes=[pltpu.VMEM((2, COMBINE_TILES, WIN, D_MODEL), F32),
                            pltpu.VMEM((WIN, D_MODEL), F32),
                            pltpu.SemaphoreType.DMA((2,)),
                            pltpu.SemaphoreType.DMA(())]),
        out_shape=jax.ShapeDtypeStruct((L, D_MODEL), F32),
        compiler_params=_params("arbitrary"),
        name="combine",
    )(start, cnt, ye, x1, aff_t, rank4, g2, lng, lnb)


def _rope_tables(L):
    inv_freq = ROPE_BASE ** (-jnp.arange(0, ROPE_AXIS_DIM, 2, dtype=F32) / ROPE_AXIS_DIM)

    def table(n, on_row_lanes):
        ang = jnp.arange(n, dtype=F32)[:, None] * inv_freq[None, :]
        one, zero = jnp.ones_like(ang), jnp.zeros_like(ang)
        mask = jnp.concatenate(([one, one, zero, zero] if on_row_lanes else [zero, zero, one, one]) * 2, -1)
        ang = jnp.concatenate([ang] * 8, -1)
        return jnp.stack([jnp.cos(ang) * mask, jnp.sin(ang) * mask])

    return table(L // GRID_W, True), table(GRID_W, False)


def _row_tile(L, pref):
    return pref if L % pref == 0 else L


def kernel(x, c, ctx, c_ctx, w_ada, b_ada, w_in, conv_w, conv_b, a_log, dt_bias, d_skip, ssd_norm_w, attn_sink,
           w_ssd_br, w_attn_br, w_o, b_o, ln1_g, ln1_b, w_router, w_gate_up, w_down, ln2_g, ln2_b):
    assert x.shape[0] == 1 and w_in.shape[0] == 1, "single batch element, depth 1"
    L = x.shape[1]
    S = ctx.shape[1]
    assert L % (LANES * SUBLANES) == 0 and S % SSD_CHUNK == 0
    cap = CAPACITY_FACTOR * L // N_EXPERTS
    x2, ctx2 = x[0], ctx[0]

    cc = jnp.zeros((SUBLANES, D_MODEL), F32).at[0].set(c[0]).at[1].set(c_ctx)
    mod = _ada(cc, w_ada[0], b_ada[0][None, :])
    sh1, sc1, g1, sh2, sc2, g2 = [mod[0:1, k * D_MODEL:(k + 1) * D_MODEL] for k in range(6)]
    csh1, csc1 = mod[1:2, 0:D_MODEL], mod[1:2, D_MODEL:2 * D_MODEL]

    w = w_in[0]
    o = 0
    parts = {}
    for name, width in (("z", D_INNER), ("xbc", CONV_CH), ("dt", 2 * SSD_HEADS), ("q", ATTN_WIDTH),
                        ("k", KV_WIDTH), ("v", KV_WIDTH), ("g", 2 * D_MODEL)):
        parts[name] = w[:, o:o + width]
        o += width
    w_main = jnp.concatenate([parts[n] for n in ("z", "g", "xbc", "q", "k", "v")], axis=1).astype(BF16)
    w_dt = jnp.pad(parts["dt"], ((0, 0), (0, LANES - 2 * SSD_HEADS))).astype(BF16)

    pad_heads = lambda v: jnp.pad(v.reshape(1, 2 * SSD_HEADS), ((0, 0), (0, LANES - 2 * SSD_HEADS)))
    bias_row, alog_row = pad_heads(dt_bias[0]), pad_heads(a_log[0])
    conv_b2 = conv_b[0][None, :]

    tm_c = _row_tile(S, 256)
    pc, dtc = _ln_proj(ctx2, csh1, csc1, w_main, w_dt, tm_c)
    xcc = _conv(pc, conv_w[0], conv_b2, tm_c)
    kx, vx = _qkv(pc, None, None, tm_c, rope=False, with_q=False)
    s_zero = jnp.zeros((2, SSD_GROUPS, SSD_STATE, GROUP_W), F32)
    _, _, s_ctx = _ssd(xcc, dtc, s_zero, bias_row, alog_row)

    p_main, dt = _ln_proj(x2, sh1, sc1, w_main, w_dt, _row_tile(L, 512))
    xc = _conv(p_main, conv_w[0], conv_b2, _row_tile(L, 1024))
    row_tab, col_tab = _rope_tables(L)
    q, kd, vd = _qkv(p_main, row_tab, col_tab, _row_tile(L, 512), rope=True, with_q=True)
    yf, yb, _ = _ssd(xc, dt, s_ctx, bias_row, alog_row)
    ya = _attn(attn_sink[0], q, kd, vd, kx, vx)

    dsk = jnp.repeat(d_skip[0], SSD_HEAD_DIM)[None, :]
    wr = jnp.pad(w_router[0].T, ((0, LANES - N_EXPERTS), (0, 0)))
    wr_hi = wr.astype(BF16)
    wr_lo = (wr - wr_hi.astype(F32)).astype(BF16)
    x1, h2, aff_t = _out(yf, yb, xc, p_main, ya, x2, dsk, ssd_norm_w[0][None, :],
                         w_ssd_br[0].astype(BF16), w_attn_br[0].astype(BF16), w_o[0].astype(BF16),
                         b_o[0][None, :], g1, ln1_g[0][None, :], ln1_b[0][None, :], sh2, sc2, wr_hi, wr_lo,
                         _row_tile(L, 256))

    R = L // LANES
    rank, start_t, cnt_t = _route(aff_t.reshape(N_EXPERTS, R, LANES), cap)
    rank4 = rank.reshape(N_EXPERTS, R, 1, LANES)
    start = start_t[:, :N_EXPERTS].T
    cnt = cnt_t[:, :N_EXPERTS].T
    ts = min(cap, 512)
    capp = _padded_cap(cap, R, ts)
    xs = _dispatch(start, cnt, h2, rank4, capp)
    ye = _expert(start, cnt, xs, w_gate_up[0], w_down[0], capp, ts)
    out = _combine(start, cnt, ye, x1, aff_t, rank4, g2, ln2_g[0][None, :], ln2_b[0][None, :])
    return out[None]
```

```python
import functools

import numpy as np
import jax
import jax.numpy as jnp
from jax import lax
from jax.experimental import pallas as pl
from jax.experimental.pallas import tpu as pltpu

F32 = jnp.float32
BF16 = jnp.bfloat16
I32 = jnp.int32
HIGHEST = lax.Precision.HIGHEST
LOG2_E = 1.4426950408889634

LANES = 128
SUBLANES = 8
VMEM_LIMIT_BYTES = 56 * 1024 * 1024

D_MODEL = 1024
GRID_W = 64
LN_EPS = 1e-5
DEPTH = 1
DEEPNORM_ALPHA = (2.0 * DEPTH) ** 0.25
D_INNER = 2048
SSD_HEAD_DIM = 64
SSD_HEADS = 32
SSD_GROUPS = 4
SSD_STATE = 128
SSD_CHUNK = 128
CONV_W = 5
CONV_CH = D_INNER + 2 * SSD_GROUPS * SSD_STATE
ATTN_HEAD_DIM = 64
ATTN_Q_HEADS = 16
ATTN_KV_HEADS = 4
ATTN_WIDTH = 1024
KV_WIDTH = 256
WINDOW = 128
ATTN_BLOCK = 128
ATTN_SCALE = ATTN_HEAD_DIM ** -0.5
ROPE_BASE = 10000.0
ROPE_AXIS_DIM = ATTN_HEAD_DIM // 2
N_EXPERTS = 16
EXPERT_FF = 1024
CAPACITY_FACTOR = 2

OFF_Z = 0
OFF_G = OFF_Z + D_INNER
OFF_XBC = OFF_G + 2 * D_MODEL
OFF_Q = OFF_XBC + CONV_CH
OFF_K = OFF_Q + ATTN_WIDTH
OFF_V = OFF_K + KV_WIDTH
N_MAIN = OFF_V + KV_WIDTH
PROJ_TN = N_MAIN // 4
KV_DUP = 2 * KV_WIDTH


def _params(*sem):
    return pltpu.CompilerParams(dimension_semantics=sem, vmem_limit_bytes=VMEM_LIMIT_BYTES)


def _sigmoid(x):
    return 1.0 / (1.0 + jnp.exp2(x * (-LOG2_E)))


def _silu(x):
    return x * _sigmoid(x)


def _dot(a, b, **kw):
    return jnp.dot(a, b, preferred_element_type=F32, **kw)


def _dot_nt(a, b, **kw):
    return lax.dot_general(a, b, (((1,), (1,)), ((), ())), preferred_element_type=F32, **kw)


def _ln(x):
    mu = jnp.mean(x, -1, keepdims=True)
    xc = x - mu
    var = jnp.mean(xc * xc, -1, keepdims=True)
    return xc * lax.rsqrt(var + LN_EPS)


def _ada_kernel(s_ref, w_ref, b_ref, o_ref):
    s = _silu(s_ref[...])
    o_ref[...] = _dot(s, w_ref[...], precision=HIGHEST) + b_ref[...]


def _ada(cc, w_ada, b_ada):
    n = w_ada.shape[1]
    tn = 1024
    return pl.pallas_call(
        _ada_kernel,
        grid=(n // tn,),
        in_specs=[pl.BlockSpec((SUBLANES, D_MODEL), lambda j: (0, 0)),
                  pl.BlockSpec((D_MODEL, tn), lambda j: (0, j)),
                  pl.BlockSpec((1, tn), lambda j: (0, j))],
        out_specs=pl.BlockSpec((SUBLANES, tn), lambda j: (0, j)),
        out_shape=jax.ShapeDtypeStruct((SUBLANES, n), F32),
        compiler_params=_params("parallel"),
        name="ada",
    )(cc, w_ada, b_ada)


def _ln_proj_kernel(x_ref, sh_ref, sc_ref, w_ref, wdt_ref, o_ref, dt_ref, h_ref, *, tm):
    rows = 256 if tm % 256 == 0 else tm
    for r0 in range(0, tm, rows):
        rs = slice(r0, r0 + rows)
        h = _ln(x_ref[rs, :]) * (1.0 + sc_ref[...]) + sh_ref[...]
        h_ref[rs, :] = h.astype(BF16)
        dt_ref[rs, :] = _dot(h_ref[rs, :], wdt_ref[...])
        for n0 in range(0, N_MAIN, PROJ_TN):
            o_ref[rs, n0:n0 + PROJ_TN] = _dot(h_ref[rs, :], w_ref[:, n0:n0 + PROJ_TN]).astype(o_ref.dtype)


def _ln_proj(x, shift, scale, w_main, w_dt, tm):
    L = x.shape[0]
    resident = lambda a: pl.BlockSpec(a.shape, lambda i: (0, 0), pipeline_mode=pl.Buffered(1))
    return pl.pallas_call(
        functools.partial(_ln_proj_kernel, tm=tm),
        grid=(L // tm,),
        in_specs=[pl.BlockSpec((tm, D_MODEL), lambda i: (i, 0)),
                  pl.BlockSpec((1, D_MODEL), lambda i: (0, 0)),
                  pl.BlockSpec((1, D_MODEL), lambda i: (0, 0)),
                  resident(w_main), resident(w_dt)],
        out_specs=[pl.BlockSpec((tm, N_MAIN), lambda i: (i, 0)),
                   pl.BlockSpec((tm, LANES), lambda i: (i, 0))],
        out_shape=[jax.ShapeDtypeStruct((L, N_MAIN), BF16),
                   jax.ShapeDtypeStruct((L, LANES), F32)],
        scratch_shapes=[pltpu.VMEM((tm, D_MODEL), BF16)],
        compiler_params=_params("parallel"),
        name="ln_proj",
    )(x, shift, scale, w_main, w_dt)


CONV_HALO = 64
CONV_ROWS = 128


def _conv_kernel(xp_ref, x_ref, xn_ref, w_ref, b_ref, o_ref, ext_ref, *, tm):
    i = pl.program_id(0)
    last = pl.num_programs(0) - 1
    zero_halo = jnp.zeros_like(xp_ref[...])
    ext_ref[0:CONV_HALO, :] = jnp.where(i > 0, xp_ref[...], zero_halo)
    ext_ref[CONV_HALO:CONV_HALO + tm, :] = x_ref[...]
    ext_ref[CONV_HALO + tm:, :] = jnp.where(i < last, xn_ref[...], zero_halo)

    half = CONV_W // 2
    side_taps = [k for k in range(CONV_W) if k != half]
    win = CONV_ROWS + 2 * CONV_HALO
    rr = lax.broadcasted_iota(I32, (len(side_taps) * CONV_ROWS, win), 0)
    jj = lax.broadcasted_iota(I32, (len(side_taps) * CONV_ROWS, win), 1)
    blk = lax.shift_right_logical(rr, CONV_ROWS.bit_length() - 1)
    tap = jnp.where(blk < half, blk, blk + 1)
    selector = jnp.where(jj == (rr - blk * CONV_ROWS) + (CONV_HALO - half) + tap, 1.0, 0.0).astype(BF16)

    for r0 in range(0, tm, CONV_ROWS):
        shifted = _dot(selector, ext_ref[r0:r0 + win, :])
        centre = ext_ref[CONV_HALO + r0:CONV_HALO + r0 + CONV_ROWS, :].astype(F32)
        acc = b_ref[...] + w_ref[half:half + 1, :] * centre
        for n, k in enumerate(side_taps):
            acc = acc + w_ref[k:k + 1, :] * shifted[n * CONV_ROWS:(n + 1) * CONV_ROWS, :]
        o_ref[r0:r0 + CONV_ROWS, :] = _silu(acc).astype(o_ref.dtype)


def _conv(p_main, conv_w, conv_b, tm, tc=1024):
    L = p_main.shape[0]
    hb = tm // CONV_HALO
    n_halo = L // CONV_HALO
    c0 = OFF_XBC // tc
    return pl.pallas_call(
        functools.partial(_conv_kernel, tm=tm),
        grid=(L // tm, CONV_CH // tc),
        in_specs=[pl.BlockSpec((CONV_HALO, tc), lambda i, j: (jnp.maximum(i * hb - 1, 0), c0 + j)),
                  pl.BlockSpec((tm, tc), lambda i, j: (i, c0 + j)),
                  pl.BlockSpec((CONV_HALO, tc), lambda i, j: (jnp.minimum((i + 1) * hb, n_halo - 1), c0 + j)),
                  pl.BlockSpec((CONV_W, tc), lambda i, j: (0, j)),
                  pl.BlockSpec((1, tc), lambda i, j: (0, j))],
        out_specs=pl.BlockSpec((tm, tc), lambda i, j: (i, j)),
        out_shape=jax.ShapeDtypeStruct((L, CONV_CH), BF16),
        scratch_shapes=[pltpu.VMEM((tm + 2 * CONV_HALO, tc), BF16)],
        compiler_params=_params("parallel", "parallel"),
        name="conv",
    )(p_main, p_main, p_main, conv_w, conv_b)


def _qkv_kernel(*refs, tm, rope, with_q):
    if with_q:
        q_ref, k_ref, v_ref, rowtab_ref, coltab_ref, qo_ref, ko_ref, vo_ref = refs
    else:
        k_ref, v_ref, ko_ref, vo_ref = refs
    lane = lax.broadcasted_iota(I32, (tm, LANES), 1)
    first = (lane & 31) < 16
    lo = lane < 64

    if rope:
        rows_per_tile = tm // GRID_W
        r0 = pl.program_id(0) * rows_per_tile
        cos, sin = [jnp.concatenate([rowtab_ref[f, pl.ds(r0 + b, 1), :] + coltab_ref[f]
                                     for b in range(rows_per_tile)], axis=0) for f in range(2)]

    def rot(x):
        if not rope:
            return x
        r = jnp.where(first, -pltpu.roll(x, LANES - 16, 1), pltpu.roll(x, 16, 1))
        return x * cos + r * sin

    def dup(x, o_ref, p):
        r = pltpu.roll(x, 64, 1)
        o_ref[:, (2 * p) * LANES:(2 * p + 1) * LANES] = jnp.where(lo, x, r).astype(o_ref.dtype)
        o_ref[:, (2 * p + 1) * LANES:(2 * p + 2) * LANES] = jnp.where(lo, r, x).astype(o_ref.dtype)

    if with_q:
        for c in range(ATTN_WIDTH // LANES):
            sl = slice(c * LANES, (c + 1) * LANES)
            qo_ref[:, sl] = (rot(q_ref[:, sl].astype(F32)) * (ATTN_SCALE * LOG2_E)).astype(qo_ref.dtype)
    for p in range(KV_WIDTH // LANES):
        sl = slice(p * LANES, (p + 1) * LANES)
        dup(rot(k_ref[:, sl].astype(F32)), ko_ref, p)
        dup(v_ref[:, sl].astype(F32), vo_ref, p)


def _qkv(p_main, row_tab, col_tab, tm, rope, with_q):
    L = p_main.shape[0]
    kv_specs = [pl.BlockSpec((tm, KV_WIDTH), lambda i: (i, OFF_K // KV_WIDTH)),
                pl.BlockSpec((tm, KV_WIDTH), lambda i: (i, OFF_V // KV_WIDTH))]
    kv_out = [pl.BlockSpec((tm, KV_DUP), lambda i: (i, 0)), pl.BlockSpec((tm, KV_DUP), lambda i: (i, 0))]
    kv_shape = [jax.ShapeDtypeStruct((L, KV_DUP), BF16), jax.ShapeDtypeStruct((L, KV_DUP), BF16)]
    if with_q:
        in_specs = ([pl.BlockSpec((tm, ATTN_WIDTH), lambda i: (i, OFF_Q // ATTN_WIDTH))] + kv_specs
                    + [pl.BlockSpec(row_tab.shape, lambda i: (0, 0, 0)), pl.BlockSpec(col_tab.shape, lambda i: (0, 0, 0))])
        out_specs = [pl.BlockSpec((tm, ATTN_WIDTH), lambda i: (i, 0))] + kv_out
        out_shape = [jax.ShapeDtypeStruct((L, ATTN_WIDTH), BF16)] + kv_shape
        args = (p_main, p_main, p_main, row_tab, col_tab)
    else:
        in_specs, out_specs, out_shape, args = kv_specs, kv_out, kv_shape, (p_main, p_main)
    return pl.pallas_call(
        functools.partial(_qkv_kernel, tm=tm, rope=rope, with_q=with_q),
        grid=(L // tm,),
        in_specs=in_specs, out_specs=out_specs, out_shape=out_shape,
        compiler_params=_params("parallel"),
        name="qkv_rope" if with_q else "kv_ctx",
    )(*args)


PAIRS_PER_GROUP = (SSD_HEADS // SSD_GROUPS) // 2
GROUP_W = (SSD_HEADS // SSD_GROUPS) * SSD_HEAD_DIM
SSD_CHUNKS_PER_STEP = 2


def _ssd_kernel(xf_ref, xb_ref, dtf_ref, dtb_ref, s0_ref, bias_ref, alog_ref, yf_ref, yb_ref, s_ref):
    Q = SSD_CHUNK

    @pl.when(pl.program_id(0) == 0)
    def _():
        s_ref[...] = s0_ref[...]

    row = lax.broadcasted_iota(I32, (Q, Q), 0)
    col = lax.broadcasted_iota(I32, (Q, Q), 1)
    lo = col < 64
    lo_row = lax.broadcasted_iota(I32, (1, Q), 1) < 64
    neg_a = -jnp.exp(alog_ref[...])

    work = []
    for sub in range(SSD_CHUNKS_PER_STEP):
        work.append((0, xf_ref, dtf_ref, yf_ref, slice(sub * Q, (sub + 1) * Q)))
        back = SSD_CHUNKS_PER_STEP - 1 - sub
        work.append((1, xb_ref, dtb_ref, yb_ref, slice(back * Q, (back + 1) * Q)))

    for d, x_ref, dt_ref, y_ref, rs in work:
        tri = (row >= col) if d == 0 else (row <= col)
        off = d * SSD_HEADS
        z = dt_ref[rs, :] + bias_ref[...]
        dt = jnp.maximum(z, 0.0) + jnp.log1p(jnp.exp(-jnp.abs(z)))
        da = dt * neg_a
        ones_tri = jnp.where(tri, 1.0, 0.0).astype(BF16)
        da_hi = da.astype(BF16)
        da_mid = (da - da_hi.astype(F32)).astype(BF16)
        da_lo = (da - da_hi.astype(F32) - da_mid.astype(F32)).astype(BF16)
        a = (_dot(ones_tri, da_hi) + (_dot(ones_tri, da_mid) + _dot(ones_tri, da_lo))) * LOG2_E
        a_t = a.T
        dt_t = dt.T
        edge = a_t[:, Q - 1:Q] if d == 0 else a_t[:, 0:1]
        w_t = (dt_t * jnp.exp2(edge - a_t)).astype(BF16)
        dec = jnp.exp2(edge)
        dt_tb = dt_t.astype(BF16)

        for g in range(SSD_GROUPS):
            bg = x_ref[rs, D_INNER + g * SSD_STATE:D_INNER + (g + 1) * SSD_STATE]
            cg = x_ref[rs, D_INNER + (SSD_GROUPS + g) * SSD_STATE:D_INNER + (SSD_GROUPS + g + 1) * SSD_STATE]
            cb = _dot_nt(cg, bg).astype(BF16)
            b_t = bg.astype(F32).T.astype(BF16)
            s_prev = s_ref[d, g]
            y_off = _dot(cg, s_prev.astype(BF16))
            for p in range(PAIRS_PER_GROUP):
                h0 = g * (SSD_HEADS // SSD_GROUPS) + 2 * p
                c0 = h0 * SSD_HEAD_DIM
                xpair = x_ref[rs, c0:c0 + LANES]
                zero = jnp.zeros_like(xpair)
                y_diag = None
                st = None
                a_cols = []
                for t, xm in enumerate((jnp.where(lo, xpair, zero), jnp.where(lo, zero, xpair))):
                    h = off + h0 + t
                    a_col = jnp.broadcast_to(a[:, h:h + 1], (Q, Q))
                    a_cols.append(a_col)
                    seg = a_col - a_t[h:h + 1, :]
                    decay = jnp.exp2(jnp.where(tri, seg, -jnp.inf)).astype(BF16)
                    yd = _dot(decay * cb * dt_tb[h:h + 1, :], xm)
                    sd = _dot(b_t * w_t[h:h + 1, :], xm)
                    y_diag = yd if y_diag is None else y_diag + yd
                    st = sd if st is None else st + sd
                h = off + h0
                ea_pair = jnp.exp2(jnp.where(lo, a_cols[0], a_cols[1]))
                ps = slice(p * LANES, (p + 1) * LANES)
                y_ref[rs, c0:c0 + LANES] = (y_diag + y_off[:, ps] * ea_pair).astype(y_ref.dtype)
                dec_pair = jnp.where(lo_row, dec[h:h + 1, :], dec[h + 1:h + 2, :])
                s_ref[d, g, :, ps] = s_prev[:, ps] * dec_pair + st


def _ssd(xc, dt, s0, bias, alog):
    L = xc.shape[0]
    rows = SSD_CHUNKS_PER_STEP * SSD_CHUNK
    nc = L // rows
    sshape = (2, SSD_GROUPS, SSD_STATE, GROUP_W)
    return pl.pallas_call(
        _ssd_kernel,
        grid=(nc,),
        in_specs=[pl.BlockSpec((rows, CONV_CH), lambda c: (c, 0)),
                  pl.BlockSpec((rows, CONV_CH), lambda c: (nc - 1 - c, 0)),
                  pl.BlockSpec((rows, LANES), lambda c: (c, 0)),
                  pl.BlockSpec((rows, LANES), lambda c: (nc - 1 - c, 0)),
                  pl.BlockSpec(sshape, lambda c: (0, 0, 0, 0)),
                  pl.BlockSpec((1, LANES), lambda c: (0, 0)),
                  pl.BlockSpec((1, LANES), lambda c: (0, 0))],
        out_specs=[pl.BlockSpec((rows, D_INNER), lambda c: (c, 0)),
                   pl.BlockSpec((rows, D_INNER), lambda c: (nc - 1 - c, 0)),
                   pl.BlockSpec(sshape, lambda c: (0, 0, 0, 0))],
        out_shape=[jax.ShapeDtypeStruct((L, D_INNER), BF16),
                   jax.ShapeDtypeStruct((L, D_INNER), BF16),
                   jax.ShapeDtypeStruct(sshape, F32)],
        compiler_params=_params("arbitrary"),
        name="ssd",
    )(xc, xc, dt, dt, s0, bias, alog)


ATTN_BLOCKS_PER_STEP = 4


def _attn_kernel(sink_ref, q_ref, kp_ref, kc_ref, kn_ref, vp_ref, vc_ref, vn_ref, kx_ref, vx_ref, o_ref, *, s_ctx):
    i = pl.program_id(0)
    last = pl.num_programs(0) - 1
    B = ATTN_BLOCK
    nsub = ATTN_BLOCKS_PER_STEP
    span = 3 * B
    width = span + s_ctx
    t = lax.broadcasted_iota(I32, (B, width), 0)
    j = lax.broadcasted_iota(I32, (B, width), 1)
    in_window = jnp.abs(j - B - t) <= WINDOW
    lo = lax.broadcasted_iota(I32, (B, LANES), 1) < 64
    group = ATTN_Q_HEADS // ATTN_KV_HEADS

    for sb in range(nsub):
        rows = slice(sb * B, (sb + 1) * B)
        j_min = jnp.where(i > 0, 0, B) if sb == 0 else 0
        j_max = jnp.where(i < last, span, 2 * B) if sb == nsub - 1 else span
        valid = (j >= span) | (in_window & (j >= j_min) & (j < j_max))

        def window(prev_ref, cur_ref, next_ref, ctx_ref, sl):
            before = prev_ref[:, sl] if sb == 0 else cur_ref[(sb - 1) * B:sb * B, sl]
            after = next_ref[:, sl] if sb == nsub - 1 else cur_ref[(sb + 1) * B:(sb + 2) * B, sl]
            return jnp.concatenate([before, cur_ref[rows, sl], after, ctx_ref[:, sl]], axis=0)

        for kh in range(ATTN_KV_HEADS):
            sl = slice(kh * LANES, (kh + 1) * LANES)
            k_all = window(kp_ref, kc_ref, kn_ref, kx_ref, sl)
            v_all = window(vp_ref, vc_ref, vn_ref, vx_ref, sl)
            q_rows = []
            for u in range(group // 2):
                c = (group // 2) * kh + u
                qpair = q_ref[rows, c * LANES:(c + 1) * LANES]
                zero_q = jnp.zeros_like(qpair)
                q_rows += [jnp.where(lo, qpair, zero_q), jnp.where(lo, zero_q, qpair)]
            s_all = _dot_nt(jnp.concatenate(q_rows, axis=0), k_all)
            e_rows, inv = [], []
            for b in range(group):
                sink = sink_ref[group * kh + b] * LOG2_E
                s = jnp.where(valid, s_all[b * B:(b + 1) * B, :], -jnp.inf)
                m = jnp.maximum(jnp.max(s, axis=1, keepdims=True), sink)
                e = jnp.exp2(s - m)
                inv.append(1.0 / (jnp.sum(e, axis=1, keepdims=True) + jnp.exp2(sink - m)))
                e_rows.append(e.astype(BF16))
            o_all = _dot(jnp.concatenate(e_rows, axis=0), v_all)
            for u in range(group // 2):
                c = (group // 2) * kh + u
                b0 = 2 * u
                o = jnp.where(lo, o_all[b0 * B:(b0 + 1) * B, :] * inv[b0],
                              o_all[(b0 + 1) * B:(b0 + 2) * B, :] * inv[b0 + 1])
                o_ref[rows, c * LANES:(c + 1) * LANES] = o.astype(o_ref.dtype)


def _attn(sink, q, kd, vd, kx, vx):
    L = q.shape[0]
    s_ctx = kx.shape[0]
    nsub = ATTN_BLOCKS_PER_STEP
    nb = L // ATTN_BLOCK
    prev = lambda i: (jnp.maximum(nsub * i - 1, 0), 0)
    cur = lambda i: (i, 0)
    nxt = lambda i: (jnp.minimum(nsub * (i + 1), nb - 1), 0)
    edge = lambda f: pl.BlockSpec((ATTN_BLOCK, KV_DUP), f)
    body = pl.BlockSpec((nsub * ATTN_BLOCK, KV_DUP), cur)
    return pl.pallas_call(
        functools.partial(_attn_kernel, s_ctx=s_ctx),
        grid=(nb // nsub,),
        in_specs=[pl.BlockSpec(memory_space=pltpu.SMEM),
                  pl.BlockSpec((nsub * ATTN_BLOCK, ATTN_WIDTH), cur),
                  edge(prev), body, edge(nxt), edge(prev), body, edge(nxt),
                  pl.BlockSpec((s_ctx, KV_DUP), lambda i: (0, 0)),
                  pl.BlockSpec((s_ctx, KV_DUP), lambda i: (0, 0))],
        out_specs=pl.BlockSpec((nsub * ATTN_BLOCK, ATTN_WIDTH), cur),
        out_shape=jax.ShapeDtypeStruct((L, ATTN_WIDTH), BF16),
        compiler_params=_params("parallel"),
        name="attn",
    )(sink, q, kd, kd, kd, vd, vd, vd, kx, vx)


OUT_CHUNK = 256


def _out_kernel(yf_ref, yb_ref, xs_ref, z_ref, g_ref, ya_ref, x_ref, dsk_ref, nw_ref, wssd_ref, wattn_ref, wo_ref,
                bo_ref, g1_ref, ln1g_ref, ln1b_ref, sh2_ref, sc2_ref, wrh_ref, wrl_ref, x1_ref, h2_ref, aff_ref):
    br_attn = _dot(ya_ref[...], wattn_ref[...])
    sumsq = None
    br_ssd = None
    for c0 in range(0, D_INNER, OUT_CHUNK):
        cs = slice(c0, c0 + OUT_CHUNK)
        y = yf_ref[:, cs].astype(F32) + yb_ref[:, cs].astype(F32) + dsk_ref[:, cs] * xs_ref[:, cs].astype(F32)
        y = y * _silu(z_ref[:, cs].astype(F32))
        sq = jnp.sum(y * y, -1, keepdims=True)
        part = _dot((y * nw_ref[:, cs]).astype(BF16), wssd_ref[cs, :])
        sumsq = sq if sumsq is None else sumsq + sq
        br_ssd = part if br_ssd is None else br_ssd + part
    br_ssd = br_ssd * lax.rsqrt(sumsq * (1.0 / D_INNER) + LN_EPS)
    g_ssd = _sigmoid(g_ref[:, :D_MODEL].astype(F32))
    g_attn = _sigmoid(g_ref[:, D_MODEL:].astype(F32))
    m = g_ssd * br_ssd + g_attn * br_attn
    o = _dot(m.astype(BF16), wo_ref[...]) + bo_ref[...]
    x1 =_ln(DEEPNORM_ALPHA * x_ref[...] + g1_ref[...] * o) * ln1g_ref[...] + ln1b_ref[...]
    x1_ref[...] = x1
    h2 = _ln(x1) * (1.0 + sc2_ref[...]) + sh2_ref[...]
    h2_ref[...] = h2.astype(h2_ref.dtype)
    h_hi = h2.astype(BF16)
    h_lo = (h2 - h_hi.astype(F32)).astype(BF16)
    logits = _dot_nt(h_hi, wrh_ref[...]) + (_dot_nt(h_hi, wrl_ref[...]) + _dot_nt(h_lo, wrh_ref[...]))
    lane = lax.broadcasted_iota(I32, logits.shape, 1)
    logits = jnp.where(lane < N_EXPERTS, logits, -jnp.inf)
    e = jnp.exp(logits - jnp.max(logits, axis=1, keepdims=True))
    aff = e / jnp.sum(e, axis=1, keepdims=True)
    aff_ref[...] = aff.T[:N_EXPERTS, :]


def _out(yf, yb, xc, p_main, ya, x, dsk, nw, wssd, wattn, wo, bo, g1, ln1g, ln1b, sh2, sc2, wrh, wrl, tm):
    L = x.shape[0]
    row = lambda w: pl.BlockSpec((1, w), lambda i: (0, 0))
    full = lambda a: pl.BlockSpec(a.shape, lambda i: (0, 0))
    return pl.pallas_call(
        _out_kernel,
        grid=(L // tm,),
        in_specs=[pl.BlockSpec((tm, D_INNER), lambda i: (i, 0)),
                  pl.BlockSpec((tm, D_INNER), lambda i: (i, 0)),
                  pl.BlockSpec((tm, D_INNER), lambda i: (i, 0)),
                  pl.BlockSpec((tm, D_INNER), lambda i: (i, OFF_Z // D_INNER)),
                  pl.BlockSpec((tm, 2 * D_MODEL), lambda i: (i, OFF_G // (2 * D_MODEL))),
                  pl.BlockSpec((tm, ATTN_WIDTH), lambda i: (i, 0)),
                  pl.BlockSpec((tm, D_MODEL), lambda i: (i, 0)),
                  row(D_INNER), row(D_INNER), full(wssd), full(wattn), full(wo),
                  row(D_MODEL), row(D_MODEL), row(D_MODEL), row(D_MODEL), row(D_MODEL), row(D_MODEL),
                  full(wrh), full(wrl)],
        out_specs=[pl.BlockSpec((tm, D_MODEL), lambda i: (i, 0)),
                   pl.BlockSpec((tm, D_MODEL), lambda i: (i, 0)),
                   pl.BlockSpec((N_EXPERTS, tm), lambda i: (0, i))],
        out_shape=[jax.ShapeDtypeStruct((L, D_MODEL), F32),
                   jax.ShapeDtypeStruct((L, D_MODEL), BF16),
                   jax.ShapeDtypeStruct((N_EXPERTS, L), F32)],
        compiler_params=_params("parallel"),
        name="out_proj",
    )(yf, yb, xc, p_main, p_main, ya, x, dsk, nw, wssd, wattn, wo, bo, g1, ln1g, ln1b, sh2, sc2, wrh, wrl)


def _count(mask):
    c = jnp.sum(jnp.where(mask, 1.0, 0.0), axis=1, keepdims=True)
    return jnp.sum(c, axis=0, keepdims=True)


def _route_kernel(aff_ref, rank_ref, start_ref, cnt_ref, sel_ref, *, cap, nrows):
    E = N_EXPERTS
    R = nrows
    tok = (lax.broadcasted_iota(I32, (R, LANES), 0) * LANES + lax.broadcasted_iota(I32, (R, LANES), 1))
    capf = jnp.float32(cap)

    def as_f32(word):
        return pltpu.bitcast(word, F32)

    def thr_body(it, ts):
        bit = lax.shift_left(jnp.int32(1), 30 - it)
        out = []
        for e in range(E):
            cand = ts[e] | bit
            out.append(jnp.where(_count(aff_ref[e] >= as_f32(cand)) >= capf, cand, ts[e]))
        return tuple(out)

    ts = lax.fori_loop(0, 31, thr_body, tuple(jnp.zeros((1, 1), I32) for _ in range(E)))
    thr = [as_f32(t) for t in ts]
    need = [capf - _count(aff_ref[e] > thr[e]) for e in range(E)]

    nbits = max((R * LANES - 1).bit_length(), 1)

    def tie_body(it, js):
        bit = lax.shift_left(jnp.int32(1), nbits - 1 - it)
        out = []
        for e in range(E):
            cand = js[e] | bit
            f = _count((aff_ref[e] == thr[e]) & (tok < cand))
            out.append(jnp.where(f < need[e], cand, js[e]))
        return tuple(out)

    js = lax.fori_loop(0, nbits, tie_body, tuple(jnp.zeros((1, 1), I32) for _ in range(E)))
    for e in range(E):
        a = aff_ref[e]
        sel = (a > thr[e]) | ((a == thr[e]) & (tok <= js[e]))
        sel_ref[e] = jnp.where(sel, 1.0, 0.0)

    li = lax.broadcasted_iota(I32, (LANES, LANES), 0)
    lj = lax.broadcasted_iota(I32, (LANES, LANES), 1)
    upper_incl = jnp.where(li <= lj, 1.0, 0.0).astype(BF16)
    ri = lax.broadcasted_iota(I32, (R, R), 0)
    rj = lax.broadcasted_iota(I32, (R, R), 1)
    row_lower = jnp.where(ri >= rj, 1.0, 0.0).astype(BF16)
    lane_id = lax.broadcasted_iota(I32, (R, LANES), 1)

    def expert_body(e, carry):
        start_acc, cnt_acc = carry
        sel = sel_ref[e]
        w = _dot(sel.astype(BF16), upper_incl)
        tot_b = jnp.broadcast_to(jnp.sum(sel, axis=1, keepdims=True), (R, LANES))
        seg_b = jnp.floor((tot_b + (SUBLANES - 1)) * (1.0 / SUBLANES)) * SUBLANES
        cum_incl = _dot(row_lower, seg_b.astype(BF16))
        rank_ref[e] = jnp.where(sel > 0.0, w - 1.0, -1.0)
        start_acc = jnp.where(lane_id == e, cum_incl - seg_b, start_acc)
        cnt_acc = jnp.where(lane_id == e, tot_b, cnt_acc)
        return start_acc, cnt_acc

    zero = jnp.zeros((R, LANES), F32)
    start_acc, cnt_acc = lax.fori_loop(0, E, expert_body, (zero, zero))
    start_ref[...] = start_acc.astype(I32)
    cnt_ref[...] = cnt_acc.astype(I32)


def _route(aff3, cap):
    E, R, _ = aff3.shape
    return pl.pallas_call(
        functools.partial(_route_kernel, cap=cap, nrows=R),
        out_shape=[jax.ShapeDtypeStruct((E, R, LANES), F32),
                   jax.ShapeDtypeStruct((R, LANES), I32),
                   jax.ShapeDtypeStruct((R, LANES), I32)],
        scratch_shapes=[pltpu.VMEM((E, R, LANES), F32)],
        compiler_params=pltpu.CompilerParams(vmem_limit_bytes=VMEM_LIMIT_BYTES),
        name="route",
    )(aff3)


SLOTS = 32
WIN = N_EXPERTS * SLOTS
DISPATCH_TILES = 2
D_SLOTS = DISPATCH_TILES * SLOTS
D_WIN = N_EXPERTS * D_SLOTS


def _slot_onehot(tgt, values, k, slots=SLOTS):
    sub = lax.broadcasted_iota(I32, (slots, tgt.shape[1]), 0).astype(F32) + (slots * k).astype(F32)
    rows = []
    for e in range(N_EXPERTS):
        hit = jnp.broadcast_to(tgt[e:e + 1, :], sub.shape) == sub
        rows.append(jnp.where(hit, jnp.broadcast_to(values[e:e + 1, :], sub.shape), 0.0))
    return jnp.concatenate(rows, axis=0).astype(BF16)


def _windows(n, slots):
    return lax.shift_right_logical(n + (slots - 1), slots.bit_length() - 1)


def _rounds(cnt_ref, r):
    n = cnt_ref[0, r]
    for e in range(1, N_EXPERTS):
        n = jnp.maximum(n, cnt_ref[e, r])
    return _windows(n, SLOTS)


def _dispatch_kernel(start_ref, cnt_ref, h_ref, rank_ref, xs_hbm, stage_ref, ovf_ref, sem, osem):
    p = pl.program_id(0)
    last = pl.num_programs(0) - 1
    slot = lax.rem(p, 2)
    r0 = p * DISPATCH_TILES
    r_end = r0 + DISPATCH_TILES - 1

    def rows_used(e):
        return start_ref[e, r_end] - start_ref[e, r0] + cnt_ref[e, r_end]

    e_id = lax.broadcasted_iota(I32, (N_EXPERTS, LANES), 0)
    tgts = []
    for b in range(DISPATCH_TILES):
        rank = rank_ref[:, b, 0, :]
        off = jnp.zeros_like(rank)
        if b > 0:
            for e in range(N_EXPERTS):
                d = start_ref[e, r0 + b] - start_ref[e, r0]
                off = jnp.where(e_id == e, d.astype(F32), off)
        tgts.append(jnp.where(rank >= 0.0, rank + off, -1.0))
    tgt = jnp.concatenate(tgts, axis=1)
    ones = jnp.ones_like(tgt)

    def compact(k):
        return _dot(_slot_onehot(tgt, ones, k, D_SLOTS), h_ref[...])

    def window_copy(src_ref, e, k, s):
        dst = pl.multiple_of(start_ref[e, r0] + D_SLOTS * k, SUBLANES)
        return pltpu.make_async_copy(src_ref.at[pl.ds(e * D_SLOTS, D_SLOTS), :],
                                     xs_hbm.at[e, pl.ds(dst, D_SLOTS), :], s)

    def stage_wait(sl):
        pltpu.make_async_copy(stage_ref.at[sl], stage_ref.at[sl], sem.at[sl]).wait()

    stage_ref[slot] = compact(jnp.int32(0))

    @pl.when(p > 0)
    def _():
        stage_wait(1 - slot)

    for e in range(N_EXPERTS):
        window_copy(stage_ref.at[slot], e, 0, sem.at[slot]).start()

    def extra(k, _):
        ovf_ref[...] = compact(k)
        for e in range(N_EXPERTS):
            @pl.when(rows_used(e) > D_SLOTS * k)
            def _():
                window_copy(ovf_ref, e, k, osem).start()
        for e in range(N_EXPERTS):
            @pl.when(rows_used(e) > D_SLOTS * k)
            def _():
                window_copy(ovf_ref, e, k, osem).wait()
        return 0

    n_max = rows_used(0)
    for e in range(1, N_EXPERTS):
        n_max = jnp.maximum(n_max, rows_used(e))
    lax.fori_loop(1, _windows(n_max, D_SLOTS), extra, 0)

    def tail_copies(act):
        n_rows = xs_hbm.shape[1]
        for e in range(N_EXPERTS):
            off0 = start_ref[e, r0] + D_SLOTS * jnp.maximum(_windows(rows_used(e), D_SLOTS), 1)
            length = n_rows - off0
            n_full = lax.div(length, D_WIN)

            def full(i, _, e=e, off0=off0):
                dst = pl.multiple_of(off0 + i * D_WIN, SUBLANES)
                act(pltpu.make_async_copy(ovf_ref, xs_hbm.at[e, pl.ds(dst, D_WIN), :], osem))
                return 0

            lax.fori_loop(0, n_full, full, 0)
            off = off0 + n_full * D_WIN
            rem = length - n_full * D_WIN
            size = D_WIN // 2
            while size >= SUBLANES:
                @pl.when((rem & size) != 0)
                def _(e=e, off=off, size=size):
                    act(pltpu.make_async_copy(ovf_ref.at[pl.ds(0, size), :],
                                              xs_hbm.at[e, pl.ds(pl.multiple_of(off, SUBLANES), size), :], osem))
                off = off + (rem & size)
                size //= 2

    @pl.when(p == last)
    def _():
        stage_wait(slot)
        ovf_ref[...] = jnp.zeros(ovf_ref.shape, F32)
        tail_copies(lambda cp: cp.start())
        tail_copies(lambda cp: cp.wait())


def _padded_cap(cap, n_tiles, ts):
    worst = cap + (SUBLANES - 1) * n_tiles
    return -(-worst // ts) * ts


def _dispatch(start, cnt, h2, rank4, cap):
    L = h2.shape[0]
    T = DISPATCH_TILES * LANES
    return pl.pallas_call(
        _dispatch_kernel,
        grid_spec=pltpu.PrefetchScalarGridSpec(
            num_scalar_prefetch=2,
            grid=(L // T,),
            in_specs=[pl.BlockSpec((T, D_MODEL), lambda p, *_: (p, 0)),
                      pl.BlockSpec((N_EXPERTS, DISPATCH_TILES, 1, LANES), lambda p, *_: (0, p, 0, 0))],
            out_specs=pl.BlockSpec(memory_space=pl.ANY),
            scratch_shapes=[pltpu.VMEM((2, D_WIN, D_MODEL), F32),
                            pltpu.VMEM((D_WIN, D_MODEL), F32),
                            pltpu.SemaphoreType.DMA((2,)),
                            pltpu.SemaphoreType.DMA(())]),
        out_shape=jax.ShapeDtypeStruct((N_EXPERTS, cap + D_SLOTS, D_MODEL), F32),
        compiler_params=_params("arbitrary"),
        name="dispatch",
    )(start, cnt, h2, rank4)


def _expert_rows(start_ref, cnt_ref, e):
    last_tile = start_ref.shape[1] - 1
    shift = SUBLANES.bit_length() - 1
    seg = lax.shift_left(lax.shift_right_logical(cnt_ref[e, last_tile] + (SUBLANES - 1), shift), shift)
    return start_ref[e, last_tile] + seg


def _expert_kernel(start_ref, cnt_ref, xs_ref, wgu_hbm, wd_hbm, o_ref, wgu_f32, wd_f32, wgu_bf, wd_bf, sem, *, ts):
    e = pl.program_id(0)
    j = pl.program_id(1)
    n_experts = pl.num_programs(0)
    live = _expert_rows(start_ref, cnt_ref, e) - j * ts

    def weight_copies(ee, slot):
        return (pltpu.make_async_copy(wgu_hbm.at[ee], wgu_f32.at[slot], sem.at[0, slot]),
                pltpu.make_async_copy(wd_hbm.at[ee], wd_f32.at[slot], sem.at[1, slot]))

    @pl.when(j == 0)
    def _():
        slot = lax.rem(e, 2)

        @pl.when(e == 0)
        def _():
            for cp in weight_copies(e, slot):
                cp.start()

        for cp in weight_copies(e, slot):
            cp.wait()

        @pl.when(e + 1 < n_experts)
        def _():
            for cp in weight_copies(e + 1, 1 - slot):
                cp.start()

        rows = 128
        for r0 in range(0, D_MODEL, rows):
            wgu_bf[r0:r0 + rows, :] = wgu_f32[slot, r0:r0 + rows, :].astype(BF16)
        for r0 in range(0, EXPERT_FF, rows):
            wd_bf[r0:r0 + rows, :] = wd_f32[slot, r0:r0 + rows, :].astype(BF16)

    @pl.when(live > 0)
    def _():
        row = lax.broadcasted_iota(I32, (ts, D_MODEL), 0)
        xs = jnp.where(row < live, xs_ref[0], 0.0).astype(BF16)
        gu = _dot(xs, wgu_bf[...])
        act = _silu(gu[:, :EXPERT_FF]) * gu[:, EXPERT_FF:]
        o_ref[0] = _dot(act.astype(BF16), wd_bf[...])

    @pl.when(live <= 0)
    def _():
        o_ref[0] = jnp.zeros((ts, D_MODEL), F32)


def _expert(start, cnt, xs, wgu, wd, capp, ts):
    E = xs.shape[0]
    return pl.pallas_call(
        functools.partial(_expert_kernel, ts=ts),
        grid_spec=pltpu.PrefetchScalarGridSpec(
            num_scalar_prefetch=2,
            grid=(E, capp // ts),
            in_specs=[pl.BlockSpec((1, ts, D_MODEL), lambda e, j, start, cnt: (
                          e, jnp.minimum(j, jnp.maximum(_expert_rows(start, cnt, e) - 1, 0) // ts), 0)),
                      pl.BlockSpec(memory_space=pl.ANY),
                      pl.BlockSpec(memory_space=pl.ANY)],
            out_specs=pl.BlockSpec((1, ts, D_MODEL), lambda e, j, *_: (e, j, 0)),
            scratch_shapes=[pltpu.VMEM((2, D_MODEL, 2 * EXPERT_FF), F32),
                            pltpu.VMEM((2, EXPERT_FF, D_MODEL), F32),
                            pltpu.VMEM((D_MODEL, 2 * EXPERT_FF), BF16),
                            pltpu.VMEM((EXPERT_FF, D_MODEL), BF16),
                            pltpu.SemaphoreType.DMA((2, 2))]),
        out_shape=jax.ShapeDtypeStruct((E, capp, D_MODEL), F32),
        compiler_params=_params("arbitrary", "arbitrary"),
        name="expert_ffn",
    )(start, cnt, xs, wgu, wd)


COMBINE_TILES = 2


def _combine_kernel(start_ref, cnt_ref, ye_hbm, x1_ref, aff_ref, rank_ref, g2_ref, lng_ref, lnb_ref,
                    o_ref, buf_ref, ext_ref, sem, esem, *, cap):
    p = pl.program_id(0)
    last = pl.num_programs(0) - 1
    slot = lax.rem(p, 2)
    T = LANES

    def win_start(e, r, k):
        return pl.multiple_of(jnp.minimum(start_ref[e, r] + SLOTS * k, cap - SLOTS), SUBLANES)

    def window_copies(dst_ref, r, k, s):
        return [pltpu.make_async_copy(ye_hbm.at[e, pl.ds(win_start(e, r, k), SLOTS), :],
                                      dst_ref.at[pl.ds(e * SLOTS, SLOTS), :], s) for e in range(N_EXPERTS)]

    def fetch(step, sl):
        for b in range(COMBINE_TILES):
            for cp in window_copies(buf_ref.at[sl, b], step * COMBINE_TILES + b, 0, sem.at[sl]):
                cp.start()

    @pl.when(p == 0)
    def _():
        fetch(p, slot)

    @pl.when(p < last)
    def _():
        fetch(p + 1, 1 - slot)

    pltpu.make_async_copy(buf_ref.at[slot], buf_ref.at[slot], sem.at[slot]).wait()

    for b in range(COMBINE_TILES):
        r = p * COMBINE_TILES + b
        rank = rank_ref[:, b, 0, :]
        aff = aff_ref[:, b * T:(b + 1) * T]
        g_hi = aff.astype(BF16).astype(F32)
        g_lo = aff - g_hi
        e_id = lax.broadcasted_iota(I32, rank.shape, 0)

        def contribution(k, src_ref, r=r, rank=rank, g_hi=g_hi, g_lo=g_lo, e_id=e_id):
            shift = jnp.zeros_like(rank)
            for e in range(N_EXPERTS):
                d = start_ref[e, r] + SLOTS * k - win_start(e, r, k)
                shift = jnp.where(e_id == e, d.astype(F32), shift)
            base = (SLOTS * k).astype(F32)
            in_round = (rank >= base) & (rank < base + SLOTS)
            tgt = jnp.where(in_round, rank + shift, -1.0)
            yb = src_ref[...].astype(BF16)
            tn = (((0,), (0,)), ((), ()))
            return (lax.dot_general(_slot_onehot(tgt, g_hi, k), yb, tn, preferred_element_type=F32)
                    + lax.dot_general(_slot_onehot(tgt, g_lo, k), yb, tn, preferred_element_type=F32))

        def extra(k, acc, r=r, contribution=contribution):
            for cp in window_copies(ext_ref, r, k, esem):
                cp.start()
            pltpu.make_async_copy(ext_ref, ext_ref, esem).wait()
            return acc + contribution(k, ext_ref)

        moe = lax.fori_loop(1, _rounds(cnt_ref, r), extra, contribution(jnp.int32(0), buf_ref.at[slot, b]))
        rows = slice(b * T, (b + 1) * T)
        y = _ln(DEEPNORM_ALPHA * x1_ref[rows, :] + g2_ref[...] * moe)
        o_ref[rows, :] = y * lng_ref[...] + lnb_ref[...]


def _combine(start, cnt, ye, x1, aff_t, rank4, g2, lng, lnb):
    L = x1.shape[0]
    E, cap, _ = ye.shape
    T = COMBINE_TILES * LANES
    row = pl.BlockSpec((1, D_MODEL), lambda p, *_: (0, 0))
    return pl.pallas_call(
        functools.partial(_combine_kernel, cap=cap),
        grid_spec=pltpu.PrefetchScalarGridSpec(
            num_scalar_prefetch=2,
            grid=(L // T,),
            in_specs=[pl.BlockSpec(memory_space=pl.ANY),
                      pl.BlockSpec((T, D_MODEL), lambda p, *_: (p, 0)),
                      pl.BlockSpec((E, T), lambda p, *_: (0, p)),
                      pl.BlockSpec((E, COMBINE_TILES, 1, LANES), lambda p, *_: (0, p, 0, 0)),
                      row, row, row],
            out_specs=pl.BlockSpec((T, D_MODEL), lambda p, *_: (p, 0)),
            scratch_shapes=[pltpu.VMEM((2, COMBINE_TILES, WIN, D_MODEL), F32),
                            pltpu.VMEM((WIN, D_MODEL), F32),
                            pltpu.SemaphoreType.DMA((2,)),
                            pltpu.SemaphoreType.DMA(())]),
        out_shape=jax.ShapeDtypeStruct((L, D_MODEL), F32),
        compiler_params=_params("arbitrary"),
        name="combine",
    )(start, cnt, ye, x1, aff_t, rank4, g2, lng, lnb)


def _rope_tables(L):
    inv_freq = ROPE_BASE ** (-jnp.arange(0, ROPE_AXIS_DIM, 2, dtype=F32) / ROPE_AXIS_DIM)

    def table(n, on_row_lanes):
        ang = jnp.arange(n, dtype=F32)[:, None] * inv_freq[None, :]
        one, zero = jnp.ones_like(ang), jnp.zeros_like(ang)
        mask = jnp.concatenate(([one, one, zero, zero] if on_row_lanes else [zero, zero, one, one]) * 2, -1)
        ang = jnp.concatenate([ang] * 8, -1)
        return jnp.stack([jnp.cos(ang) * mask, jnp.sin(ang) * mask])

    return table(L // GRID_W, True), table(GRID_W, False)


def _row_tile(L, pref):
    return pref if L % pref == 0 else L


def kernel(x, c, ctx, c_ctx, w_ada, b_ada, w_in, conv_w, conv_b, a_log, dt_bias, d_skip, ssd_norm_w, attn_sink,
           w_ssd_br, w_attn_br, w_o, b_o, ln1_g, ln1_b, w_router, w_gate_up, w_down, ln2_g, ln2_b):
    assert x.shape[0] == 1 and w_in.shape[0] == 1, "single batch element, depth 1"
    L = x.shape[1]
    S = ctx.shape[1]
    assert L % (LANES * SUBLANES) == 0 and S % (SSD_CHUNKS_PER_STEP * SSD_CHUNK) == 0
    cap = CAPACITY_FACTOR * L // N_EXPERTS
    x2, ctx2 = x[0], ctx[0]

    cc = jnp.zeros((SUBLANES, D_MODEL), F32).at[0].set(c[0]).at[1].set(c_ctx)
    mod = _ada(cc, w_ada[0], b_ada[0][None, :])
    sh1, sc1, g1, sh2, sc2, g2 = [mod[0:1, k * D_MODEL:(k + 1) * D_MODEL] for k in range(6)]
    csh1, csc1 = mod[1:2, 0:D_MODEL], mod[1:2, D_MODEL:2 * D_MODEL]

    w = w_in[0]
    o = 0
    parts = {}
    for name, width in (("z", D_INNER), ("xbc", CONV_CH), ("dt", 2 * SSD_HEADS), ("q", ATTN_WIDTH),
                        ("k", KV_WIDTH), ("v", KV_WIDTH), ("g", 2 * D_MODEL)):
        parts[name] = w[:, o:o + width]
        o += width
    w_main = jnp.concatenate([parts[n] for n in ("z", "g", "xbc", "q", "k", "v")], axis=1).astype(BF16)
    w_dt = jnp.pad(parts["dt"], ((0, 0), (0, LANES - 2 * SSD_HEADS))).astype(BF16)

    pad_heads = lambda v: jnp.pad(v.reshape(1, 2 * SSD_HEADS), ((0, 0), (0, LANES - 2 * SSD_HEADS)))
    bias_row, alog_row = pad_heads(dt_bias[0]), pad_heads(a_log[0])
    conv_b2 = conv_b[0][None, :]

    tm_c = _row_tile(S, 256)
    pc, dtc = _ln_proj(ctx2, csh1, csc1, w_main, w_dt, tm_c)
    xcc = _conv(pc, conv_w[0], conv_b2, tm_c)
    kx, vx = _qkv(pc, None, None, tm_c, rope=False, with_q=False)
    s_zero = jnp.zeros((2, SSD_GROUPS, SSD_STATE, GROUP_W), F32)
    _, _, s_ctx = _ssd(xcc, dtc, s_zero, bias_row, alog_row)

    p_main, dt = _ln_proj(x2, sh1, sc1, w_main, w_dt, _row_tile(L, 512))
    xc = _conv(p_main, conv_w[0], conv_b2, _row_tile(L, 1024))
    row_tab, col_tab = _rope_tables(L)
    q, kd, vd = _qkv(p_main, row_tab, col_tab, _row_tile(L, 512), rope=True, with_q=True)
    yf, yb, _ = _ssd(xc, dt, s_ctx, bias_row, alog_row)
    ya = _attn(attn_sink[0], q, kd, vd, kx, vx)

    dsk = jnp.repeat(d_skip[0], SSD_HEAD_DIM)[None, :]
    wr = jnp.pad(w_router[0].T, ((0, LANES - N_EXPERTS), (0, 0)))
    wr_hi = wr.astype(BF16)
    wr_lo = (wr - wr_hi.astype(F32)).astype(BF16)
    x1, h2, aff_t = _out(yf, yb, xc, p_main, ya, x2, dsk, ssd_norm_w[0][None, :],
                         w_ssd_br[0].astype(BF16), w_attn_br[0].astype(BF16), w_o[0].astype(BF16),
                         b_o[0][None, :], g1, ln1_g[0][None, :], ln1_b[0][None, :], sh2, sc2, wr_hi, wr_lo,
                         _row_tile(L, 256))

    R = L // LANES
    rank, start_t, cnt_t = _route(aff_t.reshape(N_EXPERTS, R, LANES), cap)
    rank4 = rank.reshape(N_EXPERTS, R, 1, LANES)
    start = start_t[:, :N_EXPERTS].T
    cnt = cnt_t[:, :N_EXPERTS].T
    ts = min(cap, 512)
    capp = _padded_cap(cap, R, ts)
    xs = _dispatch(start, cnt, h2, rank4, capp)
    ye = _expert(start, cnt, xs, w_gate_up[0], w_down[0], capp, ts)
    out = _combine(start, cnt, ye, x1, aff_t, rank4, g2, ln2_g[0][None, :], ln2_b[0][None, :])
    return out[None]
```

```python
import functools

import numpy as np
import jax
import jax.numpy as jnp
from jax import lax
from jax.experimental import pallas as pl
from jax.experimental.pallas import tpu as pltpu

F32 = jnp.float32
BF16 = jnp.bfloat16
I32 = jnp.int32
HIGHEST = lax.Precision.HIGHEST
LOG2_E = 1.4426950408889634

LANES = 128
SUBLANES = 8
VMEM_LIMIT_BYTES = 56 * 1024 * 1024

D_MODEL = 1024
GRID_W = 64
LN_EPS = 1e-5
DEPTH = 1
DEEPNORM_ALPHA = (2.0 * DEPTH) ** 0.25
D_INNER = 2048
SSD_HEAD_DIM = 64
SSD_HEADS = 32
SSD_GROUPS = 4
SSD_STATE = 128
SSD_CHUNK = 128
CONV_W = 5
CONV_CH = D_INNER + 2 * SSD_GROUPS * SSD_STATE
ATTN_HEAD_DIM = 64
ATTN_Q_HEADS = 16
ATTN_KV_HEADS = 4
ATTN_WIDTH = 1024
KV_WIDTH = 256
WINDOW = 128
ATTN_BLOCK = 128
ATTN_SCALE = ATTN_HEAD_DIM ** -0.5
ROPE_BASE = 10000.0
ROPE_AXIS_DIM = ATTN_HEAD_DIM // 2
N_EXPERTS = 16
EXPERT_FF = 1024
CAPACITY_FACTOR = 2

OFF_Z = 0
OFF_G = OFF_Z + D_INNER
OFF_XBC = OFF_G + 2 * D_MODEL
OFF_Q = OFF_XBC + CONV_CH
OFF_K = OFF_Q + ATTN_WIDTH
OFF_V = OFF_K + KV_WIDTH
N_MAIN = OFF_V + KV_WIDTH
PROJ_TN = N_MAIN // 4
KV_DUP = 2 * KV_WIDTH


def _params(*sem):
    return pltpu.CompilerParams(dimension_semantics=sem, vmem_limit_bytes=VMEM_LIMIT_BYTES)


def _sigmoid(x):
    return 1.0 / (1.0 + jnp.exp2(x * (-LOG2_E)))


def _silu(x):
    return x * _sigmoid(x)


def _dot(a, b, **kw):
    return jnp.dot(a, b, preferred_element_type=F32, **kw)


def _dot_nt(a, b, **kw):
    return lax.dot_general(a, b, (((1,), (1,)), ((), ())), preferred_element_type=F32, **kw)


def _ln(x):
    mu = jnp.mean(x, -1, keepdims=True)
    xc = x - mu
    var = jnp.mean(xc * xc, -1, keepdims=True)
    return xc * lax.rsqrt(var + LN_EPS)


def _ada_kernel(s_ref, w_ref, b_ref, o_ref):
    s = _silu(s_ref[...])
    o_ref[...] = _dot(s, w_ref[...], precision=HIGHEST) + b_ref[...]


def _ada(cc, w_ada, b_ada):
    n = w_ada.shape[1]
    tn = 1024
    return pl.pallas_call(
        _ada_kernel,
        grid=(n // tn,),
        in_specs=[pl.BlockSpec((SUBLANES, D_MODEL), lambda j: (0, 0)),
                  pl.BlockSpec((D_MODEL, tn), lambda j: (0, j)),
                  pl.BlockSpec((1, tn), lambda j: (0, j))],
        out_specs=pl.BlockSpec((SUBLANES, tn), lambda j: (0, j)),
        out_shape=jax.ShapeDtypeStruct((SUBLANES, n), F32),
        compiler_params=_params("parallel"),
        name="ada",
    )(cc, w_ada, b_ada)


def _ln_proj_kernel(x_ref, sh_ref, sc_ref, w_ref, wdt_ref, o_ref, dt_ref, h_ref, *, tm):
    rows = 256 if tm % 256 == 0 else tm
    for r0 in range(0, tm, rows):
        rs = slice(r0, r0 + rows)
        h = _ln(x_ref[rs, :]) * (1.0 + sc_ref[...]) + sh_ref[...]
        h_ref[rs, :] = h.astype(BF16)
        dt_ref[rs, :] = _dot(h_ref[rs, :], wdt_ref[...])
        for n0 in range(0, N_MAIN, PROJ_TN):
            o_ref[rs, n0:n0 + PROJ_TN] = _dot(h_ref[rs, :], w_ref[:, n0:n0 + PROJ_TN]).astype(o_ref.dtype)


def _ln_proj(x, shift, scale, w_main, w_dt, tm):
    L = x.shape[0]
    resident = lambda a: pl.BlockSpec(a.shape, lambda i: (0, 0), pipeline_mode=pl.Buffered(1))
    return pl.pallas_call(
        functools.partial(_ln_proj_kernel, tm=tm),
        grid=(L // tm,),
        in_specs=[pl.BlockSpec((tm, D_MODEL), lambda i: (i, 0)),
                  pl.BlockSpec((1, D_MODEL), lambda i: (0, 0)),
                  pl.BlockSpec((1, D_MODEL), lambda i: (0, 0)),
                  resident(w_main), resident(w_dt)],
        out_specs=[pl.BlockSpec((tm, N_MAIN), lambda i: (i, 0)),
                   pl.BlockSpec((tm, LANES), lambda i: (i, 0))],
        out_shape=[jax.ShapeDtypeStruct((L, N_MAIN), BF16),
                   jax.ShapeDtypeStruct((L, LANES), F32)],
        scratch_shapes=[pltpu.VMEM((tm, D_MODEL), BF16)],
        compiler_params=_params("parallel"),
        name="ln_proj",
    )(x, shift, scale, w_main, w_dt)


CONV_HALO = 64
CONV_ROWS = 128


def _conv_kernel(xp_ref, x_ref, xn_ref, w_ref, b_ref, o_ref, ext_ref, *, tm):
    i = pl.program_id(0)
    last = pl.num_programs(0) - 1
    zero_halo = jnp.zeros_like(xp_ref[...])
    ext_ref[0:CONV_HALO, :] = jnp.where(i > 0, xp_ref[...], zero_halo)
    ext_ref[CONV_HALO:CONV_HALO + tm, :] = x_ref[...]
    ext_ref[CONV_HALO + tm:, :] = jnp.where(i < last, xn_ref[...], zero_halo)

    half = CONV_W // 2
    side_taps = [k for k in range(CONV_W) if k != half]
    win = CONV_ROWS + 2 * CONV_HALO
    rr = lax.broadcasted_iota(I32, (len(side_taps) * CONV_ROWS, win), 0)
    jj = lax.broadcasted_iota(I32, (len(side_taps) * CONV_ROWS, win), 1)
    blk = lax.shift_right_logical(rr, CONV_ROWS.bit_length() - 1)
    tap = jnp.where(blk < half, blk, blk + 1)
    selector = jnp.where(jj == (rr - blk * CONV_ROWS) + (CONV_HALO - half) + tap, 1.0, 0.0).astype(BF16)

    for r0 in range(0, tm, CONV_ROWS):
        shifted = _dot(selector, ext_ref[r0:r0 + win, :])
        centre = ext_ref[CONV_HALO + r0:CONV_HALO + r0 + CONV_ROWS, :].astype(F32)
        acc = b_ref[...] + w_ref[half:half + 1, :] * centre
        for n, k in enumerate(side_taps):
            acc = acc + w_ref[k:k + 1, :] * shifted[n * CONV_ROWS:(n + 1) * CONV_ROWS, :]
        o_ref[r0:r0 + CONV_ROWS, :] = _silu(acc).astype(o_ref.dtype)


def _conv(p_main, conv_w, conv_b, tm, tc=1024):
    L = p_main.shape[0]
    hb = tm // CONV_HALO
    n_halo = L // CONV_HALO
    c0 = OFF_XBC // tc
    return pl.pallas_call(
        functools.partial(_conv_kernel, tm=tm),
        grid=(L // tm, CONV_CH // tc),
        in_specs=[pl.BlockSpec((CONV_HALO, tc), lambda i, j: (jnp.maximum(i * hb - 1, 0), c0 + j)),
                  pl.BlockSpec((tm, tc), lambda i, j: (i, c0 + j)),
                  pl.BlockSpec((CONV_HALO, tc), lambda i, j: (jnp.minimum((i + 1) * hb, n_halo - 1), c0 + j)),
                  pl.BlockSpec((CONV_W, tc), lambda i, j: (0, j)),
                  pl.BlockSpec((1, tc), lambda i, j: (0, j))],
        out_specs=pl.BlockSpec((tm, tc), lambda i, j: (i, j)),
        out_shape=jax.ShapeDtypeStruct((L, CONV_CH), BF16),
        scratch_shapes=[pltpu.VMEM((tm + 2 * CONV_HALO, tc), BF16)],
        compiler_params=_params("parallel", "parallel"),
        name="conv",
    )(p_main, p_main, p_main, conv_w, conv_b)


def _qkv_kernel(*refs, tm, rope, with_q):
    if with_q:
        q_ref, k_ref, v_ref, rowtab_ref, coltab_ref, qo_ref, ko_ref, vo_ref = refs
    else:
        k_ref, v_ref, ko_ref, vo_ref = refs
    lane = lax.broadcasted_iota(I32, (tm, LANES), 1)
    first = (lane & 31) < 16
    lo = lane < 64

    if rope:
        rows_per_tile = tm // GRID_W
        r0 = pl.program_id(0) * rows_per_tile
        cos, sin = [jnp.concatenate([rowtab_ref[f, pl.ds(r0 + b, 1), :] + coltab_ref[f]
                                     for b in range(rows_per_tile)], axis=0) for f in range(2)]

    def rot(x):
        if not rope:
            return x
        r = jnp.where(first, -pltpu.roll(x, LANES - 16, 1), pltpu.roll(x, 16, 1))
        return x * cos + r * sin

    def dup(x, o_ref, p):
        r = pltpu.roll(x, 64, 1)
        o_ref[:, (2 * p) * LANES:(2 * p + 1) * LANES] = jnp.where(lo, x, r).astype(o_ref.dtype)
        o_ref[:, (2 * p + 1) * LANES:(2 * p + 2) * LANES] = jnp.where(lo, r, x).astype(o_ref.dtype)

    if with_q:
        for c in range(ATTN_WIDTH // LANES):
            sl = slice(c * LANES, (c + 1) * LANES)
            qo_ref[:, sl] = (rot(q_ref[:, sl].astype(F32)) * (ATTN_SCALE * LOG2_E)).astype(qo_ref.dtype)
    for p in range(KV_WIDTH // LANES):
        sl = slice(p * LANES, (p + 1) * LANES)
        dup(rot(k_ref[:, sl].astype(F32)), ko_ref, p)
        dup(v_ref[:, sl].astype(F32), vo_ref, p)


def _qkv(p_main, row_tab, col_tab, tm, rope, with_q):
    L = p_main.shape[0]
    kv_specs = [pl.BlockSpec((tm, KV_WIDTH), lambda i: (i, OFF_K // KV_WIDTH)),
                pl.BlockSpec((tm, KV_WIDTH), lambda i: (i, OFF_V // KV_WIDTH))]
    kv_out = [pl.BlockSpec((tm, KV_DUP), lambda i: (i, 0)), pl.BlockSpec((tm, KV_DUP), lambda i: (i, 0))]
    kv_shape = [jax.ShapeDtypeStruct((L, KV_DUP), BF16), jax.ShapeDtypeStruct((L, KV_DUP), BF16)]
    if with_q:
        in_specs = ([pl.BlockSpec((tm, ATTN_WIDTH), lambda i: (i, OFF_Q // ATTN_WIDTH))] + kv_specs
                    + [pl.BlockSpec(row_tab.shape, lambda i: (0, 0, 0)), pl.BlockSpec(col_tab.shape, lambda i: (0, 0, 0))])
        out_specs = [pl.BlockSpec((tm, ATTN_WIDTH), lambda i: (i, 0))] + kv_out
        out_shape = [jax.ShapeDtypeStruct((L, ATTN_WIDTH), BF16)] + kv_shape
        args = (p_main, p_main, p_main, row_tab, col_tab)
    else:
        in_specs, out_specs, out_shape, args = kv_specs, kv_out, kv_shape, (p_main, p_main)
    return pl.pallas_call(
        functools.partial(_qkv_kernel, tm=tm, rope=rope, with_q=with_q),
        grid=(L // tm,),
        in_specs=in_specs, out_specs=out_specs, out_shape=out_shape,
        compiler_params=_params("parallel"),
        name="qkv_rope" if with_q else "kv_ctx",
    )(*args)


PAIRS_PER_GROUP = (SSD_HEADS // SSD_GROUPS) // 2
GROUP_W = (SSD_HEADS // SSD_GROUPS) * SSD_HEAD_DIM
SSD_CHUNKS_PER_STEP = 2


def _ssd_kernel(xf_ref, xb_ref, dtf_ref, dtb_ref, s0_ref, bias_ref, alog_ref, yf_ref, yb_ref, s_ref):
    Q = SSD_CHUNK

    @pl.when(pl.program_id(0) == 0)
    def _():
        s_ref[...] = s0_ref[...]

    row = lax.broadcasted_iota(I32, (Q, Q), 0)
    col = lax.broadcasted_iota(I32, (Q, Q), 1)
    lo = col < 64
    lo_row = lax.broadcasted_iota(I32, (1, Q), 1) < 64
    neg_a = -jnp.exp(alog_ref[...])

    work = []
    for sub in range(SSD_CHUNKS_PER_STEP):
        work.append((0, xf_ref, dtf_ref, yf_ref, slice(sub * Q, (sub + 1) * Q)))
        back = SSD_CHUNKS_PER_STEP - 1 - sub
        work.append((1, xb_ref, dtb_ref, yb_ref, slice(back * Q, (back + 1) * Q)))

    for d, x_ref, dt_ref, y_ref, rs in work:
        tri = (row >= col) if d == 0 else (row <= col)
        off = d * SSD_HEADS
        z = dt_ref[rs, :] + bias_ref[...]
        dt = jnp.maximum(z, 0.0) + jnp.log1p(jnp.exp(-jnp.abs(z)))
        da = dt * neg_a
        ones_tri = jnp.where(tri, 1.0, 0.0).astype(BF16)
        da_hi = da.astype(BF16)
        da_mid = (da - da_hi.astype(F32)).astype(BF16)
        da_lo = (da - da_hi.astype(F32) - da_mid.astype(F32)).astype(BF16)
        a = (_dot(ones_tri, da_hi) + (_dot(ones_tri, da_mid) + _dot(ones_tri, da_lo))) * LOG2_E
        a_t = a.T
        dt_t = dt.T
        edge = a_t[:, Q - 1:Q] if d == 0 else a_t[:, 0:1]
        w_t = (dt_t * jnp.exp2(edge - a_t)).astype(BF16)
        dec = jnp.exp2(edge)
        dt_tb = dt_t.astype(BF16)

        for g in range(SSD_GROUPS):
            bg = x_ref[rs, D_INNER + g * SSD_STATE:D_INNER + (g + 1) * SSD_STATE]
            cg = x_ref[rs, D_INNER + (SSD_GROUPS + g) * SSD_STATE:D_INNER + (SSD_GROUPS + g + 1) * SSD_STATE]
            cb = _dot_nt(cg, bg).astype(BF16)
            b_t = bg.astype(F32).T.astype(BF16)
            s_prev = s_ref[d, g]
            y_off = _dot(cg, s_prev.astype(BF16))
            for p in range(PAIRS_PER_GROUP):
                h0 = g * (SSD_HEADS // SSD_GROUPS) + 2 * p
                c0 = h0 * SSD_HEAD_DIM
                xpair = x_ref[rs, c0:c0 + LANES]
                zero = jnp.zeros_like(xpair)
                y_diag = None
                st = None
                a_cols = []
                for t, xm in enumerate((jnp.where(lo, xpair, zero), jnp.where(lo, zero, xpair))):
                    h = off + h0 + t
                    a_col = jnp.broadcast_to(a[:, h:h + 1], (Q, Q))
                    a_cols.append(a_col)
                    seg = a_col - a_t[h:h + 1, :]
                    decay = jnp.exp2(jnp.where(tri, seg, -jnp.inf)).astype(BF16)
                    yd = _dot(decay * cb * dt_tb[h:h + 1, :], xm)
                    sd = _dot(b_t * w_t[h:h + 1, :], xm)
                    y_diag = yd if y_diag is None else y_diag + yd
                    st = sd if st is None else st + sd
                h = off + h0
                ea_pair = jnp.exp2(jnp.where(lo, a_cols[0], a_cols[1]))
                ps = slice(p * LANES, (p + 1) * LANES)
                y_ref[rs, c0:c0 + LANES] = (y_diag + y_off[:, ps] * ea_pair).astype(y_ref.dtype)
                dec_pair = jnp.where(lo_row, dec[h:h + 1, :], dec[h + 1:h + 2, :])
                s_ref[d, g, :, ps] = s_prev[:, ps] * dec_pair + st


def _ssd(xc, dt, s0, bias, alog):
    L = xc.shape[0]
    rows = SSD_CHUNKS_PER_STEP * SSD_CHUNK
    nc = L // rows
    sshape = (2, SSD_GROUPS, SSD_STATE, GROUP_W)
    return pl.pallas_call(
        _ssd_kernel,
        grid=(nc,),
        in_specs=[pl.BlockSpec((rows, CONV_CH), lambda c: (c, 0)),
                  pl.BlockSpec((rows, CONV_CH), lambda c: (nc - 1 - c, 0)),
                  pl.BlockSpec((rows, LANES), lambda c: (c, 0)),
                  pl.BlockSpec((rows, LANES), lambda c: (nc - 1 - c, 0)),
                  pl.BlockSpec(sshape, lambda c: (0, 0, 0, 0)),
                  pl.BlockSpec((1, LANES), lambda c: (0, 0)),
                  pl.BlockSpec((1, LANES), lambda c: (0, 0))],
        out_specs=[pl.BlockSpec((rows, D_INNER), lambda c: (c, 0)),
                   pl.BlockSpec((rows, D_INNER), lambda c: (nc - 1 - c, 0)),
                   pl.BlockSpec(sshape, lambda c: (0, 0, 0, 0))],
        out_shape=[jax.ShapeDtypeStruct((L, D_INNER), BF16),
                   jax.ShapeDtypeStruct((L, D_INNER), BF16),
                   jax.ShapeDtypeStruct(sshape, F32)],
        compiler_params=_params("arbitrary"),
        name="ssd",
    )(xc, xc, dt, dt, s0, bias, alog)


ATTN_BLOCKS_PER_STEP = 4


def _attn_kernel(sink_ref, q_ref, kp_ref, kc_ref, kn_ref, vp_ref, vc_ref, vn_ref, kx_ref, vx_ref, o_ref, *, s_ctx):
    i = pl.program_id(0)
    last = pl.num_programs(0) - 1
    B = ATTN_BLOCK
    nsub = ATTN_BLOCKS_PER_STEP
    span = 3 * B
    width = span + s_ctx
    t = lax.broadcasted_iota(I32, (B, width), 0)
    j = lax.broadcasted_iota(I32, (B, width), 1)
    in_window = jnp.abs(j - B - t) <= WINDOW
    lo = lax.broadcasted_iota(I32, (B, LANES), 1) < 64
    group = ATTN_Q_HEADS // ATTN_KV_HEADS

    for sb in range(nsub):
        rows = slice(sb * B, (sb + 1) * B)
        j_min = jnp.where(i > 0, 0, B) if sb == 0 else 0
        j_max = jnp.where(i < last, span, 2 * B) if sb == nsub - 1 else span
        valid = (j >= span) | (in_window & (j >= j_min) & (j < j_max))

        def window(prev_ref, cur_ref, next_ref, ctx_ref, sl):
            before = prev_ref[:, sl] if sb == 0 else cur_ref[(sb - 1) * B:sb * B, sl]
            after = next_ref[:, sl] if sb == nsub - 1 else cur_ref[(sb + 1) * B:(sb + 2) * B, sl]
            return jnp.concatenate([before, cur_ref[rows, sl], after, ctx_ref[:, sl]], axis=0)

        for kh in range(ATTN_KV_HEADS):
            sl = slice(kh * LANES, (kh + 1) * LANES)
            k_all = window(kp_ref, kc_ref, kn_ref, kx_ref, sl)
            v_all = window(vp_ref, vc_ref, vn_ref, vx_ref, sl)
            q_rows = []
            for u in range(group // 2):
                c = (group // 2) * kh + u
                qpair = q_ref[rows, c * LANES:(c + 1) * LANES]
                zero_q = jnp.zeros_like(qpair)
                q_rows += [jnp.where(lo, qpair, zero_q), jnp.where(lo, zero_q, qpair)]
            s_all = _dot_nt(jnp.concatenate(q_rows, axis=0), k_all)
            e_rows, inv = [], []
            for b in range(group):
                sink = sink_ref[group * kh + b] * LOG2_E
                s = jnp.where(valid, s_all[b * B:(b + 1) * B, :], -jnp.inf)
                m = jnp.maximum(jnp.max(s, axis=1, keepdims=True), sink)
                e = jnp.exp2(s - m)
                inv.append(1.0 / (jnp.sum(e, axis=1, keepdims=True) + jnp.exp2(sink - m)))
                e_rows.append(e.astype(BF16))
            o_all = _dot(jnp.concatenate(e_rows, axis=0), v_all)
            for u in range(group // 2):
                c = (group // 2) * kh + u
                b0 = 2 * u
                o = jnp.where(lo, o_all[b0 * B:(b0 + 1) * B, :] * inv[b0],
                              o_all[(b0 + 1) * B:(b0 + 2) * B, :] * inv[b0 + 1])
                o_ref[rows, c * LANES:(c + 1) * LANES] = o.astype(o_ref.dtype)


def _attn(sink, q, kd, vd, kx, vx):
    L = q.shape[0]
    s_ctx = kx.shape[0]
    nsub = ATTN_BLOCKS_PER_STEP
    nb = L // ATTN_BLOCK
    prev = lambda i: (jnp.maximum(nsub * i - 1, 0), 0)
    cur = lambda i: (i, 0)
    nxt = lambda i: (jnp.minimum(nsub * (i + 1), nb - 1), 0)
    edge = lambda f: pl.BlockSpec((ATTN_BLOCK, KV_DUP), f)
    body = pl.BlockSpec((nsub * ATTN_BLOCK, KV_DUP), cur)
    return pl.pallas_call(
        functools.partial(_attn_kernel, s_ctx=s_ctx),
        grid=(nb // nsub,),
        in_specs=[pl.BlockSpec(memory_space=pltpu.SMEM),
                  pl.BlockSpec((nsub * ATTN_BLOCK, ATTN_WIDTH), cur),
                  edge(prev), body, edge(nxt), edge(prev), body, edge(nxt),
                  pl.BlockSpec((s_ctx, KV_DUP), lambda i: (0, 0)),
                  pl.BlockSpec((s_ctx, KV_DUP), lambda i: (0, 0))],
        out_specs=pl.BlockSpec((nsub * ATTN_BLOCK, ATTN_WIDTH), cur),
        out_shape=jax.ShapeDtypeStruct((L, ATTN_WIDTH), BF16),
        compiler_params=_params("parallel"),
        name="attn",
    )(sink, q, kd, kd, kd, vd, vd, vd, kx, vx)


OUT_CHUNK = 256


def _out_kernel(yf_ref, yb_ref, xs_ref, z_ref, g_ref, ya_ref, x_ref, dsk_ref, nw_ref, wssd_ref, wattn_ref, wo_ref,
                bo_ref, g1_ref, ln1g_ref, ln1b_ref, sh2_ref, sc2_ref, wrh_ref, wrl_ref, x1_ref, h2_ref, aff_ref):
    br_attn = _dot(ya_ref[...], wattn_ref[...])
    sumsq = None
    br_ssd = None
    for c0 in range(0, D_INNER, OUT_CHUNK):
        cs = slice(c0, c0 + OUT_CHUNK)
        y = yf_ref[:, cs].astype(F32) + yb_ref[:, cs].astype(F32) + dsk_ref[:, cs] * xs_ref[:, cs].astype(F32)
        y = y * _silu(z_ref[:, cs].astype(F32))
        sq = jnp.sum(y * y, -1, keepdims=True)
        part = _dot((y * nw_ref[:, cs]).astype(BF16), wssd_ref[cs, :])
        sumsq = sq if sumsq is None else sumsq + sq
        br_ssd = part if br_ssd is None else br_ssd + part
    br_ssd = br_ssd * lax.rsqrt(sumsq * (1.0 / D_INNER) + LN_EPS)
    g_ssd = _sigmoid(g_ref[:, :D_MODEL].astype(F32))
    g_attn = _sigmoid(g_ref[:, D_MODEL:].astype(F32))
    m = g_ssd * br_ssd + g_attn * br_attn
    o = _dot(m.astype(BF16), wo_ref[...]) + bo_ref[...]
    x1 =_ln(DEEPNORM_ALPHA * x_ref[...] + g1_ref[...] * o) * ln1g_ref[...] + ln1b_ref[...]
    x1_ref[...] = x1
    h2 = _ln(x1) * (1.0 + sc2_ref[...]) + sh2_ref[...]
    h2_ref[...] = h2.astype(h2_ref.dtype)
    h_hi = h2.astype(BF16)
    h_lo = (h2 - h_hi.astype(F32)).astype(BF16)
    logits = _dot_nt(h_hi, wrh_ref[...]) + (_dot_nt(h_hi, wrl_ref[...]) + _dot_nt(h_lo, wrh_ref[...]))
    lane = lax.broadcasted_iota(I32, logits.shape, 1)
    logits = jnp.where(lane < N_EXPERTS, logits, -jnp.inf)
    e = jnp.exp(logits - jnp.max(logits, axis=1, keepdims=True))
    aff = e / jnp.sum(e, axis=1, keepdims=True)
    aff_ref[...] = aff.T[:N_EXPERTS, :]


def _out(yf, yb, xc, p_main, ya, x, dsk, nw, wssd, wattn, wo, bo, g1, ln1g, ln1b, sh2, sc2, wrh, wrl, tm):
    L = x.shape[0]
    row = lambda w: pl.BlockSpec((1, w), lambda i: (0, 0))
    full = lambda a: pl.BlockSpec(a.shape, lambda i: (0, 0))
    return pl.pallas_call(
        _out_kernel,
        grid=(L // tm,),
        in_specs=[pl.BlockSpec((tm, D_INNER), lambda i: (i, 0)),
                  pl.BlockSpec((tm, D_INNER), lambda i: (i, 0)),
                  pl.BlockSpec((tm, D_INNER), lambda i: (i, 0)),
                  pl.BlockSpec((tm, D_INNER), lambda i: (i, OFF_Z // D_INNER)),
                  pl.BlockSpec((tm, 2 * D_MODEL), lambda i: (i, OFF_G // (2 * D_MODEL))),
                  pl.BlockSpec((tm, ATTN_WIDTH), lambda i: (i, 0)),
                  pl.BlockSpec((tm, D_MODEL), lambda i: (i, 0)),
                  row(D_INNER), row(D_INNER), full(wssd), full(wattn), full(wo),
                  row(D_MODEL), row(D_MODEL), row(D_MODEL), row(D_MODEL), row(D_MODEL), row(D_MODEL),
                  full(wrh), full(wrl)],
        out_specs=[pl.BlockSpec((tm, D_MODEL), lambda i: (i, 0)),
                   pl.BlockSpec((tm, D_MODEL), lambda i: (i, 0)),
                   pl.BlockSpec((N_EXPERTS, tm), lambda i: (0, i))],
        out_shape=[jax.ShapeDtypeStruct((L, D_MODEL), F32),
                   jax.ShapeDtypeStruct((L, D_MODEL), BF16),
                   jax.ShapeDtypeStruct((N_EXPERTS, L), F32)],
        compiler_params=_params("parallel"),
        name="out_proj",
    )(yf, yb, xc, p_main, p_main, ya, x, dsk, nw, wssd, wattn, wo, bo, g1, ln1g, ln1b, sh2, sc2, wrh, wrl)


def _count(mask):
    c = jnp.sum(jnp.where(mask, 1.0, 0.0), axis=1, keepdims=True)
    return jnp.sum(c, axis=0, keepdims=True)


def _route_kernel(aff_ref, rank_ref, start_ref, cnt_ref, sel_ref, *, cap, nrows):
    E = N_EXPERTS
    R = nrows
    tok = (lax.broadcasted_iota(I32, (R, LANES), 0) * LANES + lax.broadcasted_iota(I32, (R, LANES), 1))
    capf = jnp.float32(cap)

    def as_f32(word):
        return pltpu.bitcast(word, F32)

    def thr_body(it, ts):
        bit = lax.shift_left(jnp.int32(1), 30 - it)
        out = []
        for e in range(E):
            cand = ts[e] | bit
            out.append(jnp.where(_count(aff_ref[e] >= as_f32(cand)) >= capf, cand, ts[e]))
        return tuple(out)

    ts = lax.fori_loop(0, 31, thr_body, tuple(jnp.zeros((1, 1), I32) for _ in range(E)))
    thr = [as_f32(t) for t in ts]
    need = [capf - _count(aff_ref[e] > thr[e]) for e in range(E)]

    nbits = max((R * LANES - 1).bit_length(), 1)

    def tie_body(it, js):
        bit = lax.shift_left(jnp.int32(1), nbits - 1 - it)
        out = []
        for e in range(E):
            cand = js[e] | bit
            f = _count((aff_ref[e] == thr[e]) & (tok < cand))
            out.append(jnp.where(f < need[e], cand, js[e]))
        return tuple(out)

    js = lax.fori_loop(0, nbits, tie_body, tuple(jnp.zeros((1, 1), I32) for _ in range(E)))
    for e in range(E):
        a = aff_ref[e]
        sel = (a > thr[e]) | ((a == thr[e]) & (tok <= js[e]))
        sel_ref[e] = jnp.where(sel, 1.0, 0.0)

    li = lax.broadcasted_iota(I32, (LANES, LANES), 0)
    lj = lax.broadcasted_iota(I32, (LANES, LANES), 1)
    upper_incl = jnp.where(li <= lj, 1.0, 0.0).astype(BF16)
    ri = lax.broadcasted_iota(I32, (R, R), 0)
    rj = lax.broadcasted_iota(I32, (R, R), 1)
    row_lower = jnp.where(ri >= rj, 1.0, 0.0).astype(BF16)
    lane_id = lax.broadcasted_iota(I32, (R, LANES), 1)

    def expert_body(e, carry):
        start_acc, cnt_acc = carry
        sel = sel_ref[e]
        w = _dot(sel.astype(BF16), upper_incl)
        tot_b = jnp.broadcast_to(jnp.sum(sel, axis=1, keepdims=True), (R, LANES))
        seg_b = jnp.floor((tot_b + (SUBLANES - 1)) * (1.0 / SUBLANES)) * SUBLANES
        cum_incl = _dot(row_lower, seg_b.astype(BF16))
        rank_ref[e] = jnp.where(sel > 0.0, w - 1.0, -1.0)
        start_acc = jnp.where(lane_id == e, cum_incl - seg_b, start_acc)
        cnt_acc = jnp.where(lane_id == e, tot_b, cnt_acc)
        return start_acc, cnt_acc

    zero = jnp.zeros((R, LANES), F32)
    start_acc, cnt_acc = lax.fori_loop(0, E, expert_body, (zero, zero))
    start_ref[...] = start_acc.astype(I32)
    cnt_ref[...] = cnt_acc.astype(I32)


def _route(aff3, cap):
    E, R, _ = aff3.shape
    return pl.pallas_call(
        functools.partial(_route_kernel, cap=cap, nrows=R),
        out_shape=[jax.ShapeDtypeStruct((E, R, LANES), F32),
                   jax.ShapeDtypeStruct((R, LANES), I32),
                   jax.ShapeDtypeStruct((R, LANES), I32)],
        scratch_shapes=[pltpu.VMEM((E, R, LANES), F32)],
        compiler_params=pltpu.CompilerParams(vmem_limit_bytes=VMEM_LIMIT_BYTES),
        name="route",
    )(aff3)


SLOTS = 32
WIN = N_EXPERTS * SLOTS
DISPATCH_TILES = 2
D_SLOTS = DISPATCH_TILES * SLOTS
D_WIN = N_EXPERTS * D_SLOTS


def _slot_onehot(tgt, values, k, slots=SLOTS):
    sub = lax.broadcasted_iota(I32, (slots, tgt.shape[1]), 0).astype(F32) + (slots * k).astype(F32)
    rows = []
    for e in range(N_EXPERTS):
        hit = jnp.broadcast_to(tgt[e:e + 1, :], sub.shape) == sub
        rows.append(jnp.where(hit, jnp.broadcast_to(values[e:e + 1, :], sub.shape), 0.0))
    return jnp.concatenate(rows, axis=0).astype(BF16)


def _windows(n, slots):
    return lax.shift_right_logical(n + (slots - 1), slots.bit_length() - 1)


def _rounds(cnt_ref, r):
    n = cnt_ref[0, r]
    for e in range(1, N_EXPERTS):
        n = jnp.maximum(n, cnt_ref[e, r])
    return _windows(n, SLOTS)


def _dispatch_kernel(start_ref, cnt_ref, h_ref, rank_ref, xs_hbm, stage_ref, ovf_ref, sem, osem):
    p = pl.program_id(0)
    last = pl.num_programs(0) - 1
    slot = lax.rem(p, 2)
    r0 = p * DISPATCH_TILES
    r_end = r0 + DISPATCH_TILES - 1

    def rows_used(e):
        return start_ref[e, r_end] - start_ref[e, r0] + cnt_ref[e, r_end]

    e_id = lax.broadcasted_iota(I32, (N_EXPERTS, LANES), 0)
    tgts = []
    for b in range(DISPATCH_TILES):
        rank = rank_ref[:, b, 0, :]
        off = jnp.zeros_like(rank)
        if b > 0:
            for e in range(N_EXPERTS):
                d = start_ref[e, r0 + b] - start_ref[e, r0]
                off = jnp.where(e_id == e, d.astype(F32), off)
        tgts.append(jnp.where(rank >= 0.0, rank + off, -1.0))
    tgt = jnp.concatenate(tgts, axis=1)
    ones = jnp.ones_like(tgt)

    def compact(k):
        return _dot(_slot_onehot(tgt, ones, k, D_SLOTS), h_ref[...])

    def window_copy(src_ref, e, k, s):
        dst = pl.multiple_of(start_ref[e, r0] + D_SLOTS * k, SUBLANES)
        return pltpu.make_async_copy(src_ref.at[pl.ds(e * D_SLOTS, D_SLOTS), :],
                                     xs_hbm.at[e, pl.ds(dst, D_SLOTS), :], s)

    def stage_wait(sl):
        pltpu.make_async_copy(stage_ref.at[sl], stage_ref.at[sl], sem.at[sl]).wait()

    stage_ref[slot] = compact(jnp.int32(0))

    @pl.when(p > 0)
    def _():
        stage_wait(1 - slot)

    for e in range(N_EXPERTS):
        window_copy(stage_ref.at[slot], e, 0, sem.at[slot]).start()

    def extra(k, _):
        ovf_ref[...] = compact(k)
        for e in range(N_EXPERTS):
            @pl.when(rows_used(e) > D_SLOTS * k)
            def _():
                window_copy(ovf_ref, e, k, osem).start()
        for e in range(N_EXPERTS):
            @pl.when(rows_used(e) > D_SLOTS * k)
            def _():
                window_copy(ovf_ref, e, k, osem).wait()
        return 0

    n_max = rows_used(0)
    for e in range(1, N_EXPERTS):
        n_max = jnp.maximum(n_max, rows_used(e))
    lax.fori_loop(1, _windows(n_max, D_SLOTS), extra, 0)

    def tail_copies(act):
        n_rows = xs_hbm.shape[1]
        for e in range(N_EXPERTS):
            off0 = start_ref[e, r0] + D_SLOTS * jnp.maximum(_windows(rows_used(e), D_SLOTS), 1)
            length = n_rows - off0
            n_full = lax.div(length, D_WIN)

            def full(i, _, e=e, off0=off0):
                dst = pl.multiple_of(off0 + i * D_WIN, SUBLANES)
                act(pltpu.make_async_copy(ovf_ref, xs_hbm.at[e, pl.ds(dst, D_WIN), :], osem))
                return 0

            lax.fori_loop(0, n_full, full, 0)
            off = off0 + n_full * D_WIN
            rem = length - n_full * D_WIN
            size = D_WIN // 2
            while size >= SUBLANES:
                @pl.when((rem & size) != 0)
                def _(e=e, off=off, size=size):
                    act(pltpu.make_async_copy(ovf_ref.at[pl.ds(0, size), :],
                                              xs_hbm.at[e, pl.ds(pl.multiple_of(off, SUBLANES), size), :], osem))
                off = off + (rem & size)
                size //= 2

    @pl.when(p == last)
    def _():
        stage_wait(slot)
        ovf_ref[...] = jnp.zeros(ovf_ref.shape, F32)
        tail_copies(lambda cp: cp.start())
        tail_copies(lambda cp: cp.wait())


def _padded_cap(cap, n_tiles, ts):
    worst = cap + (SUBLANES - 1) * n_tiles
    return -(-worst // ts) * ts


def _dispatch(start, cnt, h2, rank4, cap):
    L = h2.shape[0]
    T = DISPATCH_TILES * LANES
    return pl.pallas_call(
        _dispatch_kernel,
        grid_spec=pltpu.PrefetchScalarGridSpec(
            num_scalar_prefetch=2,
            grid=(L // T,),
            in_specs=[pl.BlockSpec((T, D_MODEL), lambda p, *_: (p, 0)),
                      pl.BlockSpec((N_EXPERTS, DISPATCH_TILES, 1, LANES), lambda p, *_: (0, p, 0, 0))],
            out_specs=pl.BlockSpec(memory_space=pl.ANY),
            scratch_shapes=[pltpu.VMEM((2, D_WIN, D_MODEL), F32),
                            pltpu.VMEM((D_WIN, D_MODEL), F32),
                            pltpu.SemaphoreType.DMA((2,)),
                            pltpu.SemaphoreType.DMA(())]),
        out_shape=jax.ShapeDtypeStruct((N_EXPERTS, cap + D_SLOTS, D_MODEL), F32),
        compiler_params=_params("arbitrary"),
        name="dispatch",
    )(start, cnt, h2, rank4)


def _expert_rows(start_ref, cnt_ref, e):
    last_tile = start_ref.shape[1] - 1
    shift = SUBLANES.bit_length() - 1
    seg = lax.shift_left(lax.shift_right_logical(cnt_ref[e, last_tile] + (SUBLANES - 1), shift), shift)
    return start_ref[e, last_tile] + seg


def _expert_kernel(start_ref, cnt_ref, xs_ref, wgu_hbm, wd_hbm, o_ref, wgu_f32, wd_f32, wgu_bf, wd_bf, sem, *, ts):
    e = pl.program_id(0)
    j = pl.program_id(1)
    n_experts = pl.num_programs(0)
    live = _expert_rows(start_ref, cnt_ref, e) - j * ts

    def weight_copies(ee, slot):
        return (pltpu.make_async_copy(wgu_hbm.at[ee], wgu_f32.at[slot], sem.at[0, slot]),
                pltpu.make_async_copy(wd_hbm.at[ee], wd_f32.at[slot], sem.at[1, slot]))

    @pl.when(j == 0)
    def _():
        slot = lax.rem(e, 2)

        @pl.when(e == 0)
        def _():
            for cp in weight_copies(e, slot):
                cp.start()

        for cp in weight_copies(e, slot):
            cp.wait()

        @pl.when(e + 1 < n_experts)
        def _():
            for cp in weight_copies(e + 1, 1 - slot):
                cp.start()

        rows = 128
        for r0 in range(0, D_MODEL, rows):
            wgu_bf[r0:r0 + rows, :] = wgu_f32[slot, r0:r0 + rows, :].astype(BF16)
        for r0 in range(0, EXPERT_FF, rows):
            wd_bf[r0:r0 + rows, :] = wd_f32[slot, r0:r0 + rows, :].astype(BF16)

    @pl.when(live > 0)
    def _():
        row = lax.broadcasted_iota(I32, (ts, D_MODEL), 0)
        xs = jnp.where(row < live, xs_ref[0], 0.0).astype(BF16)
        gu = _dot(xs, wgu_bf[...])
        act = _silu(gu[:, :EXPERT_FF]) * gu[:, EXPERT_FF:]
        o_ref[0] = _dot(act.astype(BF16), wd_bf[...])

    @pl.when(live <= 0)
    def _():
        o_ref[0] = jnp.zeros((ts, D_MODEL), F32)


def _expert(start, cnt, xs, wgu, wd, capp, ts):
    E = xs.shape[0]
    return pl.pallas_call(
        functools.partial(_expert_kernel, ts=ts),
        grid_spec=pltpu.PrefetchScalarGridSpec(
            num_scalar_prefetch=2,
            grid=(E, capp // ts),
            in_specs=[pl.BlockSpec((1, ts, D_MODEL), lambda e, j, start, cnt: (
                          e, jnp.minimum(j, jnp.maximum(_expert_rows(start, cnt, e) - 1, 0) // ts), 0)),
                      pl.BlockSpec(memory_space=pl.ANY),
                      pl.BlockSpec(memory_space=pl.ANY)],
            out_specs=pl.BlockSpec((1, ts, D_MODEL), lambda e, j, *_: (e, j, 0)),
            scratch_shapes=[pltpu.VMEM((2, D_MODEL, 2 * EXPERT_FF), F32),
                            pltpu.VMEM((2, EXPERT_FF, D_MODEL), F32),
                            pltpu.VMEM((D_MODEL, 2 * EXPERT_FF), BF16),
                            pltpu.VMEM((EXPERT_FF, D_MODEL), BF16),
                            pltpu.SemaphoreType.DMA((2, 2))]),
        out_shape=jax.ShapeDtypeStruct((E, capp, D_MODEL), F32),
        compiler_params=_params("arbitrary", "arbitrary"),
        name="expert_ffn",
    )(start, cnt, xs, wgu, wd)


COMBINE_TILES = 4


def _combine_kernel(start_ref, cnt_ref, ye_hbm, x1_ref, aff_ref, rank_ref, g2_ref, lng_ref, lnb_ref,
                    o_ref, buf_ref, ext_ref, sem, esem, *, cap):
    p = pl.program_id(0)
    last = pl.num_programs(0) - 1
    slot = lax.rem(p, 2)
    T = LANES

    def win_start(e, r, k):
        return pl.multiple_of(jnp.minimum(start_ref[e, r] + SLOTS * k, cap - SLOTS), SUBLANES)

    def window_copies(dst_ref, r, k, s):
        return [pltpu.make_async_copy(ye_hbm.at[e, pl.ds(win_start(e, r, k), SLOTS), :],
                                      dst_ref.at[pl.ds(e * SLOTS, SLOTS), :], s) for e in range(N_EXPERTS)]

    def fetch(step, sl):
        for b in range(COMBINE_TILES):
            for cp in window_copies(buf_ref.at[sl, b], step * COMBINE_TILES + b, 0, sem.at[sl]):
                cp.start()

    @pl.when(p == 0)
    def _():
        fetch(p, slot)

    @pl.when(p < last)
    def _():
        fetch(p + 1, 1 - slot)

    pltpu.make_async_copy(buf_ref.at[slot], buf_ref.at[slot], sem.at[slot]).wait()

    for b in range(COMBINE_TILES):
        r = p * COMBINE_TILES + b
        rank = rank_ref[:, b, 0, :]
        aff = aff_ref[:, b * T:(b + 1) * T]
        g_hi = aff.astype(BF16).astype(F32)
        g_lo = aff - g_hi
        e_id = lax.broadcasted_iota(I32, rank.shape, 0)

        def contribution(k, src_ref, r=r, rank=rank, g_hi=g_hi, g_lo=g_lo, e_id=e_id):
            shift = jnp.zeros_like(rank)
            for e in range(N_EXPERTS):
                d = start_ref[e, r] + SLOTS * k - win_start(e, r, k)
                shift = jnp.where(e_id == e, d.astype(F32), shift)
            base = (SLOTS * k).astype(F32)
            in_round = (rank >= base) & (rank < base + SLOTS)
            tgt = jnp.where(in_round, rank + shift, -1.0)
            yb = src_ref[...].astype(BF16)
            tn = (((0,), (0,)), ((), ()))
            return (lax.dot_general(_slot_onehot(tgt, g_hi, k), yb, tn, preferred_element_type=F32)
                    + lax.dot_general(_slot_onehot(tgt, g_lo, k), yb, tn, preferred_element_type=F32))

        def extra(k, acc, r=r, contribution=contribution):
            for cp in window_copies(ext_ref, r, k, esem):
                cp.start()
            pltpu.make_async_copy(ext_ref, ext_ref, esem).wait()
            return acc + contribution(k, ext_ref)

        moe = lax.fori_loop(1, _rounds(cnt_ref, r), extra, contribution(jnp.int32(0), buf_ref.at[slot, b]))
        rows = slice(b * T, (b + 1) * T)
        y = _ln(DEEPNORM_ALPHA * x1_ref[rows, :] + g2_ref[...] * moe)
        o_ref[rows, :] = y * lng_ref[...] + lnb_ref[...]


def _combine(start, cnt, ye, x1, aff_t, rank4, g2, lng, lnb):
    L = x1.shape[0]
    E, cap, _ = ye.shape
    T = COMBINE_TILES * LANES
    row = pl.BlockSpec((1, D_MODEL), lambda p, *_: (0, 0))
    return pl.pallas_call(
        functools.partial(_combine_kernel, cap=cap),
        grid_spec=pltpu.PrefetchScalarGridSpec(
            num_scalar_prefetch=2,
            grid=(L // T,),
            in_specs=[pl.BlockSpec(memory_space=pl.ANY),
                      pl.BlockSpec((T, D_MODEL), lambda p, *_: (p, 0)),
                      pl.BlockSpec((E, T), lambda p, *_: (0, p)),
                      pl.BlockSpec((E, COMBINE_TILES, 1, LANES), lambda p, *_: (0, p, 0, 0)),
                      row, row, row],
            out_specs=pl.BlockSpec((T, D_MODEL), lambda p, *_: (p, 0)),
            scratch_shapes=[pltpu.VMEM((2, COMBINE_TILES, WIN, D_MODEL), F32),
                            pltpu.VMEM((WIN, D_MODEL), F32),
                            pltpu.SemaphoreType.DMA((2,)),
                            pltpu.SemaphoreType.DMA(())]),
        out_shape=jax.ShapeDtypeStruct((L, D_MODEL), F32),
        compiler_params=_params("arbitrary"),
        name="combine",
    )(start, cnt, ye, x1, aff_t, rank4, g2, lng, lnb)


def _rope_tables(L):
    inv_freq = ROPE_BASE ** (-jnp.arange(0, ROPE_AXIS_DIM, 2, dtype=F32) / ROPE_AXIS_DIM)

    def table(n, on_row_lanes):
        ang = jnp.arange(n, dtype=F32)[:, None] * inv_freq[None, :]
        one, zero = jnp.ones_like(ang), jnp.zeros_like(ang)
        mask = jnp.concatenate(([one, one, zero, zero] if on_row_lanes else [zero, zero, one, one]) * 2, -1)
        ang = jnp.concatenate([ang] * 8, -1)
        return jnp.stack([jnp.cos(ang) * mask, jnp.sin(ang) * mask])

    return table(L // GRID_W, True), table(GRID_W, False)


def _row_tile(L, pref):
    return pref if L % pref == 0 else L


def kernel(x, c, ctx, c_ctx, w_ada, b_ada, w_in, conv_w, conv_b, a_log, dt_bias, d_skip, ssd_norm_w, attn_sink,
           w_ssd_br, w_attn_br, w_o, b_o, ln1_g, ln1_b, w_router, w_gate_up, w_down, ln2_g, ln2_b):
    assert x.shape[0] == 1 and w_in.shape[0] == 1, "single batch element, depth 1"
    L = x.shape[1]
    S = ctx.shape[1]
    assert L % (LANES * SUBLANES) == 0 and S % (SSD_CHUNKS_PER_STEP * SSD_CHUNK) == 0
    cap = CAPACITY_FACTOR * L // N_EXPERTS
    x2, ctx2 = x[0], ctx[0]

    cc = jnp.zeros((SUBLANES, D_MODEL), F32).at[0].set(c[0]).at[1].set(c_ctx)
    mod = _ada(cc, w_ada[0], b_ada[0][None, :])
    sh1, sc1, g1, sh2, sc2, g2 = [mod[0:1, k * D_MODEL:(k + 1) * D_MODEL] for k in range(6)]
    csh1, csc1 = mod[1:2, 0:D_MODEL], mod[1:2, D_MODEL:2 * D_MODEL]

    w = w_in[0]
    o = 0
    parts = {}
    for name, width in (("z", D_INNER), ("xbc", CONV_CH), ("dt", 2 * SSD_HEADS), ("q", ATTN_WIDTH),
                        ("k", KV_WIDTH), ("v", KV_WIDTH), ("g", 2 * D_MODEL)):
        parts[name] = w[:, o:o + width]
        o += width
    w_main = jnp.concatenate([parts[n] for n in ("z", "g", "xbc", "q", "k", "v")], axis=1).astype(BF16)
    w_dt = jnp.pad(parts["dt"], ((0, 0), (0, LANES - 2 * SSD_HEADS))).astype(BF16)

    pad_heads = lambda v: jnp.pad(v.reshape(1, 2 * SSD_HEADS), ((0, 0), (0, LANES - 2 * SSD_HEADS)))
    bias_row, alog_row = pad_heads(dt_bias[0]), pad_heads(a_log[0])
    conv_b2 = conv_b[0][None, :]

    tm_c = _row_tile(S, 256)
    pc, dtc = _ln_proj(ctx2, csh1, csc1, w_main, w_dt, tm_c)
    xcc = _conv(pc, conv_w[0], conv_b2, tm_c)
    kx, vx = _qkv(pc, None, None, tm_c, rope=False, with_q=False)
    s_zero = jnp.zeros((2, SSD_GROUPS, SSD_STATE, GROUP_W), F32)
    _, _, s_ctx = _ssd(xcc, dtc, s_zero, bias_row, alog_row)

    p_main, dt = _ln_proj(x2, sh1, sc1, w_main, w_dt, _row_tile(L, 512))
    xc = _conv(p_main, conv_w[0], conv_b2, _row_tile(L, 1024))
    row_tab, col_tab = _rope_tables(L)
    q, kd, vd = _qkv(p_main, row_tab, col_tab, _row_tile(L, 512), rope=True, with_q=True)
    yf, yb, _ = _ssd(xc, dt, s_ctx, bias_row, alog_row)
    ya = _attn(attn_sink[0], q, kd, vd, kx, vx)

    dsk = jnp.repeat(d_skip[0], SSD_HEAD_DIM)[None, :]
    wr = jnp.pad(w_router[0].T, ((0, LANES - N_EXPERTS), (0, 0)))
    wr_hi = wr.astype(BF16)
    wr_lo = (wr - wr_hi.astype(F32)).astype(BF16)
    x1, h2, aff_t = _out(yf, yb, xc, p_main, ya, x2, dsk, ssd_norm_w[0][None, :],
                         w_ssd_br[0].astype(BF16), w_attn_br[0].astype(BF16), w_o[0].astype(BF16),
                         b_o[0][None, :], g1, ln1_g[0][None, :], ln1_b[0][None, :], sh2, sc2, wr_hi, wr_lo,
                         _row_tile(L, 256))

    R = L // LANES
    rank, start_t, cnt_t = _route(aff_t.reshape(N_EXPERTS, R, LANES), cap)
    rank4 = rank.reshape(N_EXPERTS, R, 1, LANES)
    start = start_t[:, :N_EXPERTS].T
    cnt = cnt_t[:, :N_EXPERTS].T
    ts = min(cap, 512)
    capp = _padded_cap(cap, R, ts)
    xs = _dispatch(start, cnt, h2, rank4, capp)
    ye = _expert(start, cnt, xs, w_gate_up[0], w_down[0], capp, ts)
    out = _combine(start, cnt, ye, x1, aff_t, rank4, g2, ln2_g[0][None, :], ln2_b[0][None, :])
    return out[None]
```
